```python
import jax, jax.numpy as jnp
from jax import lax
import numpy as np

D_MODEL = 1024
BATCH = 8
SEQ = 2048
DEPTH = 1

GDN_HEADS = 8
GDN_HEAD_DIM = 64
FOX_HEADS = 8
FOX_HEAD_DIM = 64
GDN_WIDTH = GDN_HEADS * GDN_HEAD_DIM
FOX_WIDTH = FOX_HEADS * FOX_HEAD_DIM
D_MIX = GDN_WIDTH + FOX_WIDTH
CONV_K = 4
CHUNK = 64
Q_BLOCK = 128
D_FF = -(-8 * D_MODEL // (3 * 256)) * 256
EPS = 1e-6

SPLIT_SIZES = [
    GDN_WIDTH, GDN_WIDTH, GDN_WIDTH,
    GDN_WIDTH,
    GDN_HEADS, GDN_HEADS,
    FOX_WIDTH, FOX_WIDTH, FOX_WIDTH,
    FOX_WIDTH,
    FOX_HEADS,
]
D_IN = sum(SPLIT_SIZES)
SPLIT_POINTS = list(np.cumsum(SPLIT_SIZES)[:-1])

kernel_name = "hymba_gdn_fox_swiglu"


def rms_norm(x, w):
    xf = x.astype(jnp.float32)
    out = xf * lax.rsqrt(jnp.mean(xf * xf, axis=-1, keepdims=True) + EPS)
    return (out * w.astype(jnp.float32)).astype(x.dtype)


def l2_norm(x):
    xf = x.astype(jnp.float32)
    return xf * lax.rsqrt(jnp.sum(xf * xf, axis=-1, keepdims=True) + EPS)


def causal_depthwise_conv(x, w):
    c = x.shape[-1]
    return lax.conv_general_dilated(
        x, w.reshape(CONV_K, 1, c).astype(x.dtype), window_strides=(1,),
        padding=[(CONV_K - 1, 0)], dimension_numbers=("NWC", "WIO", "NWC"),
        feature_group_count=c)


def gated_delta_rule(q, k, v, beta, g):
    B, T, H, Dk = q.shape
    Dv = v.shape[-1]
    N = T // CHUNK

    def chunks(t):
        return t.reshape(B, N, CHUNK, H, -1).transpose(0, 3, 1, 2, 4)

    q = chunks(q.astype(jnp.float32)) * (Dk ** -0.5)
    k = chunks(k.astype(jnp.float32))
    v = chunks(v.astype(jnp.float32))
    beta = beta.astype(jnp.float32).reshape(B, N, CHUNK, H).transpose(0, 3, 1, 2)
    g = jnp.cumsum(g.astype(jnp.float32).reshape(B, N, CHUNK, H).transpose(0, 3, 1, 2), axis=-1)

    causal = jnp.tril(jnp.ones((CHUNK, CHUNK), dtype=bool))
    strict = jnp.tril(jnp.ones((CHUNK, CHUNK), dtype=bool), k=-1)
    decay = jnp.exp(jnp.where(causal, g[..., :, None] - g[..., None, :], -jnp.inf))

    k_beta = k * beta[..., None]
    v_beta = v * beta[..., None]
    L = jnp.where(strict, jnp.einsum("bhncd,bhnmd->bhncm", k_beta, k) * decay, 0.0)
    eye = jnp.eye(CHUNK, dtype=jnp.float32)
    Tm = lax.linalg.triangular_solve(eye + L, jnp.broadcast_to(eye, L.shape),
                                     left_side=True, lower=True, unit_diagonal=True)
    u = jnp.einsum("bhncm,bhnmd->bhncd", Tm, v_beta)
    w = jnp.einsum("bhncm,bhnmd->bhncd", Tm, k_beta * jnp.exp(g)[..., None])
    intra = jnp.where(causal, jnp.einsum("bhncd,bhnmd->bhncm", q, k) * decay, 0.0)

    def to_scan(t):
        return jnp.moveaxis(t, 2, 0)

    def step(S, xs):
        q_c, k_c, u_c, w_c, A_c, g_c = xs
        v_new = u_c - jnp.einsum("bhcd,bhde->bhce", w_c, S)
        o = (jnp.einsum("bhcd,bhde->bhce", q_c * jnp.exp(g_c)[..., None], S)
             + jnp.einsum("bhcm,bhme->bhce", A_c, v_new))
        g_last = g_c[..., -1]
        S = (S * jnp.exp(g_last)[..., None, None]
             + jnp.einsum("bhcd,bhce->bhde", k_c * jnp.exp(g_last[..., None] - g_c)[..., None], v_new))
        return S, o

    S0 = jnp.zeros((B, H, Dk, Dv), jnp.float32)
    _, o = lax.scan(step, S0, (to_scan(q), to_scan(k), to_scan(u), to_scan(w),
                               to_scan(intra), to_scan(g)))
    return o.transpose(1, 0, 3, 2, 4).reshape(B, T, H, Dv)


def forgetting_attention(q, k, v, log_f):
    B, T, H, D = q.shape
    nb = T // Q_BLOCK
    F = jnp.cumsum(log_f.astype(jnp.float32), axis=1).transpose(0, 2, 1)
    qb = q.reshape(B, nb, Q_BLOCK, H, D).transpose(1, 0, 2, 3, 4)
    Fq = F.reshape(B, H, nb, Q_BLOCK).transpose(2, 0, 1, 3)
    pos_k = jnp.arange(T)
    scale = D ** -0.5

    def block(args):
        i, q_i, F_i = args
        s = jnp.einsum("bqhd,bkhd->bhqk", q_i, k, preferred_element_type=jnp.float32) * scale
        s = s + (F_i[..., :, None] - F[:, :, None, :])
        pos_q = i * Q_BLOCK + jnp.arange(Q_BLOCK)
        s = jnp.where(pos_q[:, None] >= pos_k[None, :], s, -jnp.inf)
        p = jax.nn.softmax(s, axis=-1)
        return jnp.einsum("bhqk,bkhd->bqhd", p.astype(v.dtype), v)

    o = lax.map(block, (jnp.arange(nb), qb, Fq))
    return o.transpose(1, 0, 2, 3, 4).reshape(B, T, H, D)


def setup_inputs(seed: int = 0) -> dict:
    key = jax.random.key(seed)
    ks = jax.random.split(key, 20)
    f32 = jnp.float32

    def normal(k, shape, fan_in):
        return jax.random.normal(k, shape, f32) * (fan_in ** -0.5)

    def gain(k, shape):
        return 1.0 + 0.02 * jax.random.normal(k, shape, f32)

    x = jax.random.normal(ks[0], (BATCH, SEQ, D_MODEL), f32)
    norm1_w = gain(ks[1], (DEPTH, D_MODEL))
    w_in = normal(ks[2], (DEPTH, D_MODEL, D_IN), D_MODEL)
    gdn_conv_w = normal(ks[3], (DEPTH, CONV_K, 3 * GDN_WIDTH), CONV_K)
    gdn_A_log = jnp.log(jax.random.uniform(ks[4], (DEPTH, GDN_HEADS), f32, 1.0, 16.0))
    dt = jnp.exp(jax.random.uniform(ks[5], (DEPTH, GDN_HEADS), f32, np.log(1e-3), np.log(1e-1)))
    gdn_dt_bias = dt + jnp.log(-jnp.expm1(-dt))
    gdn_out_norm_w = gain(ks[6], (DEPTH, GDN_HEAD_DIM))
    fox_f_bias = jax.random.uniform(ks[7], (DEPTH, FOX_HEADS), f32, 1.0, 5.0)
    fox_q_norm_w = gain(ks[8], (DEPTH, FOX_HEAD_DIM))
    fox_k_norm_w = gain(ks[9], (DEPTH, FOX_HEAD_DIM))
    w_out = normal(ks[10], (DEPTH, D_MIX, D_MODEL), D_MIX)
    norm2_w = gain(ks[11], (DEPTH, D_MODEL))
    w_ffn_gate = normal(ks[12], (DEPTH, D_MODEL, D_FF), D_MODEL)
    w_ffn_up = normal(ks[13], (DEPTH, D_MODEL, D_FF), D_MODEL)
    w_ffn_down = normal(ks[14], (DEPTH, D_FF, D_MODEL), D_FF)
    final_norm_w = gain(ks[15], (D_MODEL,))
    return {"x": x, "norm1_w": norm1_w, "w_in": w_in, "gdn_conv_w": gdn_conv_w,
            "gdn_A_log": gdn_A_log, "gdn_dt_bias": gdn_dt_bias,
            "gdn_out_norm_w": gdn_out_norm_w, "fox_f_bias": fox_f_bias,
            "fox_q_norm_w": fox_q_norm_w, "fox_k_norm_w": fox_k_norm_w,
            "w_out": w_out, "norm2_w": norm2_w, "w_ffn_gate": w_ffn_gate,
            "w_ffn_up": w_ffn_up, "w_ffn_down": w_ffn_down, "final_norm_w": final_norm_w}


def reference(x, norm1_w, w_in, gdn_conv_w, gdn_A_log, gdn_dt_bias, gdn_out_norm_w,
              fox_f_bias, fox_q_norm_w, fox_k_norm_w, w_out, norm2_w, w_ffn_gate,
              w_ffn_up, w_ffn_down, final_norm_w):
    B, T, _ = x.shape
    for l in range(DEPTH):
        h = rms_norm(x, norm1_w[l])
        proj = h @ w_in[l]
        (g_q, g_k, g_v, g_z, g_b, g_a,
         f_q, f_k, f_v, f_gate, f_f) = jnp.split(proj, SPLIT_POINTS, axis=-1)

        qkv = jax.nn.silu(causal_depthwise_conv(jnp.concatenate([g_q, g_k, g_v], -1), gdn_conv_w[l]))
        g_q, g_k, g_v = jnp.split(qkv, 3, axis=-1)
        gq = l2_norm(g_q.reshape(B, T, GDN_HEADS, GDN_HEAD_DIM))
        gk = l2_norm(g_k.reshape(B, T, GDN_HEADS, GDN_HEAD_DIM))
        gv = g_v.reshape(B, T, GDN_HEADS, GDN_HEAD_DIM)
        beta = jax.nn.sigmoid(g_b.astype(jnp.float32))
        g_log = -jnp.exp(gdn_A_log[l].astype(jnp.float32)) * jax.nn.softplus(
            g_a.astype(jnp.float32) + gdn_dt_bias[l].astype(jnp.float32))
        o_gdn = gated_delta_rule(gq, gk, gv, beta, g_log)
        z = g_z.reshape(B, T, GDN_HEADS, GDN_HEAD_DIM).astype(jnp.float32)
        o_gdn = rms_norm(o_gdn, gdn_out_norm_w[l]) * jax.nn.silu(z)
        o_gdn = o_gdn.astype(x.dtype).reshape(B, T, GDN_WIDTH)

        fq = rms_norm(f_q.reshape(B, T, FOX_HEADS, FOX_HEAD_DIM), fox_q_norm_w[l])
        fk = rms_norm(f_k.reshape(B, T, FOX_HEADS, FOX_HEAD_DIM), fox_k_norm_w[l])
        fv = f_v.reshape(B, T, FOX_HEADS, FOX_HEAD_DIM)
        log_f = jax.nn.log_sigmoid(f_f.astype(jnp.float32) + fox_f_bias[l].astype(jnp.float32))
        o_fox = forgetting_attention(fq, fk, fv, log_f).reshape(B, T, FOX_WIDTH)
        o_fox = o_fox * jax.nn.sigmoid(f_gate)

        mix = jnp.concatenate([o_gdn, o_fox.astype(x.dtype)], axis=-1)
        x = x + mix @ w_out[l]

        h = rms_norm(x, norm2_w[l])
        x = x + (jax.nn.silu(h @ w_ffn_gate[l]) * (h @ w_ffn_up[l])) @ w_ffn_down[l]
    return rms_norm(x, final_norm_w)
```

```python
import functools

import jax
import jax.numpy as jnp
from jax import lax
from jax.experimental import pallas as pl
from jax.experimental.pallas import tpu as pltpu

D_MODEL = 1024
HEADS = 8
HEAD_DIM = 64
WIDTH = HEADS * HEAD_DIM
CONV_K = 4
CHUNK = 64
D_FF = 2816
EPS = 1e-6

LANES = 128
MXU_DIM = 256
PACK = MXU_DIM // HEAD_DIM
N_SMALL = LANES
VMEM_LIMIT = 56 * 1024 * 1024

F32 = jnp.float32
BF16 = jnp.bfloat16
NEG_BIG = -1e30


def _const_spec(shape):
    nd = len(shape)
    return pl.BlockSpec(shape, lambda *_: (0,) * nd)


def _sigmoid(x):
    return 1.0 / (1.0 + jnp.exp(-x))


def _softplus(x):
    return jnp.maximum(x, 0.0) + jnp.log1p(jnp.exp(-jnp.abs(x)))


def _head_sums(y2, ones_bd):
    yb = y2.astype(BF16)
    parts = [jnp.dot(yb[:, c * MXU_DIM:(c + 1) * MXU_DIM], ones_bd, preferred_element_type=F32)
             for c in range(WIDTH // MXU_DIM)]
    return jnp.concatenate(parts, axis=1)


def _seg_cumsum(v, seg):
    pos = lax.broadcasted_iota(jnp.int32, v.shape, 0) & (seg - 1)
    d = 1
    while d < seg:
        v = v + jnp.where(pos >= d, pltpu.roll(v, d, 0), 0.0)
        d *= 2
    return v


def _inproj_kernel(tiles_per_seq, x_ref, n1_ref, w_ref, cw_ref, bias_ref, alog_ref, fqw_ref, fkw_ref,
                   ones_ref, gq_ref, gk_ref, gv_ref, gz_ref, fq_ref, fk_ref, fv_ref, fg_ref, sm_ref,
                   tail_ref, carry_ref):
    i = pl.program_id(0)
    tm = x_ref.shape[0]

    @pl.when(i % tiles_per_seq == 0)
    def _():
        tail_ref[...] = jnp.zeros_like(tail_ref)
        carry_ref[...] = jnp.zeros_like(carry_ref)

    x = x_ref[...]
    h = (x * lax.rsqrt(jnp.mean(x * x, axis=-1, keepdims=True) + EPS) * n1_ref[...]).astype(BF16)
    ones_bd = ones_ref[...]

    def proj(s, width=WIDTH):
        return jnp.dot(h, w_ref[:, s * WIDTH:s * WIDTH + width], preferred_element_type=F32)

    for s, out_ref in enumerate((gq_ref, gk_ref, gv_ref)):
        y = proj(s)
        cw = cw_ref[:, s * WIDTH:(s + 1) * WIDTH]
        head = jnp.concatenate([tail_ref[s], y[:8]], axis=0)
        acc = y * cw[CONV_K - 1:CONV_K]
        acc_head = y[:8] * cw[CONV_K - 1:CONV_K]
        for k in range(1, CONV_K):
            tap = cw[CONV_K - 1 - k:CONV_K - k]
            acc = acc + pltpu.roll(y, k, 0) * tap
            acc_head = acc_head + pltpu.roll(head, k, 0)[8:] * tap
        tail_ref[s] = y[tm - 8:]
        act_head = acc_head * _sigmoid(acc_head)
        act = acc * _sigmoid(acc)
        if s < 2:
            act = act * lax.rsqrt(_head_sums(act * act, ones_bd) + EPS)
            act_head = act_head * lax.rsqrt(_head_sums(act_head * act_head, ones_bd) + EPS)
        out_ref[...] = act.astype(BF16)
        out_ref[0:8, :] = act_head.astype(BF16)

    z = proj(3)
    gz_ref[...] = (z * _sigmoid(z)).astype(BF16)

    for s, out_ref, w_norm in ((4, fq_ref, fqw_ref), (5, fk_ref, fkw_ref)):
        y = proj(s)
        ms = _head_sums(y * y, ones_bd) * (1.0 / HEAD_DIM)
        out_ref[...] = (y * lax.rsqrt(ms + EPS) * w_norm[...]).astype(BF16)

    fv_ref[...] = proj(6).astype(BF16)
    fg_ref[...] = _sigmoid(proj(7)).astype(BF16)

    t = proj(8, N_SMALL) + bias_ref[...]
    lane = lax.broadcasted_iota(jnp.int32, t.shape, 1)
    beta = _sigmoid(t)
    g_log = -jnp.exp(alog_ref[...]) * _softplus(t)
    log_f = -_softplus(-t)
    g_cum = _seg_cumsum(g_log, CHUNK)
    f_cum = _seg_cumsum(log_f, tm) + carry_ref[0:1, :]
    carry_ref[0:1, :] = f_cum[tm - 1:tm, :]
    sm_ref[...] = jnp.where(lane < HEADS, beta, jnp.where(lane < 2 * HEADS, g_cum, f_cum))


def _inproj_call(x2, n1w, w_all, conv_w, bias_vec, alog_vec, fqw, fkw, ones_bd, *, seq, tm):
    m = x2.shape[0]
    assert seq % tm == 0 and tm % CHUNK == 0 and (tm & (tm - 1)) == 0
    wide = pl.BlockSpec((tm, WIDTH), lambda i: (i, 0))
    out_shape = [jax.ShapeDtypeStruct((m, WIDTH), BF16)] * 8 + [jax.ShapeDtypeStruct((m, N_SMALL), F32)]
    return pl.pallas_call(
        functools.partial(_inproj_kernel, seq // tm),
        grid=(m // tm,),
        in_specs=[pl.BlockSpec((tm, D_MODEL), lambda i: (i, 0)),
                  _const_spec(n1w.shape), _const_spec(w_all.shape), _const_spec(conv_w.shape),
                  _const_spec(bias_vec.shape), _const_spec(alog_vec.shape),
                  _const_spec(fqw.shape), _const_spec(fkw.shape), _const_spec(ones_bd.shape)],
        out_specs=[wide] * 8 + [pl.BlockSpec((tm, N_SMALL), lambda i: (i, 0))],
        out_shape=out_shape,
        scratch_shapes=[pltpu.VMEM((3, 8, WIDTH), F32), pltpu.VMEM((8, N_SMALL), F32)],
        compiler_params=pltpu.CompilerParams(dimension_semantics=("arbitrary",),
                                             vmem_limit_bytes=VMEM_LIMIT),
    )(x2, n1w, w_all, conv_w, bias_vec, alog_vec, fqw, fkw, ones_bd)


def _block_diag(x, bd_mask):
    t = jnp.concatenate([x.astype(BF16)] * PACK, axis=0)
    return jnp.where(bd_mask, t, jnp.zeros_like(t))


def _mm(lhs, rhs_bd):
    return jnp.dot(lhs.astype(BF16), rhs_bd, preferred_element_type=F32)


def _gdn_wy_kernel(q_ref, k_ref, v_ref, bx_ref, gx_ref, grow_ref,
                   u_ref, w_ref, a_ref, qg_ref, kd_ref, dl_ref):
    n_chunks = q_ref.shape[0] // CHUNK
    shape = (CHUNK, MXU_DIM)
    row = lax.broadcasted_iota(jnp.int32, shape, 0)
    col = lax.broadcasted_iota(jnp.int32, shape, 1) & (HEAD_DIM - 1)
    causal = row >= col
    strict = row > col
    eye = (row == col).astype(F32)
    r2 = lax.broadcasted_iota(jnp.int32, (MXU_DIM, MXU_DIM), 0)
    c2 = lax.broadcasted_iota(jnp.int32, (MXU_DIM, MXU_DIM), 1)
    bd_mask = (r2 // HEAD_DIM) == (c2 // HEAD_DIM)
    scale = HEAD_DIM ** -0.5

    for c in range(n_chunks):
        rs = slice(c * CHUNK, (c + 1) * CHUNK)
        q = q_ref[rs, :].astype(F32)
        k = k_ref[rs, :].astype(F32)
        v = v_ref[rs, :].astype(F32)
        bx = bx_ref[rs, :]
        gx = gx_ref[rs, :]
        grow = grow_ref[c]
        eg = jnp.exp(gx)
        kb = k * bx
        vb = v * bx
        kbg = kb * eg
        decay = jnp.exp(jnp.where(causal, gx - grow, NEG_BIG))

        bd_k = _block_diag(k, bd_mask)
        s1 = lax.dot_general(jnp.concatenate([kb, q], axis=0).astype(BF16), bd_k,
                             (((1,), (1,)), ((), ())), preferred_element_type=F32)
        lmat = jnp.where(strict, s1[:CHUNK] * decay, 0.0)
        a_ref[rs, :] = (s1[CHUNK:] * decay * scale).astype(BF16)

        blk8 = (row >> 3) == (col >> 3)
        n8 = jnp.where(blk8, -lmat, 0.0)
        t0 = eye + n8
        p1 = _mm(n8, _block_diag(n8, bd_mask))
        r = _mm(jnp.concatenate([p1, t0], axis=0), _block_diag(p1, bd_mask))
        p2 = r[:CHUNK]
        ta = t0 + r[CHUNK:]
        tinv = ta + _mm(ta, _block_diag(p2, bd_mask))
        for ls in (3, 4, 5):
            off = ((row >> (ls + 1)) == (col >> (ls + 1))) & ((row >> ls) == (col >> ls) + 1)
            b_off = jnp.where(off, lmat, 0.0)
            y = _mm(b_off, _block_diag(tinv, bd_mask))
            tinv = tinv - _mm(tinv, _block_diag(y, bd_mask))

        u_ref[rs, :] = _mm(tinv, _block_diag(vb, bd_mask)).astype(BF16)
        w_ref[rs, :] = _mm(tinv, _block_diag(kbg, bd_mask)).astype(BF16)
        qg_ref[rs, :] = (q * eg * scale).astype(BF16)
        g_last = gx[CHUNK - 1:CHUNK, :]
        kd_ref[rs, :] = (k * jnp.exp(g_last - gx)).astype(BF16)
        dl_ref[c] = jnp.exp(g_last)


def _gdn_wy_call(gq, gk, gv, betax, gx, grow, *, rows):
    m = gq.shape[0]
    n_groups = WIDTH // MXU_DIM
    cpb = rows // CHUNK
    blk = pl.BlockSpec((rows, MXU_DIM), lambda i, g: (i, g))
    rowblk = pl.BlockSpec((cpb, 1, MXU_DIM), lambda i, g: (i, 0, g))
    bf = jax.ShapeDtypeStruct((m, WIDTH), BF16)
    return pl.pallas_call(
        _gdn_wy_kernel,
        grid=(m // rows, n_groups),
        in_specs=[blk] * 5 + [rowblk],
        out_specs=[blk] * 5 + [rowblk],
        out_shape=[bf] * 5 + [jax.ShapeDtypeStruct((m // CHUNK, 1, WIDTH), F32)],
        compiler_params=pltpu.CompilerParams(dimension_semantics=("parallel", "parallel"),
                                             vmem_limit_bytes=VMEM_LIMIT),
    )(gq, gk, gv, betax, gx, grow)


def _gdn_scan_kernel(u_ref, w_ref, a_ref, qg_ref, kd_ref, dl_ref, gz_ref, nw_ref, ones_ref,
                     o_ref, s_ref, oacc_ref):
    seq = u_ref.shape[0]
    n_groups = WIDTH // MXU_DIM
    r2 = lax.broadcasted_iota(jnp.int32, (MXU_DIM, MXU_DIM), 0)
    c2 = lax.broadcasted_iota(jnp.int32, (MXU_DIM, MXU_DIM), 1)
    bd_mask = (r2 // HEAD_DIM) == (c2 // HEAD_DIM)
    s_ref[...] = jnp.zeros_like(s_ref)

    def body(c, carry):
        r0 = pl.multiple_of(c * CHUNK, CHUNK)
        rs = pl.ds(r0, CHUNK)
        dl = dl_ref[c]
        for g in range(n_groups):
            ls = slice(g * MXU_DIM, (g + 1) * MXU_DIM)
            state = s_ref[g]
            r = jnp.dot(jnp.concatenate([w_ref[rs, ls], qg_ref[rs, ls]], axis=0),
                        state.astype(BF16), preferred_element_type=F32)
            v_new = u_ref[rs, ls].astype(F32) - r[:CHUNK]
            o = r[CHUNK:] + jnp.dot(a_ref[rs, ls], _block_diag(v_new, bd_mask),
                                    preferred_element_type=F32)
            oacc_ref[rs, ls] = o
            upd = lax.dot_general(kd_ref[rs, ls], v_new.astype(BF16),
                                  (((0,), (0,)), ((), ())), preferred_element_type=F32)
            s_ref[g] = state * dl[:, ls] + jnp.where(bd_mask, upd, 0.0)
        return carry

    lax.fori_loop(0, seq // CHUNK, body, 0)

    o = oacc_ref[...]
    ms = _head_sums(o * o, ones_ref[...]) * (1.0 / HEAD_DIM)
    o_ref[...] = (o * lax.rsqrt(ms + EPS) * nw_ref[...] * gz_ref[...].astype(F32)).astype(BF16)


def _gdn_scan_call(u, w, a, qg, kd, dl, gz, nw, ones_bd, *, seq):
    m = u.shape[0]
    n_chunks = seq // CHUNK
    blk = pl.BlockSpec((seq, WIDTH), lambda b: (b, 0))
    return pl.pallas_call(
        _gdn_scan_kernel,
        grid=(m // seq,),
        in_specs=[blk] * 5 + [pl.BlockSpec((n_chunks, 1, WIDTH), lambda b: (b, 0, 0)), blk,
                              _const_spec(nw.shape), _const_spec(ones_bd.shape)],
        out_specs=blk,
        out_shape=jax.ShapeDtypeStruct((m, WIDTH), BF16),
        scratch_shapes=[pltpu.VMEM((WIDTH // MXU_DIM, MXU_DIM, MXU_DIM), F32),
                        pltpu.VMEM((seq, WIDTH), F32)],
        compiler_params=pltpu.CompilerParams(dimension_semantics=("parallel",),
                                             vmem_limit_bytes=VMEM_LIMIT),
    )(u, w, a, qg, kd, dl, gz, nw, ones_bd)


def _fox_kernel(q_ref, k_ref, v_ref, fx_ref, ft_ref, gate_ref, o_ref):
    qi = pl.program_id(2)
    tq = q_ref.shape[0]
    tk = tq
    lane = lax.broadcasted_iota(jnp.int32, (tq, LANES), 1)
    q = q_ref[...]
    zero = jnp.zeros_like(q)
    q_heads = (jnp.where(lane < HEAD_DIM, q, zero), jnp.where(lane >= HEAD_DIM, q, zero))
    f_q = (fx_ref[:, 0:1], fx_ref[:, HEAD_DIM:HEAD_DIM + 1])
    row = lax.broadcasted_iota(jnp.int32, (tq, tk), 0)
    col = lax.broadcasted_iota(jnp.int32, (tq, tk), 1)

    def body(j, carry):
        k0 = pl.multiple_of(j * tk, tk)
        k = k_ref[pl.ds(k0, tk), :]
        v = v_ref[pl.ds(k0, tk), :]
        f_k = ft_ref[0, 0, :, pl.ds(k0, tk)]
        visible = (row + (qi - j) * tq) >= col
        out = []
        for hh in range(2):
            m_prev, l_prev, acc = carry[hh]
            s = lax.dot_general(q_heads[hh], k, (((1,), (1,)), ((), ())), preferred_element_type=F32)
            s = jnp.where(visible, s + (f_q[hh] - f_k[hh:hh + 1, :]), NEG_BIG)
            m_new = jnp.maximum(m_prev, jnp.max(s, axis=1, keepdims=True))
            p = jnp.exp(s - m_new)
            alpha = jnp.exp(m_prev - m_new)
            l_new = alpha * l_prev + jnp.sum(p, axis=1, keepdims=True)
            acc = alpha * acc + jnp.dot(p.astype(BF16), v, preferred_element_type=F32)
            out.append((m_new, l_new, acc))
        return tuple(out)

    init = tuple((jnp.full((tq, 1), NEG_BIG, F32), jnp.zeros((tq, 1), F32), jnp.zeros((tq, LANES), F32))
                 for _ in range(2))
    (_, l0, a0), (_, l1, a1) = lax.fori_loop(0, qi + 1, body, init)
    o = jnp.where(lane < HEAD_DIM, a0 / l0, a1 / l1)
    o_ref[...] = (o * gate_ref[...].astype(F32)).astype(BF16)


def _fox_call(fq, fk, fv, fx, ftp, gate, *, seq, tq):
    m = fq.shape[0]
    batch = m // seq
    nq = seq // tq
    n_pairs = WIDTH // LANES
    qblk = pl.BlockSpec((tq, LANES), lambda b, p, i: (b * nq + i, p))
    kvblk = pl.BlockSpec((seq, LANES), lambda b, p, i: (b, p))
    return pl.pallas_call(
        _fox_kernel,
        grid=(batch, n_pairs, nq),
        in_specs=[qblk, kvblk, kvblk, qblk,
                  pl.BlockSpec((1, 1, 2, seq), lambda b, p, i: (b, p, 0, 0)), qblk],
        out_specs=qblk,
        out_shape=jax.ShapeDtypeStruct((m, WIDTH), BF16),
        compiler_params=pltpu.CompilerParams(dimension_semantics=("parallel", "parallel", "arbitrary"),
                                             vmem_limit_bytes=VMEM_LIMIT),
    )(fq, fk, fv, fx, ftp, gate)


FF_SPLITS = ((0, 1024), (1024, 1024), (2048, 768))


def _ffn_kernel(final, mg_ref, mf_ref, x_ref, wo_ref, n2_ref, wg_ref, wu_ref, wd_ref, fn_ref, o_ref):
    x1 = (x_ref[...]
          + jnp.dot(mg_ref[...], wo_ref[0:WIDTH, :], preferred_element_type=F32)
          + jnp.dot(mf_ref[...], wo_ref[WIDTH:2 * WIDTH, :], preferred_element_type=F32))
    h = (x1 * lax.rsqrt(jnp.mean(x1 * x1, axis=-1, keepdims=True) + EPS) * n2_ref[...]).astype(BF16)
    ffn = None
    for start, size in FF_SPLITS:
        gate = jnp.dot(h, wg_ref[:, start:start + size], preferred_element_type=F32)
        up = jnp.dot(h, wu_ref[:, start:start + size], preferred_element_type=F32)
        act = (gate * _sigmoid(gate) * up).astype(BF16)
        down = jnp.dot(act, wd_ref[start:start + size, :], preferred_element_type=F32)
        ffn = down if ffn is None else ffn + down
    y = x1 + ffn
    if final:
        y = y * lax.rsqrt(jnp.mean(y * y, axis=-1, keepdims=True) + EPS) * fn_ref[...]
    o_ref[...] = y


def _ffn_call(mix_g, mix_f, x2, wo, n2w, wg, wu, wd, fnw, *, tm, final):
    m = x2.shape[0]
    half = pl.BlockSpec((tm, WIDTH), lambda i: (i, 0))
    full = pl.BlockSpec((tm, D_MODEL), lambda i: (i, 0))

    def resident(shape):
        return pl.BlockSpec(shape, lambda i: (0, 0), pipeline_mode=pl.Buffered(1))

    return pl.pallas_call(
        functools.partial(_ffn_kernel, final),
        grid=(m // tm,),
        in_specs=[half, half, full, resident(wo.shape), resident(n2w.shape), resident(wg.shape),
                  resident(wu.shape), resident(wd.shape), resident(fnw.shape)],
        out_specs=full,
        out_shape=jax.ShapeDtypeStruct((m, D_MODEL), F32),
        compiler_params=pltpu.CompilerParams(dimension_semantics=("parallel",),
                                             vmem_limit_bytes=VMEM_LIMIT),
    )(mix_g, mix_f, x2, wo, n2w, wg, wu, wd, fnw)


def _lane_vec(parts):
    v = jnp.concatenate([p.astype(F32).reshape(-1) for p in parts])
    return jnp.pad(v, (0, N_SMALL - v.shape[0])).reshape(1, N_SMALL)


def kernel(x, norm1_w, w_in, gdn_conv_w, gdn_A_log, gdn_dt_bias, gdn_out_norm_w, fox_f_bias, fox_q_norm_w, fox_k_norm_w, w_out, norm2_w, w_ffn_gate, w_ffn_up, w_ffn_down, final_norm_w):
    batch, seq, _ = x.shape
    m = batch * seq
    depth = norm1_w.shape[0]
    zeros8 = jnp.zeros((HEADS,), F32)
    r2 = jnp.arange(MXU_DIM) // HEAD_DIM
    ones_bd = (r2[:, None] == r2[None, :]).astype(BF16)

    x2 = x.reshape(m, D_MODEL)
    for l in range(depth):
        w = w_in[l]
        big = 4 * WIDTH
        g_small = w[:, big:big + 2 * HEADS]
        f_start = big + 2 * HEADS
        f_small = w[:, f_start + big:f_start + big + HEADS]
        w_all = jnp.concatenate(
            [w[:, :big], w[:, f_start:f_start + big], g_small, f_small,
             jnp.zeros((D_MODEL, N_SMALL - 3 * HEADS), w.dtype)], axis=1).astype(BF16)
        bias_vec = _lane_vec([zeros8, gdn_dt_bias[l], fox_f_bias[l]])
        alog_vec = _lane_vec([zeros8, gdn_A_log[l], zeros8])
        fqw = (jnp.tile(fox_q_norm_w[l].astype(F32), HEADS) * (HEAD_DIM ** -0.5)).reshape(1, WIDTH)
        fkw = jnp.tile(fox_k_norm_w[l].astype(F32), HEADS).reshape(1, WIDTH)

        gq, gk, gv, gz, fq, fk, fv, fg, small = _inproj_call(
            x2, norm1_w[l].reshape(1, D_MODEL), w_all, gdn_conv_w[l].astype(F32), bias_vec, alog_vec,
            fqw, fkw, ones_bd, seq=seq, tm=512)

        beta = small[:, 0:HEADS]
        g_cum = small[:, HEADS:2 * HEADS]
        f_cum = small[:, 2 * HEADS:3 * HEADS]
        betax = jnp.repeat(beta, HEAD_DIM, axis=1)
        gx = jnp.repeat(g_cum, HEAD_DIM, axis=1)
        grow = g_cum.reshape(m // CHUNK, CHUNK, HEADS).transpose(0, 2, 1).reshape(m // CHUNK, 1, WIDTH)
        fx = jnp.repeat(f_cum, HEAD_DIM, axis=1)
        ftp = f_cum.reshape(batch, seq, HEADS // 2, 2).transpose(0, 2, 3, 1)

        u, wmat, amat, qg, kd, dl = _gdn_wy_call(gq, gk, gv, betax, gx, grow, rows=256)
        nw = jnp.tile(gdn_out_norm_w[l].astype(F32), HEADS).reshape(1, WIDTH)
        mix_g = _gdn_scan_call(u, wmat, amat, qg, kd, dl, gz, nw, ones_bd, seq=seq)
        mix_f = _fox_call(fq, fk, fv, fx, ftp, fg, seq=seq, tq=256)

        x2 = _ffn_call(mix_g, mix_f, x2, w_out[l].astype(BF16), norm2_w[l].reshape(1, D_MODEL),
                       w_ffn_gate[l].astype(BF16), w_ffn_up[l].astype(BF16), w_ffn_down[l].astype(BF16),
                       final_norm_w.reshape(1, D_MODEL), tm=512, final=(l == depth - 1))
    return x2.reshape(batch, seq, D_MODEL)
```

```python
import functools

import jax
import jax.numpy as jnp
from jax import lax
from jax.experimental import pallas as pl
from jax.experimental.pallas import tpu as pltpu

D_MODEL = 1024
HEADS = 8
HEAD_DIM = 64
WIDTH = HEADS * HEAD_DIM
CONV_K = 4
CHUNK = 64
D_FF = 2816
EPS = 1e-6

LANES = 128
MXU_DIM = 256
PACK = MXU_DIM // HEAD_DIM
N_SMALL = LANES
VMEM_LIMIT = 56 * 1024 * 1024

F32 = jnp.float32
BF16 = jnp.bfloat16
NEG_BIG = -1e30


def _const_spec(shape):
    nd = len(shape)
    return pl.BlockSpec(shape, lambda *_: (0,) * nd)


def _sigmoid(x):
    return 1.0 / (1.0 + jnp.exp(-x))


def _softplus(x):
    return jnp.maximum(x, 0.0) + jnp.log1p(jnp.exp(-jnp.abs(x)))


def _head_sums(y2, ones_bd):
    yb = y2.astype(BF16)
    parts = [jnp.dot(yb[:, c * MXU_DIM:(c + 1) * MXU_DIM], ones_bd, preferred_element_type=F32)
             for c in range(WIDTH // MXU_DIM)]
    return jnp.concatenate(parts, axis=1)


def _seg_cumsum(v, seg):
    pos = lax.broadcasted_iota(jnp.int32, v.shape, 0) & (seg - 1)
    d = 1
    while d < seg:
        v = v + jnp.where(pos >= d, pltpu.roll(v, d, 0), 0.0)
        d *= 2
    return v


def _inproj_kernel(tiles_per_seq, x_ref, n1_ref, w_ref, cw_ref, bias_ref, alog_ref, fqw_ref, fkw_ref,
                   ones_ref, gq_ref, gk_ref, gv_ref, gz_ref, fq_ref, fk_ref, fv_ref, fg_ref, sm_ref,
                   tail_ref, carry_ref):
    i = pl.program_id(0)
    tm = x_ref.shape[0]

    @pl.when(i % tiles_per_seq == 0)
    def _():
        tail_ref[...] = jnp.zeros_like(tail_ref)
        carry_ref[...] = jnp.zeros_like(carry_ref)

    x = x_ref[...]
    h = (x * lax.rsqrt(jnp.mean(x * x, axis=-1, keepdims=True) + EPS) * n1_ref[...]).astype(BF16)
    ones_bd = ones_ref[...]

    def proj(s, width=WIDTH):
        return jnp.dot(h, w_ref[:, s * WIDTH:s * WIDTH + width], preferred_element_type=F32)

    for s, out_ref in enumerate((gq_ref, gk_ref, gv_ref)):
        y = proj(s)
        cw = cw_ref[:, s * WIDTH:(s + 1) * WIDTH]
        head = jnp.concatenate([tail_ref[s], y[:8]], axis=0)
        acc = y * cw[CONV_K - 1:CONV_K]
        acc_head = y[:8] * cw[CONV_K - 1:CONV_K]
        for k in range(1, CONV_K):
            tap = cw[CONV_K - 1 - k:CONV_K - k]
            acc = acc + pltpu.roll(y, k, 0) * tap
            acc_head = acc_head + pltpu.roll(head, k, 0)[8:] * tap
        tail_ref[s] = y[tm - 8:]
        act_head = acc_head * _sigmoid(acc_head)
        act = acc * _sigmoid(acc)
        if s < 2:
            act = act * lax.rsqrt(_head_sums(act * act, ones_bd) + EPS)
            act_head = act_head * lax.rsqrt(_head_sums(act_head * act_head, ones_bd) + EPS)
        out_ref[...] = act.astype(BF16)
        out_ref[0:8, :] = act_head.astype(BF16)

    z = proj(3)
    gz_ref[...] = (z * _sigmoid(z)).astype(BF16)

    for s, out_ref, w_norm in ((4, fq_ref, fqw_ref), (5, fk_ref, fkw_ref)):
        y = proj(s)
        ms = _head_sums(y * y, ones_bd) * (1.0 / HEAD_DIM)
        out_ref[...] = (y * lax.rsqrt(ms + EPS) * w_norm[...]).astype(BF16)

    fv_ref[...] = proj(6).astype(BF16)
    fg_ref[...] = _sigmoid(proj(7)).astype(BF16)

    t = proj(8, N_SMALL) + bias_ref[...]
    lane = lax.broadcasted_iota(jnp.int32, t.shape, 1)
    beta = _sigmoid(t)
    g_log = -jnp.exp(alog_ref[...]) * _softplus(t)
    log_f = -_softplus(-t)
    g_cum = _seg_cumsum(g_log, CHUNK)
    f_cum = _seg_cumsum(log_f, tm) + carry_ref[0:1, :]
    carry_ref[0:1, :] = f_cum[tm - 1:tm, :]
    sm_ref[...] = jnp.where(lane < HEADS, beta, jnp.where(lane < 2 * HEADS, g_cum, f_cum))


def _inproj_call(x2, n1w, w_all, conv_w, bias_vec, alog_vec, fqw, fkw, ones_bd, *, seq, tm):
    m = x2.shape[0]
    assert seq % tm == 0 and tm % CHUNK == 0 and (tm & (tm - 1)) == 0
    wide = pl.BlockSpec((tm, WIDTH), lambda i: (i, 0))
    out_shape = [jax.ShapeDtypeStruct((m, WIDTH), BF16)] * 8 + [jax.ShapeDtypeStruct((m, N_SMALL), F32)]
    return pl.pallas_call(
        functools.partial(_inproj_kernel, seq // tm),
        grid=(m // tm,),
        in_specs=[pl.BlockSpec((tm, D_MODEL), lambda i: (i, 0)),
                  _const_spec(n1w.shape), _const_spec(w_all.shape), _const_spec(conv_w.shape),
                  _const_spec(bias_vec.shape), _const_spec(alog_vec.shape),
                  _const_spec(fqw.shape), _const_spec(fkw.shape), _const_spec(ones_bd.shape)],
        out_specs=[wide] * 8 + [pl.BlockSpec((tm, N_SMALL), lambda i: (i, 0))],
        out_shape=out_shape,
        scratch_shapes=[pltpu.VMEM((3, 8, WIDTH), F32), pltpu.VMEM((8, N_SMALL), F32)],
        compiler_params=pltpu.CompilerParams(dimension_semantics=("arbitrary",),
                                             vmem_limit_bytes=VMEM_LIMIT),
    )(x2, n1w, w_all, conv_w, bias_vec, alog_vec, fqw, fkw, ones_bd)


def _lane_lo_mask():
    return lax.broadcasted_iota(jnp.int32, (CHUNK, LANES), 1) < HEAD_DIM


def _block_diag(x, lane_lo):
    xb = x.astype(BF16)
    zero = jnp.zeros((CHUNK, LANES), BF16)
    blocks = []
    for h in range(PACK):
        half = xb[:, (h // 2) * LANES:(h // 2 + 1) * LANES]
        keep = jnp.where(lane_lo if h % 2 == 0 else jnp.logical_not(lane_lo), half, zero)
        blocks.append(jnp.concatenate([keep, zero] if h < 2 else [zero, keep], axis=1))
    return jnp.concatenate(blocks, axis=0)


def _mm(lhs, rhs_bd):
    return jnp.dot(lhs.astype(BF16), rhs_bd, preferred_element_type=F32)


def _gdn_wy_kernel(q_ref, k_ref, v_ref, bx_ref, gx_ref, grow_ref,
                   u_ref, w_ref, a_ref, qg_ref, kd_ref, dl_ref):
    n_chunks = q_ref.shape[0] // CHUNK
    shape = (CHUNK, MXU_DIM)
    row = lax.broadcasted_iota(jnp.int32, shape, 0)
    col = lax.broadcasted_iota(jnp.int32, shape, 1) & (HEAD_DIM - 1)
    causal = row >= col
    strict = row > col
    eye = (row == col).astype(F32)
    bd_mask = _lane_lo_mask()
    scale = HEAD_DIM ** -0.5

    cs = range(n_chunks)
    rows = [slice(c * CHUNK, (c + 1) * CHUNK) for c in cs]
    bd = lambda xs: [_block_diag(x, bd_mask) for x in xs]
    mm = lambda ls, rs_: [_mm(a, b) for a, b in zip(ls, rs_)]
    stack = lambda xs, ys: [jnp.concatenate([x, y], axis=0) for x, y in zip(xs, ys)]

    q = [q_ref[r, :].astype(F32) for r in rows]
    k = [k_ref[r, :].astype(F32) for r in rows]
    gx = [gx_ref[r, :] for r in rows]
    kb = [k[c] * bx_ref[rows[c], :] for c in cs]
    decay = [jnp.exp(jnp.where(causal, gx[c] - grow_ref[c], NEG_BIG)) for c in cs]

    bd_k = bd(k)
    s1 = [lax.dot_general(jnp.concatenate([kb[c], q[c]], axis=0).astype(BF16), bd_k[c],
                          (((1,), (1,)), ((), ())), preferred_element_type=F32) for c in cs]
    lmat = [jnp.where(strict, s1[c][:CHUNK] * decay[c], 0.0) for c in cs]
    for c in cs:
        a_ref[rows[c], :] = (s1[c][CHUNK:] * decay[c] * scale).astype(BF16)

    blk8 = (row >> 3) == (col >> 3)
    n8 = [jnp.where(blk8, -l, 0.0) for l in lmat]
    t0 = [eye + n for n in n8]
    p1 = mm(n8, bd(n8))
    r = mm(stack(p1, t0), bd(p1))
    ta = [t0[c] + r[c][CHUNK:] for c in cs]
    tinv = [ta[c] + z for c, z in enumerate(mm(ta, bd([x[:CHUNK] for x in r])))]
    for ls in (3, 4, 5):
        off = ((row >> (ls + 1)) == (col >> (ls + 1))) & ((row >> ls) == (col >> ls) + 1)
        y = mm([jnp.where(off, l, 0.0) for l in lmat], bd(tinv))
        tinv = [tinv[c] - z for c, z in enumerate(mm(tinv, bd(y)))]

    eg = [jnp.exp(g) for g in gx]
    u = mm(tinv, bd([v_ref[rows[c], :].astype(F32) * bx_ref[rows[c], :] for c in cs]))
    w = mm(tinv, bd([kb[c] * eg[c] for c in cs]))
    for c in cs:
        u_ref[rows[c], :] = u[c].astype(BF16)
        w_ref[rows[c], :] = w[c].astype(BF16)
        qg_ref[rows[c], :] = (q[c] * eg[c] * scale).astype(BF16)
        g_last = gx[c][CHUNK - 1:CHUNK, :]
        kd_ref[rows[c], :] = (k[c] * jnp.exp(g_last - gx[c])).astype(BF16)
        dl_ref[c] = jnp.exp(g_last)


def _gdn_wy_call(gq, gk, gv, betax, gx, grow, *, rows):
    m = gq.shape[0]
    n_groups = WIDTH // MXU_DIM
    cpb = rows // CHUNK
    blk = pl.BlockSpec((rows, MXU_DIM), lambda i, g: (i, g))
    rowblk = pl.BlockSpec((cpb, 1, MXU_DIM), lambda i, g: (i, 0, g))
    bf = jax.ShapeDtypeStruct((m, WIDTH), BF16)
    return pl.pallas_call(
        _gdn_wy_kernel,
        grid=(m // rows, n_groups),
        in_specs=[blk] * 5 + [rowblk],
        out_specs=[blk] * 5 + [rowblk],
        out_shape=[bf] * 5 + [jax.ShapeDtypeStruct((m // CHUNK, 1, WIDTH), F32)],
        compiler_params=pltpu.CompilerParams(dimension_semantics=("parallel", "parallel"),
                                             vmem_limit_bytes=VMEM_LIMIT),
    )(gq, gk, gv, betax, gx, grow)


def _gdn_scan_kernel(u_ref, w_ref, a_ref, qg_ref, kd_ref, dl_ref, gz_ref, nw_ref, ones_ref,
                     o_ref, s_ref, oacc_ref):
    seq = u_ref.shape[0]
    n_groups = WIDTH // MXU_DIM
    r2 = lax.broadcasted_iota(jnp.int32, (MXU_DIM, MXU_DIM), 0)
    c2 = lax.broadcasted_iota(jnp.int32, (MXU_DIM, MXU_DIM), 1)
    bd_mask = (r2 // HEAD_DIM) == (c2 // HEAD_DIM)
    lane_lo = _lane_lo_mask()
    s_ref[...] = jnp.zeros_like(s_ref)

    def body(c, carry):
        r0 = pl.multiple_of(c * CHUNK, CHUNK)
        rs = pl.ds(r0, CHUNK)
        dl = dl_ref[c]
        for g in range(n_groups):
            ls = slice(g * MXU_DIM, (g + 1) * MXU_DIM)
            state = s_ref[g]
            r = jnp.dot(jnp.concatenate([w_ref[rs, ls], qg_ref[rs, ls]], axis=0),
                        state.astype(BF16), preferred_element_type=F32)
            v_new = u_ref[rs, ls].astype(F32) - r[:CHUNK]
            o = r[CHUNK:] + jnp.dot(a_ref[rs, ls], _block_diag(v_new, lane_lo),
                                    preferred_element_type=F32)
            oacc_ref[rs, ls] = o
            upd = lax.dot_general(kd_ref[rs, ls], v_new.astype(BF16),
                                  (((0,), (0,)), ((), ())), preferred_element_type=F32)
            s_ref[g] = state * dl[:, ls] + jnp.where(bd_mask, upd, 0.0)
        return carry

    lax.fori_loop(0, seq // CHUNK, body, 0)

    o = oacc_ref[...]
    ms = _head_sums(o * o, ones_ref[...]) * (1.0 / HEAD_DIM)
    o_ref[...] = (o * lax.rsqrt(ms + EPS) * nw_ref[...] * gz_ref[...].astype(F32)).astype(BF16)


def _gdn_scan_call(u, w, a, qg, kd, dl, gz, nw, ones_bd, *, seq):
    m = u.shape[0]
    n_chunks = seq // CHUNK
    blk = pl.BlockSpec((seq, WIDTH), lambda b: (b, 0))
    return pl.pallas_call(
        _gdn_scan_kernel,
        grid=(m // seq,),
        in_specs=[blk] * 5 + [pl.BlockSpec((n_chunks, 1, WIDTH), lambda b: (b, 0, 0)), blk,
                              _const_spec(nw.shape), _const_spec(ones_bd.shape)],
        out_specs=blk,
        out_shape=jax.ShapeDtypeStruct((m, WIDTH), BF16),
        scratch_shapes=[pltpu.VMEM((WIDTH // MXU_DIM, MXU_DIM, MXU_DIM), F32),
                        pltpu.VMEM((seq, WIDTH), F32)],
        compiler_params=pltpu.CompilerParams(dimension_semantics=("parallel",),
                                             vmem_limit_bytes=VMEM_LIMIT),
    )(u, w, a, qg, kd, dl, gz, nw, ones_bd)


FOX_TQ = 256
FOX_TK = 256
FOX_WIDTH = 512


def _fox_kernel(tk, q_ref, k_ref, v_ref, fx_ref, ft_ref, gate_ref, o_ref, vt_ref):
    qi = pl.program_id(2)
    tq, width = q_ref.shape
    n_heads = width // HEAD_DIM
    sub = tq // tk

    @pl.when(qi == 0)
    def _():
        vt_ref[...] = v_ref[...].T

    lane = lax.broadcasted_iota(jnp.int32, (tq, LANES), 1)
    q0 = pl.multiple_of(qi * tq, tq)
    q_t = []
    for h in range(n_heads):
        pair = q_ref[:, (h // 2) * LANES:(h // 2 + 1) * LANES]
        keep = (lane < HEAD_DIM) if h % 2 == 0 else (lane >= HEAD_DIM)
        q_t.append(jnp.where(keep, pair, jnp.zeros_like(pair)).T)
    f_q = ft_ref[0, 0, :, pl.ds(q0, tq)]
    kv_pos = lax.broadcasted_iota(jnp.int32, (tk, tq), 0)
    q_pos = lax.broadcasted_iota(jnp.int32, (tk, tq), 1)

    def tile(j, carry, diag_offset):
        k0 = pl.multiple_of(j * tk, tk)
        heads = range(n_heads)
        s = []
        for h in heads:
            k = k_ref[pl.ds(k0, tk), (h // 2) * LANES:(h // 2 + 1) * LANES]
            f_k = fx_ref[pl.ds(k0, tk), h * HEAD_DIM:h * HEAD_DIM + 1]
            s_h = jnp.dot(k, q_t[h], preferred_element_type=F32) + (f_q[h:h + 1, :] - f_k)
            if diag_offset is not None:
                s_h = jnp.where(kv_pos + diag_offset <= q_pos, s_h, NEG_BIG)
            s.append(s_h)
        m_new = [jnp.maximum(carry[h][0], jnp.max(s[h], axis=0, keepdims=True)) for h in heads]
        p = [jnp.exp(s[h] - m_new[h]) for h in heads]
        alpha = [jnp.exp(carry[h][0] - m_new[h]) for h in heads]
        l_new = [alpha[h] * carry[h][1] + jnp.sum(p[h], axis=0, keepdims=True) for h in heads]
        pv = [jnp.dot(vt_ref[h * HEAD_DIM:(h + 1) * HEAD_DIM, pl.ds(k0, tk)], p[h].astype(BF16),
                      preferred_element_type=F32) for h in heads]
        return tuple((m_new[h], l_new[h], alpha[h] * carry[h][2] + pv[h]) for h in heads)

    init = tuple((jnp.full((1, tq), NEG_BIG, F32), jnp.zeros((1, tq), F32), jnp.zeros((HEAD_DIM, tq), F32))
                 for _ in range(n_heads))
    carry = lax.fori_loop(0, qi * sub, lambda j, c: tile(j, c, None), init)
    for d in range(sub):
        carry = tile(qi * sub + d, carry, d * tk)
    o_t = jnp.concatenate([acc / l_sum for _, l_sum, acc in carry], axis=0)
    o_ref[...] = (o_t.T * gate_ref[...].astype(F32)).astype(BF16)


def _fox_call(fq, fk, fv, fx, ftp, gate, *, seq, tq, tk, width):
    m = fq.shape[0]
    batch = m // seq
    nq = seq // tq
    qblk = pl.BlockSpec((tq, width), lambda b, g, i: (b * nq + i, g))
    kvblk = pl.BlockSpec((seq, width), lambda b, g, i: (b, g))
    return pl.pallas_call(
        functools.partial(_fox_kernel, tk),
        grid=(batch, WIDTH // width, nq),
        in_specs=[qblk, kvblk, kvblk, kvblk,
                  pl.BlockSpec((1, 1, width // HEAD_DIM, seq), lambda b, g, i: (b, g, 0, 0)), qblk],
        out_specs=qblk,
        out_shape=jax.ShapeDtypeStruct((m, WIDTH), BF16),
        scratch_shapes=[pltpu.VMEM((width, seq), BF16)],
        compiler_params=pltpu.CompilerParams(dimension_semantics=("parallel", "parallel", "arbitrary"),
                                             vmem_limit_bytes=VMEM_LIMIT),
    )(fq, fk, fv, fx, ftp, gate)


FF_SPLITS = ((0, 1024), (1024, 1024), (2048, 768))


def _ffn_kernel(final, mg_ref, mf_ref, x_ref, wo_ref, n2_ref, wg_ref, wu_ref, wd_ref, fn_ref, o_ref):
    x1 = (x_ref[...]
          + jnp.dot(mg_ref[...], wo_ref[0:WIDTH, :], preferred_element_type=F32)
          + jnp.dot(mf_ref[...], wo_ref[WIDTH:2 * WIDTH, :], preferred_element_type=F32))
    h = (x1 * lax.rsqrt(jnp.mean(x1 * x1, axis=-1, keepdims=True) + EPS) * n2_ref[...]).astype(BF16)
    ffn = None
    for start, size in FF_SPLITS:
        gate = jnp.dot(h, wg_ref[:, start:start + size], preferred_element_type=F32)
        up = jnp.dot(h, wu_ref[:, start:start + size], preferred_element_type=F32)
        act = (gate * _sigmoid(gate) * up).astype(BF16)
        down = jnp.dot(act, wd_ref[start:start + size, :], preferred_element_type=F32)
        ffn = down if ffn is None else ffn + down
    y = x1 + ffn
    if final:
        y = y * lax.rsqrt(jnp.mean(y * y, axis=-1, keepdims=True) + EPS) * fn_ref[...]
    o_ref[...] = y


def _ffn_call(mix_g, mix_f, x2, wo, n2w, wg, wu, wd, fnw, *, tm, final):
    m = x2.shape[0]
    half = pl.BlockSpec((tm, WIDTH), lambda i: (i, 0))
    full = pl.BlockSpec((tm, D_MODEL), lambda i: (i, 0))

    def resident(shape):
        return pl.BlockSpec(shape, lambda i: (0, 0), pipeline_mode=pl.Buffered(1))

    return pl.pallas_call(
        functools.partial(_ffn_kernel, final),
        grid=(m // tm,),
        in_specs=[half, half, full, resident(wo.shape), resident(n2w.shape), resident(wg.shape),
                  resident(wu.shape), resident(wd.shape), resident(fnw.shape)],
        out_specs=full,
        out_shape=jax.ShapeDtypeStruct((m, D_MODEL), F32),
        compiler_params=pltpu.CompilerParams(dimension_semantics=("parallel",),
                                             vmem_limit_bytes=VMEM_LIMIT),
    )(mix_g, mix_f, x2, wo, n2w, wg, wu, wd, fnw)


def _lane_vec(parts):
    v = jnp.concatenate([p.astype(F32).reshape(-1) for p in parts])
    return jnp.pad(v, (0, N_SMALL - v.shape[0])).reshape(1, N_SMALL)


def kernel(x, norm1_w, w_in, gdn_conv_w, gdn_A_log, gdn_dt_bias, gdn_out_norm_w, fox_f_bias, fox_q_norm_w, fox_k_norm_w, w_out, norm2_w, w_ffn_gate, w_ffn_up, w_ffn_down, final_norm_w):
    batch, seq, _ = x.shape
    m = batch * seq
    depth = norm1_w.shape[0]
    zeros8 = jnp.zeros((HEADS,), F32)
    r2 = jnp.arange(MXU_DIM) // HEAD_DIM
    ones_bd = (r2[:, None] == r2[None, :]).astype(BF16)

    x2 = x.reshape(m, D_MODEL)
    for l in range(depth):
        w = w_in[l]
        big = 4 * WIDTH
        g_small = w[:, big:big + 2 * HEADS]
        f_start = big + 2 * HEADS
        f_small = w[:, f_start + big:f_start + big + HEADS]
        w_all = jnp.concatenate(
            [w[:, :big], w[:, f_start:f_start + big], g_small, f_small,
             jnp.zeros((D_MODEL, N_SMALL - 3 * HEADS), w.dtype)], axis=1).astype(BF16)
        bias_vec = _lane_vec([zeros8, gdn_dt_bias[l], fox_f_bias[l]])
        alog_vec = _lane_vec([zeros8, gdn_A_log[l], zeros8])
        fqw = (jnp.tile(fox_q_norm_w[l].astype(F32), HEADS) * (HEAD_DIM ** -0.5)).reshape(1, WIDTH)
        fkw = jnp.tile(fox_k_norm_w[l].astype(F32), HEADS).reshape(1, WIDTH)

        gq, gk, gv, gz, fq, fk, fv, fg, small = _inproj_call(
            x2, norm1_w[l].reshape(1, D_MODEL), w_all, gdn_conv_w[l].astype(F32), bias_vec, alog_vec,
            fqw, fkw, ones_bd, seq=seq, tm=512)

        beta = small[:, 0:HEADS]
        g_cum = small[:, HEADS:2 * HEADS]
        f_cum = small[:, 2 * HEADS:3 * HEADS]
        betax = jnp.repeat(beta, HEAD_DIM, axis=1)
        gx = jnp.repeat(g_cum, HEAD_DIM, axis=1)
        grow = g_cum.reshape(m // CHUNK, CHUNK, HEADS).transpose(0, 2, 1).reshape(m // CHUNK, 1, WIDTH)
        fx = jnp.repeat(f_cum, HEAD_DIM, axis=1)
        fox_heads = FOX_WIDTH // HEAD_DIM
        ftp = f_cum.reshape(batch, seq, HEADS // fox_heads, fox_heads).transpose(0, 2, 3, 1)

        u, wmat, amat, qg, kd, dl = _gdn_wy_call(gq, gk, gv, betax, gx, grow, rows=512)
        nw = jnp.tile(gdn_out_norm_w[l].astype(F32), HEADS).reshape(1, WIDTH)
        mix_g = _gdn_scan_call(u, wmat, amat, qg, kd, dl, gz, nw, ones_bd, seq=seq)
        mix_f = _fox_call(fq, fk, fv, fx, ftp, fg, seq=seq, tq=FOX_TQ, tk=FOX_TK, width=FOX_WIDTH)

        x2 = _ffn_call(mix_g, mix_f, x2, w_out[l].astype(BF16), norm2_w[l].reshape(1, D_MODEL),
                       w_ffn_gate[l].astype(BF16), w_ffn_up[l].astype(BF16), w_ffn_down[l].astype(BF16),
                       final_norm_w.reshape(1, D_MODEL), tm=512, final=(l == depth - 1))
    return x2.reshape(batch, seq, D_MODEL)
```

```python
import functools

import numpy as np
import jax
import jax.numpy as jnp
from jax import lax
from jax.experimental import pallas as pl
from jax.experimental.pallas import tpu as pltpu

D_MODEL = 1024
HEADS = 8
HEAD_DIM = 64
WIDTH = HEADS * HEAD_DIM
CONV_K = 4
CHUNK = 64
D_FF = 2816
EPS = 1e-6

LANES = 128
MXU_DIM = 256
PACK = MXU_DIM // HEAD_DIM
N_SMALL = LANES
VMEM_LIMIT = 56 * 1024 * 1024

F32 = jnp.float32
BF16 = jnp.bfloat16
NEG_BIG = -1e30
LOG2E = 1.4426950408889634
EXT_STRIDE = 6


def _const_spec(shape):
    nd = len(shape)
    return pl.BlockSpec(shape, lambda *_: (0,) * nd)


def _sigmoid(x):
    return 1.0 / (1.0 + jnp.exp(-x))


def _softplus(x):
    return jnp.maximum(x, 0.0) + jnp.log1p(jnp.exp(-jnp.abs(x)))


def _head_sums(y2, ones_bd):
    yb = y2.astype(BF16)
    parts = [jnp.dot(yb[:, c * MXU_DIM:(c + 1) * MXU_DIM], ones_bd, preferred_element_type=F32)
             for c in range(WIDTH // MXU_DIM)]
    return jnp.concatenate(parts, axis=1)


def _seg_cumsum(v, seg):
    pos = lax.broadcasted_iota(jnp.int32, v.shape, 0) & (seg - 1)
    d = 1
    while d < seg:
        v = v + jnp.where(pos >= d, pltpu.roll(v, d, 0), 0.0)
        d *= 2
    return v


def _inproj_kernel(tiles_per_seq, x_ref, n1_ref, w_ref, cw_ref, bias_ref, alog_ref, fqw_ref, fkw_ref,
                   ones_ref, sel_ref, one_ref,
                   gq_ref, gk_ref, gv_ref, gz_ref, fq_ref, fk_ref, fv_ref, fg_ref, sm_ref, kx_ref, qx_ref,
                   tail_ref, carry_ref):
    i = pl.program_id(0)
    tm = x_ref.shape[0]

    @pl.when(i % tiles_per_seq == 0)
    def _():
        tail_ref[...] = jnp.zeros_like(tail_ref)
        carry_ref[...] = jnp.zeros_like(carry_ref)

    x = x_ref[...]
    h = (x * lax.rsqrt(jnp.mean(x * x, axis=-1, keepdims=True) + EPS) * n1_ref[...]).astype(BF16)
    ones_bd = ones_ref[...]

    def proj(s, width=WIDTH):
        return jnp.dot(h, w_ref[:, s * WIDTH:s * WIDTH + width], preferred_element_type=F32)

    for s, out_ref in enumerate((gq_ref, gk_ref, gv_ref)):
        y = proj(s)
        cw = cw_ref[:, s * WIDTH:(s + 1) * WIDTH]
        head = jnp.concatenate([tail_ref[s], y[:8]], axis=0)
        acc = y * cw[CONV_K - 1:CONV_K]
        acc_head = y[:8] * cw[CONV_K - 1:CONV_K]
        for k in range(1, CONV_K):
            tap = cw[CONV_K - 1 - k:CONV_K - k]
            acc = acc + pltpu.roll(y, k, 0) * tap
            acc_head = acc_head + pltpu.roll(head, k, 0)[8:] * tap
        tail_ref[s] = y[tm - 8:]
        act_head = acc_head * _sigmoid(acc_head)
        act = acc * _sigmoid(acc)
        if s < 2:
            act = act * lax.rsqrt(_head_sums(act * act, ones_bd) + EPS)
            act_head = act_head * lax.rsqrt(_head_sums(act_head * act_head, ones_bd) + EPS)
        out_ref[...] = act.astype(BF16)
        out_ref[0:8, :] = act_head.astype(BF16)

    z = proj(3)
    gz_ref[...] = (z * _sigmoid(z)).astype(BF16)

    for s, out_ref, w_norm in ((4, fq_ref, fqw_ref), (5, fk_ref, fkw_ref)):
        y = proj(s)
        ms = _head_sums(y * y, ones_bd) * (1.0 / HEAD_DIM)
        out_ref[...] = (y * lax.rsqrt(ms + EPS) * w_norm[...]).astype(BF16)

    fv_ref[...] = proj(6).astype(BF16)
    fg_ref[...] = _sigmoid(proj(7)).astype(BF16)

    t = proj(8, N_SMALL) + bias_ref[...]
    lane = lax.broadcasted_iota(jnp.int32, t.shape, 1)
    beta = _sigmoid(t)
    g_log = -jnp.exp(alog_ref[...]) * _softplus(t)
    log_f = -_softplus(-t)
    g_cum = _seg_cumsum(g_log, CHUNK)
    f_cum = _seg_cumsum(log_f, tm) + carry_ref[0:1, :]
    carry_ref[0:1, :] = f_cum[tm - 1:tm, :]
    sm_ref[...] = jnp.where(lane < HEADS, beta, jnp.where(lane < 2 * HEADS, g_cum, f_cum))

    f2 = f_cum * LOG2E
    hi = f2.astype(BF16)
    r1 = f2 - hi.astype(F32)
    mid = r1.astype(BF16)
    lo = (r1 - mid.astype(F32)).astype(BF16)
    ext = jnp.dot(jnp.concatenate([hi, mid, lo], axis=1), sel_ref[...], preferred_element_type=F32) + one_ref[...]
    kx_ref[...] = ext[:, :LANES].astype(BF16)
    qx_ref[...] = ext[:, LANES:].astype(BF16)


def _bias_selectors():
    sel = np.zeros((3 * LANES, 2 * LANES), np.float32)
    one = np.zeros((1, 2 * LANES), np.float32)
    for h in range(HEADS):
        src = 2 * HEADS + h
        for part in range(3):
            sel[part * LANES + src, EXT_STRIDE * h + part] = -1.0
            sel[part * LANES + src, LANES + EXT_STRIDE * h + 3 + part] = 1.0
            one[0, EXT_STRIDE * h + 3 + part] = 1.0
            one[0, LANES + EXT_STRIDE * h + part] = 1.0
    return jnp.asarray(sel, BF16), jnp.asarray(one, F32)


def _inproj_call(x2, n1w, w_all, conv_w, bias_vec, alog_vec, fqw, fkw, ones_bd, *, seq, tm):
    m = x2.shape[0]
    assert seq % tm == 0 and tm % CHUNK == 0 and (tm & (tm - 1)) == 0
    sel, one = _bias_selectors()
    wide = pl.BlockSpec((tm, WIDTH), lambda i: (i, 0))
    narrow = pl.BlockSpec((tm, LANES), lambda i: (i, 0))
    out_shape = ([jax.ShapeDtypeStruct((m, WIDTH), BF16)] * 8 + [jax.ShapeDtypeStruct((m, N_SMALL), F32)]
                 + [jax.ShapeDtypeStruct((m, LANES), BF16)] * 2)
    return pl.pallas_call(
        functools.partial(_inproj_kernel, seq // tm),
        grid=(m // tm,),
        in_specs=[pl.BlockSpec((tm, D_MODEL), lambda i: (i, 0)),
                  _const_spec(n1w.shape), _const_spec(w_all.shape), _const_spec(conv_w.shape),
                  _const_spec(bias_vec.shape), _const_spec(alog_vec.shape),
                  _const_spec(fqw.shape), _const_spec(fkw.shape), _const_spec(ones_bd.shape),
                  _const_spec(sel.shape), _const_spec(one.shape)],
        out_specs=[wide] * 8 + [narrow] * 3,
        out_shape=out_shape,
        scratch_shapes=[pltpu.VMEM((3, 8, WIDTH), F32), pltpu.VMEM((8, N_SMALL), F32)],
        compiler_params=pltpu.CompilerParams(dimension_semantics=("arbitrary",),
                                             vmem_limit_bytes=VMEM_LIMIT),
    )(x2, n1w, w_all, conv_w, bias_vec, alog_vec, fqw, fkw, ones_bd, sel, one)


def _lane_lo_mask():
    return lax.broadcasted_iota(jnp.int32, (CHUNK, LANES), 1) < HEAD_DIM


def _block_diag(x, lane_lo):
    xb = x.astype(BF16)
    zero = jnp.zeros((CHUNK, LANES), BF16)
    blocks = []
    for h in range(PACK):
        half = xb[:, (h // 2) * LANES:(h // 2 + 1) * LANES]
        keep = jnp.where(lane_lo if h % 2 == 0 else jnp.logical_not(lane_lo), half, zero)
        blocks.append(jnp.concatenate([keep, zero] if h < 2 else [zero, keep], axis=1))
    return jnp.concatenate(blocks, axis=0)


def _mm(lhs, rhs_bd):
    return jnp.dot(lhs.astype(BF16), rhs_bd, preferred_element_type=F32)


def _expand_heads(sm, first_lane):
    lane_lo = lax.broadcasted_iota(jnp.int32, (sm.shape[0], LANES), 1) < HEAD_DIM
    parts = []
    for p in range(HEADS // 2):
        c = first_lane + 2 * p
        even = jnp.broadcast_to(sm[:, c:c + 1], (sm.shape[0], LANES))
        odd = jnp.broadcast_to(sm[:, c + 1:c + 2], (sm.shape[0], LANES))
        parts.append(jnp.where(lane_lo, even, odd))
    return jnp.concatenate(parts, axis=1)


def _gdn_wy_kernel(q_ref, k_ref, v_ref, sm_ref, grow_ref,
                   u_ref, w_ref, a_ref, qg_ref, kd_ref, dl_ref):
    n_chunks = q_ref.shape[0] // CHUNK
    n_groups = WIDTH // MXU_DIM
    shape = (CHUNK, MXU_DIM)
    row = lax.broadcasted_iota(jnp.int32, shape, 0)
    col = lax.broadcasted_iota(jnp.int32, shape, 1) & (HEAD_DIM - 1)
    causal = row >= col
    strict = row > col
    eye = (row == col).astype(F32)
    bd_mask = _lane_lo_mask()
    scale = HEAD_DIM ** -0.5

    units = [(slice(c * CHUNK, (c + 1) * CHUNK), slice(g * MXU_DIM, (g + 1) * MXU_DIM), c)
             for c in range(n_chunks) for g in range(n_groups)]
    cs = range(len(units))
    bd = lambda xs: [_block_diag(x, bd_mask) for x in xs]
    mm = lambda ls, rs_: [_mm(a, b) for a, b in zip(ls, rs_)]
    stack = lambda xs, ys: [jnp.concatenate([x, y], axis=0) for x, y in zip(xs, ys)]

    sm = sm_ref[...]
    bx_all = _expand_heads(sm, 0)
    gx_all = _expand_heads(sm, HEADS)
    q = [q_ref[r, l].astype(F32) for r, l, _ in units]
    k = [k_ref[r, l].astype(F32) for r, l, _ in units]
    bx = [bx_all[r, l] for r, l, _ in units]
    gx = [gx_all[r, l] for r, l, _ in units]
    kb = [k[c] * bx[c] for c in cs]
    decay = [jnp.exp(jnp.where(causal, gx[c] - grow_ref[ch, :, l], NEG_BIG))
             for c, (_, l, ch) in enumerate(units)]

    bd_k = bd(k)
    s1 = [lax.dot_general(jnp.concatenate([kb[c], q[c]], axis=0).astype(BF16), bd_k[c],
                          (((1,), (1,)), ((), ())), preferred_element_type=F32) for c in cs]
    lmat = [jnp.where(strict, s1[c][:CHUNK] * decay[c], 0.0) for c in cs]
    for c, (r, l, _) in enumerate(units):
        a_ref[r, l] = (s1[c][CHUNK:] * decay[c] * scale).astype(BF16)

    blk8 = (row >> 3) == (col >> 3)
    n8 = [jnp.where(blk8, -l, 0.0) for l in lmat]
    t0 = [eye + n for n in n8]
    p1 = mm(n8, bd(n8))
    r = mm(stack(p1, t0), bd(p1))
    ta = [t0[c] + r[c][CHUNK:] for c in cs]
    tinv = [ta[c] + z for c, z in enumerate(mm(ta, bd([x[:CHUNK] for x in r])))]
    for ls in (3, 4, 5):
        off = ((row >> (ls + 1)) == (col >> (ls + 1))) & ((row >> ls) == (col >> ls) + 1)
        y = mm([jnp.where(off, l, 0.0) for l in lmat], bd(tinv))
        tinv = [tinv[c] - z for c, z in enumerate(mm(tinv, bd(y)))]

    eg = [jnp.exp(g) for g in gx]
    u = mm(tinv, bd([v_ref[r, l].astype(F32) * bx[c] for c, (r, l, _) in enumerate(units)]))
    w = mm(tinv, bd([kb[c] * eg[c] for c in cs]))
    for c, (r, l, ch) in enumerate(units):
        u_ref[r, l] = u[c].astype(BF16)
        w_ref[r, l] = w[c].astype(BF16)
        qg_ref[r, l] = (q[c] * eg[c] * scale).astype(BF16)
        g_last = gx[c][CHUNK - 1:CHUNK, :]
        kd_ref[r, l] = (k[c] * jnp.exp(g_last - gx[c])).astype(BF16)
        dl_ref[ch, :, l] = jnp.exp(g_last)


def _gdn_wy_call(gq, gk, gv, small, grow, *, rows):
    m = gq.shape[0]
    cpb = rows // CHUNK
    blk = pl.BlockSpec((rows, WIDTH), lambda i: (i, 0))
    rowblk = pl.BlockSpec((cpb, 1, WIDTH), lambda i: (i, 0, 0))
    bf = jax.ShapeDtypeStruct((m, WIDTH), BF16)
    return pl.pallas_call(
        _gdn_wy_kernel,
        grid=(m // rows,),
        in_specs=[blk] * 3 + [pl.BlockSpec((rows, N_SMALL), lambda i: (i, 0)), rowblk],
        out_specs=[blk] * 5 + [rowblk],
        out_shape=[bf] * 5 + [jax.ShapeDtypeStruct((m // CHUNK, 1, WIDTH), F32)],
        compiler_params=pltpu.CompilerParams(dimension_semantics=("parallel",),
                                             vmem_limit_bytes=VMEM_LIMIT),
    )(gq, gk, gv, small, grow)


def _gdn_scan_kernel(u_ref, w_ref, a_ref, qg_ref, kd_ref, dl_ref, gz_ref, nw_ref, ones_ref,
                     o_ref, s_ref, oacc_ref):
    seq = u_ref.shape[0]
    n_groups = WIDTH // MXU_DIM
    r2 = lax.broadcasted_iota(jnp.int32, (MXU_DIM, MXU_DIM), 0)
    c2 = lax.broadcasted_iota(jnp.int32, (MXU_DIM, MXU_DIM), 1)
    bd_mask = (r2 // HEAD_DIM) == (c2 // HEAD_DIM)
    lane_lo = _lane_lo_mask()
    s_ref[...] = jnp.zeros_like(s_ref)

    def body(c, carry):
        r0 = pl.multiple_of(c * CHUNK, CHUNK)
        rs = pl.ds(r0, CHUNK)
        dl = dl_ref[c]
        for g in range(n_groups):
            ls = slice(g * MXU_DIM, (g + 1) * MXU_DIM)
            state = s_ref[g]
            r = jnp.dot(jnp.concatenate([w_ref[rs, ls], qg_ref[rs, ls]], axis=0),
                        state.astype(BF16), preferred_element_type=F32)
            v_new = u_ref[rs, ls].astype(F32) - r[:CHUNK]
            o = r[CHUNK:] + jnp.dot(a_ref[rs, ls], _block_diag(v_new, lane_lo),
                                    preferred_element_type=F32)
            oacc_ref[rs, ls] = o
            upd = lax.dot_general(kd_ref[rs, ls], v_new.astype(BF16),
                                  (((0,), (0,)), ((), ())), preferred_element_type=F32)
            s_ref[g] = state * dl[:, ls] + jnp.where(bd_mask, upd, 0.0)
        return carry

    lax.fori_loop(0, seq // CHUNK, body, 0)

    o = oacc_ref[...]
    ms = _head_sums(o * o, ones_ref[...]) * (1.0 / HEAD_DIM)
    o_ref[...] = (o * lax.rsqrt(ms + EPS) * nw_ref[...] * gz_ref[...].astype(F32)).astype(BF16)


def _gdn_scan_call(u, w, a, qg, kd, dl, gz, nw, ones_bd, *, seq):
    m = u.shape[0]
    n_chunks = seq // CHUNK
    blk = pl.BlockSpec((seq, WIDTH), lambda b: (b, 0))
    return pl.pallas_call(
        _gdn_scan_kernel,
        grid=(m // seq,),
        in_specs=[blk] * 5 + [pl.BlockSpec((n_chunks, 1, WIDTH), lambda b: (b, 0, 0)), blk,
                              _const_spec(nw.shape), _const_spec(ones_bd.shape)],
        out_specs=blk,
        out_shape=jax.ShapeDtypeStruct((m, WIDTH), BF16),
        scratch_shapes=[pltpu.VMEM((WIDTH // MXU_DIM, MXU_DIM, MXU_DIM), F32),
                        pltpu.VMEM((seq, WIDTH), F32)],
        compiler_params=pltpu.CompilerParams(dimension_semantics=("parallel",),
                                             vmem_limit_bytes=VMEM_LIMIT),
    )(u, w, a, qg, kd, dl, gz, nw, ones_bd)


FOX_TQ = 256
FOX_TK = 256


def _fox_kernel(tk, q_ref, qx_ref, k_ref, kx_ref, v_ref, gate_ref, o_ref, vt_ref):
    qi = pl.program_id(1)
    tq = q_ref.shape[0]
    sub = tq // tk
    heads = range(HEADS)

    @pl.when(qi == 0)
    def _():
        vt_ref[...] = v_ref[...].T

    lane = lax.broadcasted_iota(jnp.int32, (tq, LANES), 1)
    ext_row = lax.broadcasted_iota(jnp.int32, (LANES, tq), 0)
    qx_t = qx_ref[...].T
    q_t = []
    for h in heads:
        pair = q_ref[:, (h // 2) * LANES:(h // 2 + 1) * LANES]
        keep = (lane < HEAD_DIM) if h % 2 == 0 else (lane >= HEAD_DIM)
        own = (ext_row >= EXT_STRIDE * h) & (ext_row < EXT_STRIDE * (h + 1))
        q_t.append(jnp.concatenate([jnp.where(keep, pair, jnp.zeros_like(pair)).T,
                                    jnp.where(own, qx_t, jnp.zeros_like(qx_t))], axis=0))
    kv_pos = lax.broadcasted_iota(jnp.int32, (tk, tq), 0)
    q_pos = lax.broadcasted_iota(jnp.int32, (tk, tq), 1)

    def tile(j, carry, diag_offset):
        k0 = pl.multiple_of(j * tk, tk)
        kx = kx_ref[pl.ds(k0, tk), :]
        s = []
        for h in heads:
            k = jnp.concatenate([k_ref[pl.ds(k0, tk), (h // 2) * LANES:(h // 2 + 1) * LANES], kx], axis=1)
            s_h = jnp.dot(k, q_t[h], preferred_element_type=F32)
            if diag_offset is not None:
                s_h = jnp.where(kv_pos + diag_offset <= q_pos, s_h, NEG_BIG)
            s.append(s_h)
        m_new = [jnp.maximum(carry[h][0], jnp.max(s[h], axis=0, keepdims=True)) for h in heads]
        p = [jnp.exp2(s[h] - m_new[h]) for h in heads]
        alpha = [jnp.exp2(carry[h][0] - m_new[h]) for h in heads]
        l_new = [alpha[h] * carry[h][1] + jnp.sum(p[h], axis=0, keepdims=True) for h in heads]
        pv = [jnp.dot(vt_ref[h * HEAD_DIM:(h + 1) * HEAD_DIM, pl.ds(k0, tk)], p[h].astype(BF16),
                      preferred_element_type=F32) for h in heads]
        return tuple((m_new[h], l_new[h], alpha[h] * carry[h][2] + pv[h]) for h in heads)

    init = tuple((jnp.full((1, tq), NEG_BIG, F32), jnp.zeros((1, tq), F32), jnp.zeros((HEAD_DIM, tq), F32))
                 for _ in heads)
    carry = lax.fori_loop(0, qi * sub, lambda j, c: tile(j, c, None), init)
    for d in range(sub):
        carry = tile(qi * sub + d, carry, d * tk)
    o_t = jnp.concatenate([acc / l_sum for _, l_sum, acc in carry], axis=0)
    o_ref[...] = (o_t.T * gate_ref[...].astype(F32)).astype(BF16)


def _fox_call(fq, qx, fk, kx, fv, gate, *, seq, tq, tk):
    m = fq.shape[0]
    nq = seq // tq
    qblk = pl.BlockSpec((tq, WIDTH), lambda b, i: (b * nq + i, 0))
    kvblk = pl.BlockSpec((seq, WIDTH), lambda b, i: (b, 0))
    return pl.pallas_call(
        functools.partial(_fox_kernel, tk),
        grid=(m // seq, nq),
        in_specs=[qblk, pl.BlockSpec((tq, LANES), lambda b, i: (b * nq + i, 0)),
                  kvblk, pl.BlockSpec((seq, LANES), lambda b, i: (b, 0)), kvblk, qblk],
        out_specs=qblk,
        out_shape=jax.ShapeDtypeStruct((m, WIDTH), BF16),
        scratch_shapes=[pltpu.VMEM((WIDTH, seq), BF16)],
        compiler_params=pltpu.CompilerParams(dimension_semantics=("parallel", "arbitrary"),
                                             vmem_limit_bytes=VMEM_LIMIT),
    )(fq, qx, fk, kx, fv, gate)


FF_SPLITS = ((0, 1024), (1024, 1024), (2048, 768))


def _ffn_kernel(final, mg_ref, mf_ref, x_ref, wo_ref, n2_ref, wg_ref, wu_ref, wd_ref, fn_ref, o_ref):
    x1 = (x_ref[...]
          + jnp.dot(mg_ref[...], wo_ref[0:WIDTH, :], preferred_element_type=F32)
          + jnp.dot(mf_ref[...], wo_ref[WIDTH:2 * WIDTH, :], preferred_element_type=F32))
    h = (x1 * lax.rsqrt(jnp.mean(x1 * x1, axis=-1, keepdims=True) + EPS) * n2_ref[...]).astype(BF16)
    ffn = None
    for start, size in FF_SPLITS:
        gate = jnp.dot(h, wg_ref[:, start:start + size], preferred_element_type=F32)
        up = jnp.dot(h, wu_ref[:, start:start + size], preferred_element_type=F32)
        act = (gate * _sigmoid(gate) * up).astype(BF16)
        down = jnp.dot(act, wd_ref[start:start + size, :], preferred_element_type=F32)
        ffn = down if ffn is None else ffn + down
    y = x1 + ffn
    if final:
        y = y * lax.rsqrt(jnp.mean(y * y, axis=-1, keepdims=True) + EPS) * fn_ref[...]
    o_ref[...] = y


def _ffn_call(mix_g, mix_f, x2, wo, n2w, wg, wu, wd, fnw, *, tm, final):
    m = x2.shape[0]
    half = pl.BlockSpec((tm, WIDTH), lambda i: (i, 0))
    full = pl.BlockSpec((tm, D_MODEL), lambda i: (i, 0))

    def resident(shape):
        return pl.BlockSpec(shape, lambda i: (0, 0), pipeline_mode=pl.Buffered(1))

    return pl.pallas_call(
        functools.partial(_ffn_kernel, final),
        grid=(m // tm,),
        in_specs=[half, half, full, resident(wo.shape), resident(n2w.shape), resident(wg.shape),
                  resident(wu.shape), resident(wd.shape), resident(fnw.shape)],
        out_specs=full,
        out_shape=jax.ShapeDtypeStruct((m, D_MODEL), F32),
        compiler_params=pltpu.CompilerParams(dimension_semantics=("parallel",),
                                             vmem_limit_bytes=VMEM_LIMIT),
    )(mix_g, mix_f, x2, wo, n2w, wg, wu, wd, fnw)


def _lane_vec(parts):
    v = jnp.concatenate([p.astype(F32).reshape(-1) for p in parts])
    return jnp.pad(v, (0, N_SMALL - v.shape[0])).reshape(1, N_SMALL)


def kernel(x, norm1_w, w_in, gdn_conv_w, gdn_A_log, gdn_dt_bias, gdn_out_norm_w, fox_f_bias, fox_q_norm_w, fox_k_norm_w, w_out, norm2_w, w_ffn_gate, w_ffn_up, w_ffn_down, final_norm_w):
    batch, seq, _ = x.shape
    m = batch * seq
    depth = norm1_w.shape[0]
    zeros8 = jnp.zeros((HEADS,), F32)
    r2 = jnp.arange(MXU_DIM) // HEAD_DIM
    ones_bd = (r2[:, None] == r2[None, :]).astype(BF16)

    x2 = x.reshape(m, D_MODEL)
    for l in range(depth):
        w = w_in[l]
        big = 4 * WIDTH
        g_small = w[:, big:big + 2 * HEADS]
        f_start = big + 2 * HEADS
        f_small = w[:, f_start + big:f_start + big + HEADS]
        w_all = jnp.concatenate(
            [w[:, :big], w[:, f_start:f_start + big], g_small, f_small,
             jnp.zeros((D_MODEL, N_SMALL - 3 * HEADS), w.dtype)], axis=1).astype(BF16)
        bias_vec = _lane_vec([zeros8, gdn_dt_bias[l], fox_f_bias[l]])
        alog_vec = _lane_vec([zeros8, gdn_A_log[l], zeros8])
        fqw = (jnp.tile(fox_q_norm_w[l].astype(F32), HEADS) * (HEAD_DIM ** -0.5 * LOG2E)).reshape(1, WIDTH)
        fkw = jnp.tile(fox_k_norm_w[l].astype(F32), HEADS).reshape(1, WIDTH)

        gq, gk, gv, gz, fq, fk, fv, fg, small, kx, qx = _inproj_call(
            x2, norm1_w[l].reshape(1, D_MODEL), w_all, gdn_conv_w[l].astype(F32), bias_vec, alog_vec,
            fqw, fkw, ones_bd, seq=seq, tm=512)

        g_cum = small[:, HEADS:2 * HEADS]
        grow = g_cum.reshape(m // CHUNK, CHUNK, HEADS).transpose(0, 2, 1).reshape(m // CHUNK, 1, WIDTH)

        u, wmat, amat, qg, kd, dl = _gdn_wy_call(gq, gk, gv, small, grow, rows=256)
        nw = jnp.tile(gdn_out_norm_w[l].astype(F32), HEADS).reshape(1, WIDTH)
        mix_g = _gdn_scan_call(u, wmat, amat, qg, kd, dl, gz, nw, ones_bd, seq=seq)
        mix_f = _fox_call(fq, qx, fk, kx, fv, fg, seq=seq, tq=FOX_TQ, tk=FOX_TK)

        x2 = _ffn_call(mix_g, mix_f, x2, w_out[l].astype(BF16), norm2_w[l].reshape(1, D_MODEL),
                       w_ffn_gate[l].astype(BF16), w_ffn_up[l].astype(BF16), w_ffn_down[l].astype(BF16),
                       final_norm_w.reshape(1, D_MODEL), tm=512, final=(l == depth - 1))
    return x2.reshape(batch, seq, D_MODEL)
```

```python
import functools

import numpy as np
import jax
import jax.numpy as jnp
from jax import lax
from jax.experimental import pallas as pl
from jax.experimental.pallas import tpu as pltpu

D_MODEL = 1024
HEADS = 8
HEAD_DIM = 64
WIDTH = HEADS * HEAD_DIM
CONV_K = 4
CHUNK = 64
D_FF = 2816
EPS = 1e-6

LANES = 128
MXU_DIM = 256
PACK = MXU_DIM // HEAD_DIM
N_SMALL = LANES
VMEM_LIMIT = 56 * 1024 * 1024

F32 = jnp.float32
BF16 = jnp.bfloat16
NEG_BIG = -1e30
LOG2E = 1.4426950408889634
EXT_STRIDE = 6


def _const_spec(shape):
    nd = len(shape)
    return pl.BlockSpec(shape, lambda *_: (0,) * nd)


def _sigmoid(x):
    return 1.0 / (1.0 + jnp.exp(-x))


def _softplus(x):
    return jnp.maximum(x, 0.0) + jnp.log1p(jnp.exp(-jnp.abs(x)))


def _head_sums(y2, ones_bd):
    yb = y2.astype(BF16)
    parts = [jnp.dot(yb[:, c * MXU_DIM:(c + 1) * MXU_DIM], ones_bd, preferred_element_type=F32)
             for c in range(WIDTH // MXU_DIM)]
    return jnp.concatenate(parts, axis=1)


def _seg_cumsum(v, seg):
    pos = lax.broadcasted_iota(jnp.int32, v.shape, 0) & (seg - 1)
    d = 1
    while d < seg:
        v = v + jnp.where(pos >= d, pltpu.roll(v, d, 0), 0.0)
        d *= 2
    return v


def _inproj_kernel(tiles_per_seq, x_ref, n1_ref, wg_ref, wf_ref, ws_ref, cw_ref, bias_ref, alog_ref, fqw_ref, fkw_ref,
                   ones_ref, sel_ref, one_ref,
                   gq_ref, gk_ref, gv_ref, gz_ref, fq_ref, fk_ref, fv_ref, fg_ref, sm_ref, kx_ref, qx_ref,
                   tail_ref, carry_ref):
    i = pl.program_id(0)
    tm = x_ref.shape[0]

    @pl.when(i % tiles_per_seq == 0)
    def _():
        tail_ref[...] = jnp.zeros_like(tail_ref)
        carry_ref[...] = jnp.zeros_like(carry_ref)

    x = x_ref[...]
    h = (x * lax.rsqrt(jnp.mean(x * x, axis=-1, keepdims=True) + EPS) * n1_ref[...]).astype(BF16)
    ones_bd = ones_ref[...]

    def proj(s):
        if s == 8:
            return jnp.dot(h, ws_ref[...], preferred_element_type=F32)
        w_ref = wg_ref if s < 4 else wf_ref
        return jnp.dot(h, w_ref[:, (s % 4) * WIDTH:(s % 4 + 1) * WIDTH], preferred_element_type=F32)

    for s, out_ref in enumerate((gq_ref, gk_ref, gv_ref)):
        y = proj(s)
        cw = cw_ref[:, s * WIDTH:(s + 1) * WIDTH]
        head = jnp.concatenate([tail_ref[s], y[:8]], axis=0)
        acc = y * cw[CONV_K - 1:CONV_K]
        acc_head = y[:8] * cw[CONV_K - 1:CONV_K]
        for k in range(1, CONV_K):
            tap = cw[CONV_K - 1 - k:CONV_K - k]
            acc = acc + pltpu.roll(y, k, 0) * tap
            acc_head = acc_head + pltpu.roll(head, k, 0)[8:] * tap
        tail_ref[s] = y[tm - 8:]
        act_head = acc_head * _sigmoid(acc_head)
        act = acc * _sigmoid(acc)
        if s < 2:
            act = act * lax.rsqrt(_head_sums(act * act, ones_bd) + EPS)
            act_head = act_head * lax.rsqrt(_head_sums(act_head * act_head, ones_bd) + EPS)
        out_ref[...] = act.astype(BF16)
        out_ref[0:8, :] = act_head.astype(BF16)

    z = proj(3)
    gz_ref[...] = (z * _sigmoid(z)).astype(BF16)

    for s, out_ref, w_norm in ((4, fq_ref, fqw_ref), (5, fk_ref, fkw_ref)):
        y = proj(s)
        ms = _head_sums(y * y, ones_bd) * (1.0 / HEAD_DIM)
        out_ref[...] = (y * lax.rsqrt(ms + EPS) * w_norm[...]).astype(BF16)

    fv_ref[...] = proj(6).astype(BF16)
    fg_ref[...] = _sigmoid(proj(7)).astype(BF16)

    t = proj(8) + bias_ref[...]
    lane = lax.broadcasted_iota(jnp.int32, t.shape, 1)
    beta = _sigmoid(t)
    g_log = -jnp.exp(alog_ref[...]) * _softplus(t)
    log_f = -_softplus(-t)
    g_cum = _seg_cumsum(g_log, CHUNK)
    f_cum = _seg_cumsum(log_f, tm) + carry_ref[0:1, :]
    carry_ref[0:1, :] = f_cum[tm - 1:tm, :]
    sm_ref[...] = jnp.where(lane < HEADS, beta, jnp.where(lane < 2 * HEADS, g_cum, f_cum))

    f2 = f_cum * LOG2E
    hi = f2.astype(BF16)
    r1 = f2 - hi.astype(F32)
    mid = r1.astype(BF16)
    lo = (r1 - mid.astype(F32)).astype(BF16)
    ext = jnp.dot(jnp.concatenate([hi, mid, lo], axis=1), sel_ref[...], preferred_element_type=F32) + one_ref[...]
    kx_ref[...] = ext[:, :LANES].astype(BF16)
    qx_ref[...] = ext[:, LANES:].astype(BF16)


def _bias_selectors():
    sel = np.zeros((3 * LANES, 2 * LANES), np.float32)
    one = np.zeros((1, 2 * LANES), np.float32)
    for h in range(HEADS):
        src = 2 * HEADS + h
        for part in range(3):
            sel[part * LANES + src, EXT_STRIDE * h + part] = -1.0
            sel[part * LANES + src, LANES + EXT_STRIDE * h + 3 + part] = 1.0
            one[0, EXT_STRIDE * h + 3 + part] = 1.0
            one[0, LANES + EXT_STRIDE * h + part] = 1.0
    return jnp.asarray(sel, BF16), jnp.asarray(one, F32)


def _inproj_call(x2, n1w, w_gdn, w_fox, w_gates, conv_w, bias_vec, alog_vec, fqw, fkw, ones_bd, *, seq, tm):
    m = x2.shape[0]
    assert seq % tm == 0 and tm % CHUNK == 0 and (tm & (tm - 1)) == 0
    sel, one = _bias_selectors()
    wide = pl.BlockSpec((tm, WIDTH), lambda i: (i, 0))
    narrow = pl.BlockSpec((tm, LANES), lambda i: (i, 0))
    out_shape = ([jax.ShapeDtypeStruct((m, WIDTH), BF16)] * 8 + [jax.ShapeDtypeStruct((m, N_SMALL), F32)]
                 + [jax.ShapeDtypeStruct((m, LANES), BF16)] * 2)
    return pl.pallas_call(
        functools.partial(_inproj_kernel, seq // tm),
        grid=(m // tm,),
        in_specs=[pl.BlockSpec((tm, D_MODEL), lambda i: (i, 0)),
                  _const_spec(n1w.shape), _const_spec(w_gdn.shape), _const_spec(w_fox.shape),
                  _const_spec(w_gates.shape), _const_spec(conv_w.shape),
                  _const_spec(bias_vec.shape), _const_spec(alog_vec.shape),
                  _const_spec(fqw.shape), _const_spec(fkw.shape), _const_spec(ones_bd.shape),
                  _const_spec(sel.shape), _const_spec(one.shape)],
        out_specs=[wide] * 8 + [narrow] * 3,
        out_shape=out_shape,
        scratch_shapes=[pltpu.VMEM((3, 8, WIDTH), F32), pltpu.VMEM((8, N_SMALL), F32)],
        compiler_params=pltpu.CompilerParams(dimension_semantics=("arbitrary",),
                                             vmem_limit_bytes=VMEM_LIMIT),
    )(x2, n1w, w_gdn, w_fox, w_gates, conv_w, bias_vec, alog_vec, fqw, fkw, ones_bd, sel, one)


def _lane_lo_mask():
    return lax.broadcasted_iota(jnp.int32, (CHUNK, LANES), 1) < HEAD_DIM


def _block_diag(x, lane_lo):
    xb = x.astype(BF16)
    zero = jnp.zeros((CHUNK, LANES), BF16)
    blocks = []
    for h in range(PACK):
        half = xb[:, (h // 2) * LANES:(h // 2 + 1) * LANES]
        keep = jnp.where(lane_lo if h % 2 == 0 else jnp.logical_not(lane_lo), half, zero)
        blocks.append(jnp.concatenate([keep, zero] if h < 2 else [zero, keep], axis=1))
    return jnp.concatenate(blocks, axis=0)


def _mm(lhs, rhs_bd):
    return jnp.dot(lhs.astype(BF16), rhs_bd, preferred_element_type=F32)


def _expand_heads(sm, first_lane):
    lane_lo = lax.broadcasted_iota(jnp.int32, (sm.shape[0], LANES), 1) < HEAD_DIM
    parts = []
    for p in range(HEADS // 2):
        c = first_lane + 2 * p
        even = jnp.broadcast_to(sm[:, c:c + 1], (sm.shape[0], LANES))
        odd = jnp.broadcast_to(sm[:, c + 1:c + 2], (sm.shape[0], LANES))
        parts.append(jnp.where(lane_lo, even, odd))
    return jnp.concatenate(parts, axis=1)


def _gdn_wy_kernel(q_ref, k_ref, v_ref, sm_ref, grow_ref,
                   u_ref, w_ref, a_ref, qg_ref, kd_ref, dl_ref):
    n_chunks = q_ref.shape[0] // CHUNK
    n_groups = WIDTH // MXU_DIM
    shape = (CHUNK, MXU_DIM)
    row = lax.broadcasted_iota(jnp.int32, shape, 0)
    col = lax.broadcasted_iota(jnp.int32, shape, 1) & (HEAD_DIM - 1)
    causal = row >= col
    strict = row > col
    eye = (row == col).astype(F32)
    bd_mask = _lane_lo_mask()
    scale = HEAD_DIM ** -0.5

    units = [(slice(c * CHUNK, (c + 1) * CHUNK), slice(g * MXU_DIM, (g + 1) * MXU_DIM), c)
             for c in range(n_chunks) for g in range(n_groups)]
    cs = range(len(units))
    bd = lambda xs: [_block_diag(x, bd_mask) for x in xs]
    mm = lambda ls, rs_: [_mm(a, b) for a, b in zip(ls, rs_)]
    stack = lambda xs, ys: [jnp.concatenate([x, y], axis=0) for x, y in zip(xs, ys)]

    sm = sm_ref[...]
    bx_all = _expand_heads(sm, 0)
    gx_all = _expand_heads(sm, HEADS)
    q = [q_ref[r, l].astype(F32) for r, l, _ in units]
    k = [k_ref[r, l].astype(F32) for r, l, _ in units]
    bx = [bx_all[r, l] for r, l, _ in units]
    gx = [gx_all[r, l] for r, l, _ in units]
    kb = [k[c] * bx[c] for c in cs]
    decay = [jnp.exp(jnp.where(causal, gx[c] - grow_ref[ch, :, l], NEG_BIG))
             for c, (_, l, ch) in enumerate(units)]

    bd_k = bd(k)
    s1 = [lax.dot_general(jnp.concatenate([kb[c], q[c]], axis=0).astype(BF16), bd_k[c],
                          (((1,), (1,)), ((), ())), preferred_element_type=F32) for c in cs]
    lmat = [jnp.where(strict, s1[c][:CHUNK] * decay[c], 0.0) for c in cs]
    for c, (r, l, _) in enumerate(units):
        a_ref[r, l] = (s1[c][CHUNK:] * decay[c] * scale).astype(BF16)

    blk8 = (row >> 3) == (col >> 3)
    n8 = [jnp.where(blk8, -l, 0.0) for l in lmat]
    t0 = [eye + n for n in n8]
    p1 = mm(n8, bd(n8))
    r = mm(stack(p1, t0), bd(p1))
    ta = [t0[c] + r[c][CHUNK:] for c in cs]
    tinv = [ta[c] + z for c, z in enumerate(mm(ta, bd([x[:CHUNK] for x in r])))]
    for ls in (3, 4, 5):
        off = ((row >> (ls + 1)) == (col >> (ls + 1))) & ((row >> ls) == (col >> ls) + 1)
        y = mm([jnp.where(off, l, 0.0) for l in lmat], bd(tinv))
        tinv = [tinv[c] - z for c, z in enumerate(mm(tinv, bd(y)))]

    eg = [jnp.exp(g) for g in gx]
    u = mm(tinv, bd([v_ref[r, l].astype(F32) * bx[c] for c, (r, l, _) in enumerate(units)]))
    w = mm(tinv, bd([kb[c] * eg[c] for c in cs]))
    for c, (r, l, ch) in enumerate(units):
        u_ref[r, l] = u[c].astype(BF16)
        w_ref[r, l] = w[c].astype(BF16)
        qg_ref[r, l] = (q[c] * eg[c] * scale).astype(BF16)
        g_last = gx[c][CHUNK - 1:CHUNK, :]
        kd_ref[r, l] = (k[c] * jnp.exp(g_last - gx[c])).astype(BF16)
        dl_ref[ch, :, l] = jnp.exp(g_last)


def _gdn_wy_call(gq, gk, gv, small, grow, *, rows):
    m = gq.shape[0]
    cpb = rows // CHUNK
    blk = pl.BlockSpec((rows, WIDTH), lambda i: (i, 0))
    rowblk = pl.BlockSpec((cpb, 1, WIDTH), lambda i: (i, 0, 0))
    bf = jax.ShapeDtypeStruct((m, WIDTH), BF16)
    return pl.pallas_call(
        _gdn_wy_kernel,
        grid=(m // rows,),
        in_specs=[blk] * 3 + [pl.BlockSpec((rows, N_SMALL), lambda i: (i, 0)), rowblk],
        out_specs=[blk] * 5 + [rowblk],
        out_shape=[bf] * 5 + [jax.ShapeDtypeStruct((m // CHUNK, 1, WIDTH), F32)],
        compiler_params=pltpu.CompilerParams(dimension_semantics=("parallel",),
                                             vmem_limit_bytes=VMEM_LIMIT),
    )(gq, gk, gv, small, grow)


def _gdn_scan_kernel(u_ref, w_ref, a_ref, qg_ref, kd_ref, dl_ref, gz_ref, nw_ref, ones_ref,
                     o_ref, s_ref, oacc_ref):
    n_seq, rows, _ = u_ref.shape
    n_groups = WIDTH // MXU_DIM
    chains = [(b, g, slice(g * MXU_DIM, (g + 1) * MXU_DIM)) for b in range(n_seq) for g in range(n_groups)]
    r2 = lax.broadcasted_iota(jnp.int32, (MXU_DIM, MXU_DIM), 0)
    c2 = lax.broadcasted_iota(jnp.int32, (MXU_DIM, MXU_DIM), 1)
    bd_mask = (r2 // HEAD_DIM) == (c2 // HEAD_DIM)
    lane_lo = _lane_lo_mask()

    @pl.when(pl.program_id(1) == 0)
    def _():
        s_ref[...] = jnp.zeros_like(s_ref)

    def body(c, carry):
        rs = pl.ds(pl.multiple_of(c * CHUNK, CHUNK), CHUNK)
        state = [s_ref[b, g] for b, g, _ in chains]
        r = [jnp.dot(jnp.concatenate([w_ref[b, rs, l], qg_ref[b, rs, l]], axis=0),
                     state[i].astype(BF16), preferred_element_type=F32) for i, (b, _, l) in enumerate(chains)]
        v_new = [u_ref[b, rs, l].astype(F32) - r[i][:CHUNK] for i, (b, _, l) in enumerate(chains)]
        upd = [lax.dot_general(kd_ref[b, rs, l], v_new[i].astype(BF16), (((0,), (0,)), ((), ())),
                               preferred_element_type=F32) for i, (b, _, l) in enumerate(chains)]
        for i, (b, g, l) in enumerate(chains):
            s_ref[b, g] = state[i] * dl_ref[b, c, :, l] + jnp.where(bd_mask, upd[i], 0.0)
        for i, (b, _, l) in enumerate(chains):
            oacc_ref[b, rs, l] = r[i][CHUNK:] + jnp.dot(a_ref[b, rs, l], _block_diag(v_new[i], lane_lo),
                                                        preferred_element_type=F32)
        return carry

    lax.fori_loop(0, rows // CHUNK, body, 0)

    for b in range(n_seq):
        o = oacc_ref[b]
        ms = _head_sums(o * o, ones_ref[...]) * (1.0 / HEAD_DIM)
        o_ref[b] = (o * lax.rsqrt(ms + EPS) * nw_ref[...] * gz_ref[b].astype(F32)).astype(BF16)


def _gdn_scan_call(u, w, a, qg, kd, dl, gz, nw, ones_bd, *, seq, n_seq, rows):
    batch = u.shape[0] // seq
    as3d = lambda t: t.reshape(batch, seq, WIDTH)
    blk = pl.BlockSpec((n_seq, rows, WIDTH), lambda b, t: (b, t, 0))
    dl4 = dl.reshape(batch, seq // CHUNK, 1, WIDTH)
    out = pl.pallas_call(
        _gdn_scan_kernel,
        grid=(batch // n_seq, seq // rows),
        in_specs=[blk] * 5 + [pl.BlockSpec((n_seq, rows // CHUNK, 1, WIDTH), lambda b, t: (b, t, 0, 0)), blk,
                              _const_spec(nw.shape), _const_spec(ones_bd.shape)],
        out_specs=blk,
        out_shape=jax.ShapeDtypeStruct((batch, seq, WIDTH), BF16),
        scratch_shapes=[pltpu.VMEM((n_seq, WIDTH // MXU_DIM, MXU_DIM, MXU_DIM), F32),
                        pltpu.VMEM((n_seq, rows, WIDTH), F32)],
        compiler_params=pltpu.CompilerParams(dimension_semantics=("parallel", "arbitrary"),
                                             vmem_limit_bytes=VMEM_LIMIT),
    )(as3d(u), as3d(w), as3d(a), as3d(qg), as3d(kd), dl4, as3d(gz), nw, ones_bd)
    return out.reshape(batch * seq, WIDTH)


FOX_TQ = 256
FOX_TK = 256


def _fox_kernel(tk, q_ref, qx_ref, k_ref, kx_ref, v_ref, gate_ref, o_ref, vt_ref):
    qi = pl.program_id(1)
    tq = q_ref.shape[0]
    sub = tq // tk
    heads = range(HEADS)

    @pl.when(qi == 0)
    def _():
        vt_ref[...] = v_ref[...].T

    lane = lax.broadcasted_iota(jnp.int32, (tq, LANES), 1)
    ext_row = lax.broadcasted_iota(jnp.int32, (LANES, tq), 0)
    qx_t = qx_ref[...].T
    q_t = []
    for h in heads:
        pair = q_ref[:, (h // 2) * LANES:(h // 2 + 1) * LANES]
        keep = (lane < HEAD_DIM) if h % 2 == 0 else (lane >= HEAD_DIM)
        own = (ext_row >= EXT_STRIDE * h) & (ext_row < EXT_STRIDE * (h + 1))
        q_t.append(jnp.concatenate([jnp.where(keep, pair, jnp.zeros_like(pair)).T,
                                    jnp.where(own, qx_t, jnp.zeros_like(qx_t))], axis=0))
    kv_pos = lax.broadcasted_iota(jnp.int32, (tk, tq), 0)
    q_pos = lax.broadcasted_iota(jnp.int32, (tk, tq), 1)

    def tile(j, carry, diag_offset):
        k0 = pl.multiple_of(j * tk, tk)
        kx = kx_ref[pl.ds(k0, tk), :]
        s = []
        for h in heads:
            k = jnp.concatenate([k_ref[pl.ds(k0, tk), (h // 2) * LANES:(h // 2 + 1) * LANES], kx], axis=1)
            s_h = jnp.dot(k, q_t[h], preferred_element_type=F32)
            if diag_offset is not None:
                s_h = jnp.where(kv_pos + diag_offset <= q_pos, s_h, NEG_BIG)
            s.append(s_h)
        m_new = [jnp.maximum(carry[h][0], jnp.max(s[h], axis=0, keepdims=True)) for h in heads]
        p = [jnp.exp2(s[h] - m_new[h]) for h in heads]
        alpha = [jnp.exp2(carry[h][0] - m_new[h]) for h in heads]
        l_new = [alpha[h] * carry[h][1] + jnp.sum(p[h], axis=0, keepdims=True) for h in heads]
        pv = [jnp.dot(vt_ref[h * HEAD_DIM:(h + 1) * HEAD_DIM, pl.ds(k0, tk)], p[h].astype(BF16),
                      preferred_element_type=F32) for h in heads]
        return tuple((m_new[h], l_new[h], alpha[h] * carry[h][2] + pv[h]) for h in heads)

    init = tuple((jnp.full((1, tq), NEG_BIG, F32), jnp.zeros((1, tq), F32), jnp.zeros((HEAD_DIM, tq), F32))
                 for _ in heads)
    carry = lax.fori_loop(0, qi * sub, lambda j, c: tile(j, c, None), init)
    for d in range(sub):
        carry = tile(qi * sub + d, carry, d * tk)
    o_t = jnp.concatenate([acc / l_sum for _, l_sum, acc in carry], axis=0)
    o_ref[...] = (o_t.T * gate_ref[...].astype(F32)).astype(BF16)


def _fox_call(fq, qx, fk, kx, fv, gate, *, seq, tq, tk):
    m = fq.shape[0]
    nq = seq // tq
    qblk = pl.BlockSpec((tq, WIDTH), lambda b, i: (b * nq + i, 0))
    kvblk = pl.BlockSpec((seq, WIDTH), lambda b, i: (b, 0))
    return pl.pallas_call(
        functools.partial(_fox_kernel, tk),
        grid=(m // seq, nq),
        in_specs=[qblk, pl.BlockSpec((tq, LANES), lambda b, i: (b * nq + i, 0)),
                  kvblk, pl.BlockSpec((seq, LANES), lambda b, i: (b, 0)), kvblk, qblk],
        out_specs=qblk,
        out_shape=jax.ShapeDtypeStruct((m, WIDTH), BF16),
        scratch_shapes=[pltpu.VMEM((WIDTH, seq), BF16)],
        compiler_params=pltpu.CompilerParams(dimension_semantics=("parallel", "arbitrary"),
                                             vmem_limit_bytes=VMEM_LIMIT),
    )(fq, qx, fk, kx, fv, gate)


FF_SPLITS = ((0, 1024), (1024, 1024), (2048, 768))


def _ffn_kernel(final, mg_ref, mf_ref, x_ref, wo_ref, n2_ref, wg_ref, wu_ref, wd_ref, fn_ref, o_ref):
    x1 = (x_ref[...]
          + jnp.dot(mg_ref[...], wo_ref[0:WIDTH, :], preferred_element_type=F32)
          + jnp.dot(mf_ref[...], wo_ref[WIDTH:2 * WIDTH, :], preferred_element_type=F32))
    h = (x1 * lax.rsqrt(jnp.mean(x1 * x1, axis=-1, keepdims=True) + EPS) * n2_ref[...]).astype(BF16)
    ffn = None
    for start, size in FF_SPLITS:
        gate = jnp.dot(h, wg_ref[:, start:start + size], preferred_element_type=F32)
        up = jnp.dot(h, wu_ref[:, start:start + size], preferred_element_type=F32)
        act = (gate * _sigmoid(gate) * up).astype(BF16)
        down = jnp.dot(act, wd_ref[start:start + size, :], preferred_element_type=F32)
        ffn = down if ffn is None else ffn + down
    y = x1 + ffn
    if final:
        y = y * lax.rsqrt(jnp.mean(y * y, axis=-1, keepdims=True) + EPS) * fn_ref[...]
    o_ref[...] = y


def _ffn_call(mix_g, mix_f, x2, wo, n2w, wg, wu, wd, fnw, *, tm, final):
    m = x2.shape[0]
    half = pl.BlockSpec((tm, WIDTH), lambda i: (i, 0))
    full = pl.BlockSpec((tm, D_MODEL), lambda i: (i, 0))

    def resident(shape):
        return pl.BlockSpec(shape, lambda i: (0, 0), pipeline_mode=pl.Buffered(1))

    return pl.pallas_call(
        functools.partial(_ffn_kernel, final),
        grid=(m // tm,),
        in_specs=[half, half, full, resident(wo.shape), resident(n2w.shape), resident(wg.shape),
                  resident(wu.shape), resident(wd.shape), resident(fnw.shape)],
        out_specs=full,
        out_shape=jax.ShapeDtypeStruct((m, D_MODEL), F32),
        compiler_params=pltpu.CompilerParams(dimension_semantics=("parallel",),
                                             vmem_limit_bytes=VMEM_LIMIT),
    )(mix_g, mix_f, x2, wo, n2w, wg, wu, wd, fnw)


def _lane_vec(parts):
    v = jnp.concatenate([p.astype(F32).reshape(-1) for p in parts])
    return jnp.pad(v, (0, N_SMALL - v.shape[0])).reshape(1, N_SMALL)


def kernel(x, norm1_w, w_in, gdn_conv_w, gdn_A_log, gdn_dt_bias, gdn_out_norm_w, fox_f_bias, fox_q_norm_w, fox_k_norm_w, w_out, norm2_w, w_ffn_gate, w_ffn_up, w_ffn_down, final_norm_w):
    batch, seq, _ = x.shape
    m = batch * seq
    depth = norm1_w.shape[0]
    zeros8 = jnp.zeros((HEADS,), F32)
    r2 = jnp.arange(MXU_DIM) // HEAD_DIM
    ones_bd = (r2[:, None] == r2[None, :]).astype(BF16)

    x2 = x.reshape(m, D_MODEL)
    for l in range(depth):
        w = w_in[l]
        big = 4 * WIDTH
        g_small = w[:, big:big + 2 * HEADS]
        f_start = big + 2 * HEADS
        f_small = w[:, f_start + big:f_start + big + HEADS]
        w_gdn = w[:, :big].astype(BF16)
        w_fox = w[:, f_start:f_start + big].astype(BF16)
        w_gates = jnp.concatenate(
            [g_small, f_small, jnp.zeros((D_MODEL, N_SMALL - 3 * HEADS), w.dtype)], axis=1).astype(BF16)
        bias_vec = _lane_vec([zeros8, gdn_dt_bias[l], fox_f_bias[l]])
        alog_vec = _lane_vec([zeros8, gdn_A_log[l], zeros8])
        fqw = (jnp.tile(fox_q_norm_w[l].astype(F32), HEADS) * (HEAD_DIM ** -0.5 * LOG2E)).reshape(1, WIDTH)
        fkw = jnp.tile(fox_k_norm_w[l].astype(F32), HEADS).reshape(1, WIDTH)

        gq, gk, gv, gz, fq, fk, fv, fg, small, kx, qx = _inproj_call(
            x2, norm1_w[l].reshape(1, D_MODEL), w_gdn, w_fox, w_gates, gdn_conv_w[l].astype(F32), bias_vec, alog_vec,
            fqw, fkw, ones_bd, seq=seq, tm=512)

        g_cum = small[:, HEADS:2 * HEADS]
        grow = g_cum.reshape(m // CHUNK, CHUNK, HEADS).transpose(0, 2, 1).reshape(m // CHUNK, 1, WIDTH)

        u, wmat, amat, qg, kd, dl = _gdn_wy_call(gq, gk, gv, small, grow, rows=512)
        nw = jnp.tile(gdn_out_norm_w[l].astype(F32), HEADS).reshape(1, WIDTH)
        mix_g = _gdn_scan_call(u, wmat, amat, qg, kd, dl, gz, nw, ones_bd, seq=seq, n_seq=2, rows=512)
        mix_f = _fox_call(fq, qx, fk, kx, fv, fg, seq=seq, tq=FOX_TQ, tk=FOX_TK)

        x2 = _ffn_call(mix_g, mix_f, x2, w_out[l].astype(BF16), norm2_w[l].reshape(1, D_MODEL),
                       w_ffn_gate[l].astype(BF16), w_ffn_up[l].astype(BF16), w_ffn_down[l].astype(BF16),
                       final_norm_w.reshape(1, D_MODEL), tm=512, final=(l == depth - 1))
    return x2.reshape(batch, seq, D_MODEL)
```

```python
import functools

import numpy as np
import jax
import jax.numpy as jnp
from jax import lax
from jax.experimental import pallas as pl
from jax.experimental.pallas import tpu as pltpu

D_MODEL = 1024
HEADS = 8
HEAD_DIM = 64
WIDTH = HEADS * HEAD_DIM
CONV_K = 4
CHUNK = 64
D_FF = 2816
EPS = 1e-6

LANES = 128
MXU_DIM = 256
PACK = MXU_DIM // HEAD_DIM
N_SMALL = LANES
VMEM_LIMIT = 56 * 1024 * 1024

F32 = jnp.float32
BF16 = jnp.bfloat16
NEG_BIG = -1e30
LOG2E = 1.4426950408889634
EXT_STRIDE = 16


def _const_spec(shape):
    nd = len(shape)
    return pl.BlockSpec(shape, lambda *_: (0,) * nd)


def _sigmoid(x):
    return 1.0 / (1.0 + jnp.exp(-x))


def _softplus(x):
    return jnp.maximum(x, 0.0) + jnp.log1p(jnp.exp(-jnp.abs(x)))


def _head_sums(y2, ones_bd):
    yb = y2.astype(BF16)
    parts = [jnp.dot(yb[:, c * MXU_DIM:(c + 1) * MXU_DIM], ones_bd, preferred_element_type=F32)
             for c in range(WIDTH // MXU_DIM)]
    return jnp.concatenate(parts, axis=1)


def _seg_cumsum(v, seg):
    pos = lax.broadcasted_iota(jnp.int32, v.shape, 0) & (seg - 1)
    d = 1
    while d < seg:
        v = v + jnp.where(pos >= d, pltpu.roll(v, d, 0), 0.0)
        d *= 2
    return v


def _inproj_kernel(tiles_per_seq, x_ref, n1_ref, wg_ref, wf_ref, ws_ref, cw_ref, bias_ref, alog_ref, fqw_ref, fkw_ref,
                   ones_ref, sel_ref, one_ref,
                   gq_ref, gk_ref, gv_ref, gz_ref, fq_ref, fk_ref, fv_ref, fg_ref, sm_ref, kx_ref, qx_ref,
                   tail_ref, carry_ref):
    i = pl.program_id(0)
    tm = x_ref.shape[0]

    @pl.when(i % tiles_per_seq == 0)
    def _():
        tail_ref[...] = jnp.zeros_like(tail_ref)
        carry_ref[...] = jnp.zeros_like(carry_ref)

    x = x_ref[...]
    h = (x * lax.rsqrt(jnp.mean(x * x, axis=-1, keepdims=True) + EPS) * n1_ref[...]).astype(BF16)
    ones_bd = ones_ref[...]

    def proj(s):
        if s == 8:
            return jnp.dot(h, ws_ref[...], preferred_element_type=F32)
        w_ref = wg_ref if s < 4 else wf_ref
        return jnp.dot(h, w_ref[:, (s % 4) * WIDTH:(s % 4 + 1) * WIDTH], preferred_element_type=F32)

    for s, out_ref in enumerate((gq_ref, gk_ref, gv_ref)):
        y = proj(s)
        cw = cw_ref[:, s * WIDTH:(s + 1) * WIDTH]
        head = jnp.concatenate([tail_ref[s], y[:8]], axis=0)
        acc = y * cw[CONV_K - 1:CONV_K]
        acc_head = y[:8] * cw[CONV_K - 1:CONV_K]
        for k in range(1, CONV_K):
            tap = cw[CONV_K - 1 - k:CONV_K - k]
            acc = acc + pltpu.roll(y, k, 0) * tap
            acc_head = acc_head + pltpu.roll(head, k, 0)[8:] * tap
        tail_ref[s] = y[tm - 8:]
        act_head = acc_head * _sigmoid(acc_head)
        act = acc * _sigmoid(acc)
        if s < 2:
            act = act * lax.rsqrt(_head_sums(act * act, ones_bd) + EPS)
            act_head = act_head * lax.rsqrt(_head_sums(act_head * act_head, ones_bd) + EPS)
        out_ref[...] = act.astype(BF16)
        out_ref[0:8, :] = act_head.astype(BF16)

    z = proj(3)
    gz_ref[...] = (z * _sigmoid(z)).astype(BF16)

    for s, out_ref, w_norm in ((4, fq_ref, fqw_ref), (5, fk_ref, fkw_ref)):
        y = proj(s)
        ms = _head_sums(y * y, ones_bd) * (1.0 / HEAD_DIM)
        out_ref[...] = (y * lax.rsqrt(ms + EPS) * w_norm[...]).astype(BF16)

    fv_ref[...] = proj(6).astype(BF16)
    fg_ref[...] = _sigmoid(proj(7)).astype(BF16)

    t = proj(8) + bias_ref[...]
    lane = lax.broadcasted_iota(jnp.int32, t.shape, 1)
    beta = _sigmoid(t)
    g_log = -jnp.exp(alog_ref[...]) * _softplus(t)
    log_f = -_softplus(-t)
    g_cum = _seg_cumsum(g_log, CHUNK)
    f_cum = _seg_cumsum(log_f, tm) + carry_ref[0:1, :]
    carry_ref[0:1, :] = f_cum[tm - 1:tm, :]
    sm_ref[...] = jnp.where(lane < HEADS, beta, jnp.where(lane < 2 * HEADS, g_cum, f_cum))

    f2 = f_cum * LOG2E
    hi = f2.astype(BF16)
    r1 = f2 - hi.astype(F32)
    mid = r1.astype(BF16)
    lo = (r1 - mid.astype(F32)).astype(BF16)
    ext = jnp.dot(jnp.concatenate([hi, mid, lo], axis=1), sel_ref[...], preferred_element_type=F32) + one_ref[...]
    kx_ref[...] = ext[:, :LANES].astype(BF16)
    qx_ref[...] = ext[:, LANES:].astype(BF16)


def _bias_selectors():
    sel = np.zeros((3 * LANES, 2 * LANES), np.float32)
    one = np.zeros((1, 2 * LANES), np.float32)
    for h in range(HEADS):
        src = 2 * HEADS + h
        for part in range(3):
            sel[part * LANES + src, EXT_STRIDE * h + part] = -1.0
            sel[part * LANES + src, LANES + EXT_STRIDE * h + 3 + part] = 1.0
            one[0, EXT_STRIDE * h + 3 + part] = 1.0
            one[0, LANES + EXT_STRIDE * h + part] = 1.0
    return jnp.asarray(sel, BF16), jnp.asarray(one, F32)


def _inproj_call(x2, n1w, w_gdn, w_fox, w_gates, conv_w, bias_vec, alog_vec, fqw, fkw, ones_bd, *, seq, tm):
    m = x2.shape[0]
    assert seq % tm == 0 and tm % CHUNK == 0 and (tm & (tm - 1)) == 0
    sel, one = _bias_selectors()
    wide = pl.BlockSpec((tm, WIDTH), lambda i: (i, 0))
    narrow = pl.BlockSpec((tm, LANES), lambda i: (i, 0))
    out_shape = ([jax.ShapeDtypeStruct((m, WIDTH), BF16)] * 8 + [jax.ShapeDtypeStruct((m, N_SMALL), F32)]
                 + [jax.ShapeDtypeStruct((m, LANES), BF16)] * 2)
    return pl.pallas_call(
        functools.partial(_inproj_kernel, seq // tm),
        grid=(m // tm,),
        in_specs=[pl.BlockSpec((tm, D_MODEL), lambda i: (i, 0)),
                  _const_spec(n1w.shape), _const_spec(w_gdn.shape), _const_spec(w_fox.shape),
                  _const_spec(w_gates.shape), _const_spec(conv_w.shape),
                  _const_spec(bias_vec.shape), _const_spec(alog_vec.shape),
                  _const_spec(fqw.shape), _const_spec(fkw.shape), _const_spec(ones_bd.shape),
                  _const_spec(sel.shape), _const_spec(one.shape)],
        out_specs=[wide] * 8 + [narrow] * 3,
        out_shape=out_shape,
        scratch_shapes=[pltpu.VMEM((3, 8, WIDTH), F32), pltpu.VMEM((8, N_SMALL), F32)],
        compiler_params=pltpu.CompilerParams(dimension_semantics=("arbitrary",),
                                             vmem_limit_bytes=VMEM_LIMIT),
    )(x2, n1w, w_gdn, w_fox, w_gates, conv_w, bias_vec, alog_vec, fqw, fkw, ones_bd, sel, one)


def _lane_lo_mask():
    return lax.broadcasted_iota(jnp.int32, (CHUNK, LANES), 1) < HEAD_DIM


def _block_diag(x, lane_lo):
    xb = x.astype(BF16)
    zero = jnp.zeros((CHUNK, LANES), BF16)
    blocks = []
    for h in range(PACK):
        half = xb[:, (h // 2) * LANES:(h // 2 + 1) * LANES]
        keep = jnp.where(lane_lo if h % 2 == 0 else jnp.logical_not(lane_lo), half, zero)
        blocks.append(jnp.concatenate([keep, zero] if h < 2 else [zero, keep], axis=1))
    return jnp.concatenate(blocks, axis=0)


def _mm(lhs, rhs_bd):
    return jnp.dot(lhs.astype(BF16), rhs_bd, preferred_element_type=F32)


def _expand_heads(sm, first_lane):
    lane_lo = lax.broadcasted_iota(jnp.int32, (sm.shape[0], LANES), 1) < HEAD_DIM
    parts = []
    for p in range(HEADS // 2):
        c = first_lane + 2 * p
        even = jnp.broadcast_to(sm[:, c:c + 1], (sm.shape[0], LANES))
        odd = jnp.broadcast_to(sm[:, c + 1:c + 2], (sm.shape[0], LANES))
        parts.append(jnp.where(lane_lo, even, odd))
    return jnp.concatenate(parts, axis=1)


def _gdn_wy_kernel(q_ref, k_ref, v_ref, sm_ref, grow_ref,
                   u_ref, w_ref, a_ref, qg_ref, kd_ref, dl_ref):
    n_chunks = q_ref.shape[0] // CHUNK
    n_groups = WIDTH // MXU_DIM
    shape = (CHUNK, MXU_DIM)
    row = lax.broadcasted_iota(jnp.int32, shape, 0)
    col = lax.broadcasted_iota(jnp.int32, shape, 1) & (HEAD_DIM - 1)
    causal = row >= col
    strict = row > col
    eye = (row == col).astype(F32)
    bd_mask = _lane_lo_mask()
    scale = HEAD_DIM ** -0.5

    units = [(slice(c * CHUNK, (c + 1) * CHUNK), slice(g * MXU_DIM, (g + 1) * MXU_DIM), c)
             for c in range(n_chunks) for g in range(n_groups)]
    cs = range(len(units))
    bd = lambda xs: [_block_diag(x, bd_mask) for x in xs]
    mm = lambda ls, rs_: [_mm(a, b) for a, b in zip(ls, rs_)]
    stack = lambda xs, ys: [jnp.concatenate([x, y], axis=0) for x, y in zip(xs, ys)]

    sm = sm_ref[...]
    bx_all = _expand_heads(sm, 0)
    gx_all = _expand_heads(sm, HEADS)
    q = [q_ref[r, l].astype(F32) for r, l, _ in units]
    k = [k_ref[r, l].astype(F32) for r, l, _ in units]
    bx = [bx_all[r, l] for r, l, _ in units]
    gx = [gx_all[r, l] for r, l, _ in units]
    kb = [k[c] * bx[c] for c in cs]
    decay = [jnp.exp(jnp.where(causal, gx[c] - grow_ref[ch, :, l], NEG_BIG))
             for c, (_, l, ch) in enumerate(units)]

    bd_k = bd(k)
    s1 = [lax.dot_general(jnp.concatenate([kb[c], q[c]], axis=0).astype(BF16), bd_k[c],
                          (((1,), (1,)), ((), ())), preferred_element_type=F32) for c in cs]
    lmat = [jnp.where(strict, s1[c][:CHUNK] * decay[c], 0.0) for c in cs]
    for c, (r, l, _) in enumerate(units):
        a_ref[r, l] = (s1[c][CHUNK:] * decay[c] * scale).astype(BF16)

    blk8 = (row >> 3) == (col >> 3)
    n8 = [jnp.where(blk8, -l, 0.0) for l in lmat]
    t0 = [eye + n for n in n8]
    p1 = mm(n8, bd(n8))
    r = mm(stack(p1, t0), bd(p1))
    ta = [t0[c] + r[c][CHUNK:] for c in cs]
    tinv = [ta[c] + z for c, z in enumerate(mm(ta, bd([x[:CHUNK] for x in r])))]
    for ls in (3, 4, 5):
        off = ((row >> (ls + 1)) == (col >> (ls + 1))) & ((row >> ls) == (col >> ls) + 1)
        y = mm([jnp.where(off, l, 0.0) for l in lmat], bd(tinv))
        tinv = [tinv[c] - z for c, z in enumerate(mm(tinv, bd(y)))]

    eg = [jnp.exp(g) for g in gx]
    u = mm(tinv, bd([v_ref[r, l].astype(F32) * bx[c] for c, (r, l, _) in enumerate(units)]))
    w = mm(tinv, bd([kb[c] * eg[c] for c in cs]))
    for c, (r, l, ch) in enumerate(units):
        u_ref[r, l] = u[c].astype(BF16)
        w_ref[r, l] = w[c].astype(BF16)
        qg_ref[r, l] = (q[c] * eg[c] * scale).astype(BF16)
        g_last = gx[c][CHUNK - 1:CHUNK, :]
        kd_ref[r, l] = (k[c] * jnp.exp(g_last - gx[c])).astype(BF16)
        dl_ref[ch, :, l] = jnp.exp(g_last)


def _gdn_wy_call(gq, gk, gv, small, grow, *, rows):
    m = gq.shape[0]
    cpb = rows // CHUNK
    blk = pl.BlockSpec((rows, WIDTH), lambda i: (i, 0))
    rowblk = pl.BlockSpec((cpb, 1, WIDTH), lambda i: (i, 0, 0))
    bf = jax.ShapeDtypeStruct((m, WIDTH), BF16)
    return pl.pallas_call(
        _gdn_wy_kernel,
        grid=(m // rows,),
        in_specs=[blk] * 3 + [pl.BlockSpec((rows, N_SMALL), lambda i: (i, 0)), rowblk],
        out_specs=[blk] * 5 + [rowblk],
        out_shape=[bf] * 5 + [jax.ShapeDtypeStruct((m // CHUNK, 1, WIDTH), F32)],
        compiler_params=pltpu.CompilerParams(dimension_semantics=("parallel",),
                                             vmem_limit_bytes=VMEM_LIMIT),
    )(gq, gk, gv, small, grow)


def _gdn_scan_kernel(u_ref, w_ref, a_ref, qg_ref, kd_ref, dl_ref, gz_ref, nw_ref, ones_ref,
                     o_ref, s_ref, oacc_ref):
    n_seq, rows, _ = u_ref.shape
    n_groups = WIDTH // MXU_DIM
    chains = [(b, g, slice(g * MXU_DIM, (g + 1) * MXU_DIM)) for b in range(n_seq) for g in range(n_groups)]
    r2 = lax.broadcasted_iota(jnp.int32, (MXU_DIM, MXU_DIM), 0)
    c2 = lax.broadcasted_iota(jnp.int32, (MXU_DIM, MXU_DIM), 1)
    bd_mask = (r2 // HEAD_DIM) == (c2 // HEAD_DIM)
    lane_lo = _lane_lo_mask()

    @pl.when(pl.program_id(1) == 0)
    def _():
        s_ref[...] = jnp.zeros_like(s_ref)

    def body(c, carry):
        rs = pl.ds(pl.multiple_of(c * CHUNK, CHUNK), CHUNK)
        state = [s_ref[b, g] for b, g, _ in chains]
        r = [jnp.dot(jnp.concatenate([w_ref[b, rs, l], qg_ref[b, rs, l]], axis=0),
                     state[i].astype(BF16), preferred_element_type=F32) for i, (b, _, l) in enumerate(chains)]
        v_new = [u_ref[b, rs, l].astype(F32) - r[i][:CHUNK] for i, (b, _, l) in enumerate(chains)]
        upd = [lax.dot_general(kd_ref[b, rs, l], v_new[i].astype(BF16), (((0,), (0,)), ((), ())),
                               preferred_element_type=F32) for i, (b, _, l) in enumerate(chains)]
        for i, (b, g, l) in enumerate(chains):
            s_ref[b, g] = state[i] * dl_ref[b, c, :, l] + jnp.where(bd_mask, upd[i], 0.0)
        for i, (b, _, l) in enumerate(chains):
            oacc_ref[b, rs, l] = r[i][CHUNK:] + jnp.dot(a_ref[b, rs, l], _block_diag(v_new[i], lane_lo),
                                                        preferred_element_type=F32)
        return carry

    lax.fori_loop(0, rows // CHUNK, body, 0)

    for b in range(n_seq):
        o = oacc_ref[b]
        ms = _head_sums(o * o, ones_ref[...]) * (1.0 / HEAD_DIM)
        o_ref[b] = (o * lax.rsqrt(ms + EPS) * nw_ref[...] * gz_ref[b].astype(F32)).astype(BF16)


def _gdn_scan_call(u, w, a, qg, kd, dl, gz, nw, ones_bd, *, seq, n_seq, rows):
    batch = u.shape[0] // seq
    as3d = lambda t: t.reshape(batch, seq, WIDTH)
    blk = pl.BlockSpec((n_seq, rows, WIDTH), lambda b, t: (b, t, 0))
    dl4 = dl.reshape(batch, seq // CHUNK, 1, WIDTH)
    out = pl.pallas_call(
        _gdn_scan_kernel,
        grid=(batch // n_seq, seq // rows),
        in_specs=[blk] * 5 + [pl.BlockSpec((n_seq, rows // CHUNK, 1, WIDTH), lambda b, t: (b, t, 0, 0)), blk,
                              _const_spec(nw.shape), _const_spec(ones_bd.shape)],
        out_specs=blk,
        out_shape=jax.ShapeDtypeStruct((batch, seq, WIDTH), BF16),
        scratch_shapes=[pltpu.VMEM((n_seq, WIDTH // MXU_DIM, MXU_DIM, MXU_DIM), F32),
                        pltpu.VMEM((n_seq, rows, WIDTH), F32)],
        compiler_params=pltpu.CompilerParams(dimension_semantics=("parallel", "arbitrary"),
                                             vmem_limit_bytes=VMEM_LIMIT),
    )(as3d(u), as3d(w), as3d(a), as3d(qg), as3d(kd), dl4, as3d(gz), nw, ones_bd)
    return out.reshape(batch * seq, WIDTH)


FOX_TQ = 256
FOX_TK = 256
FOX_SPLIT = 1
VT_ROWS = HEAD_DIM + 16


def _fox_kernel(tk, q_ref, qx_ref, k_ref, kx_ref, v_ref, gate_ref, o_ref, vt_ref, kp_ref):
    qi = pl.program_id(1)
    tq = q_ref.shape[0]
    sub = tq // tk
    heads = range(HEADS)

    @pl.when(qi == 0)
    def _():
        v_t = v_ref[...].T
        for h in heads:
            vt_ref[h * VT_ROWS:h * VT_ROWS + HEAD_DIM] = v_t[h * HEAD_DIM:(h + 1) * HEAD_DIM]
            vt_ref[h * VT_ROWS + HEAD_DIM:(h + 1) * VT_ROWS] = jnp.ones((VT_ROWS - HEAD_DIM, v_t.shape[1]), BF16)
        lane = lax.broadcasted_iota(jnp.int32, (k_ref.shape[0], LANES), 1)
        kx = kx_ref[...].astype(F32)
        for p in range(HEADS // 2):
            pair = k_ref[:, p * LANES:(p + 1) * LANES].astype(F32)
            for h in (2 * p, 2 * p + 1):
                own = pair if h % 2 == 0 else pltpu.roll(pair, HEAD_DIM, 1)
                ext = pltpu.roll(kx, (HEAD_DIM - EXT_STRIDE * h) % LANES, 1)
                blk = jnp.where(lane < HEAD_DIM, own, jnp.where(lane < HEAD_DIM + EXT_STRIDE, ext, 0.0))
                kp_ref[:, h * LANES:(h + 1) * LANES] = blk.astype(BF16)

    qx_t = qx_ref[...].T
    q_t = []
    for h in heads:
        pair_t = q_ref[:, (h // 2) * LANES:(h // 2 + 1) * LANES].T
        q_t.append(jnp.concatenate(
            [pair_t[(h % 2) * HEAD_DIM:(h % 2 + 1) * HEAD_DIM], qx_t[EXT_STRIDE * h:EXT_STRIDE * (h + 1)],
             jnp.zeros((LANES - HEAD_DIM - EXT_STRIDE, tq), BF16)], axis=0))
    kv_pos = lax.broadcasted_iota(jnp.int32, (tk // FOX_SPLIT, tq), 0)
    q_pos = lax.broadcasted_iota(jnp.int32, (tk // FOX_SPLIT, tq), 1)

    def scores(k0, rows, diag_offset):
        s = []
        for h in heads:
            s_h = jnp.dot(kp_ref[pl.ds(k0, rows), h * LANES:(h + 1) * LANES], q_t[h],
                          preferred_element_type=F32)
            if diag_offset is not None:
                s_h = jnp.where(kv_pos + diag_offset <= q_pos, s_h, NEG_BIG)
            s.append(s_h)
        return s

    def absorb(k0, rows, s, carry):
        m_new = [jnp.maximum(carry[h][0], jnp.max(s[h], axis=0, keepdims=True)) for h in heads]
        p = [jnp.exp2(s[h] - m_new[h]).astype(BF16) for h in heads]
        alpha = [jnp.exp2(carry[h][0] - m_new[h]) for h in heads]
        pv = [jnp.dot(vt_ref[h * VT_ROWS:(h + 1) * VT_ROWS, pl.ds(k0, rows)], p[h],
                      preferred_element_type=F32) for h in heads]
        return tuple((m_new[h], alpha[h] * carry[h][1] + pv[h]) for h in heads)

    def tile(j, carry, diag_offset):
        half = tk // FOX_SPLIT
        starts = [pl.multiple_of(j * tk + i * half, half) for i in range(FOX_SPLIT)]
        s = [scores(starts[i], half, None if diag_offset is None else diag_offset + i * half)
             for i in range(FOX_SPLIT)]
        for i in range(FOX_SPLIT):
            carry = absorb(starts[i], half, s[i], carry)
        return carry

    init = tuple((jnp.full((1, tq), NEG_BIG, F32), jnp.zeros((VT_ROWS, tq), F32)) for _ in heads)
    carry = lax.fori_loop(0, qi * sub, lambda j, c: tile(j, c, None), init)
    for d in range(sub):
        carry = tile(qi * sub + d, carry, d * tk)
    o_t = jnp.concatenate([acc[:HEAD_DIM] / acc[HEAD_DIM:HEAD_DIM + 1] for _, acc in carry], axis=0)
    o_ref[...] = (o_t.T * gate_ref[...].astype(F32)).astype(BF16)


def _fox_call(fq, qx, fk, kx, fv, gate, *, seq, tq, tk):
    m = fq.shape[0]
    nq = seq // tq
    qblk = pl.BlockSpec((tq, WIDTH), lambda b, i: (b * nq + i, 0))
    kvblk = pl.BlockSpec((seq, WIDTH), lambda b, i: (b, 0))
    return pl.pallas_call(
        functools.partial(_fox_kernel, tk),
        grid=(m // seq, nq),
        in_specs=[qblk, pl.BlockSpec((tq, LANES), lambda b, i: (b * nq + i, 0)),
                  kvblk, pl.BlockSpec((seq, LANES), lambda b, i: (b, 0)), kvblk, qblk],
        out_specs=qblk,
        out_shape=jax.ShapeDtypeStruct((m, WIDTH), BF16),
        scratch_shapes=[pltpu.VMEM((HEADS * VT_ROWS, seq), BF16), pltpu.VMEM((seq, HEADS * LANES), BF16)],
        compiler_params=pltpu.CompilerParams(dimension_semantics=("parallel", "arbitrary"),
                                             vmem_limit_bytes=VMEM_LIMIT),
    )(fq, qx, fk, kx, fv, gate)


FF_SPLITS = ((0, 1024), (1024, 1024), (2048, 768))


def _ffn_kernel(final, mg_ref, mf_ref, x_ref, wo_ref, n2_ref, wg_ref, wu_ref, wd_ref, fn_ref, o_ref):
    x1 = (x_ref[...]
          + jnp.dot(mg_ref[...], wo_ref[0:WIDTH, :], preferred_element_type=F32)
          + jnp.dot(mf_ref[...], wo_ref[WIDTH:2 * WIDTH, :], preferred_element_type=F32))
    h = (x1 * lax.rsqrt(jnp.mean(x1 * x1, axis=-1, keepdims=True) + EPS) * n2_ref[...]).astype(BF16)
    ffn = None
    for start, size in FF_SPLITS:
        gate = jnp.dot(h, wg_ref[:, start:start + size], preferred_element_type=F32)
        up = jnp.dot(h, wu_ref[:, start:start + size], preferred_element_type=F32)
        act = (gate * _sigmoid(gate) * up).astype(BF16)
        down = jnp.dot(act, wd_ref[start:start + size, :], preferred_element_type=F32)
        ffn = down if ffn is None else ffn + down
    y = x1 + ffn
    if final:
        y = y * lax.rsqrt(jnp.mean(y * y, axis=-1, keepdims=True) + EPS) * fn_ref[...]
    o_ref[...] = y


def _ffn_call(mix_g, mix_f, x2, wo, n2w, wg, wu, wd, fnw, *, tm, final):
    m = x2.shape[0]
    half = pl.BlockSpec((tm, WIDTH), lambda i: (i, 0))
    full = pl.BlockSpec((tm, D_MODEL), lambda i: (i, 0))

    def resident(shape):
        return pl.BlockSpec(shape, lambda i: (0, 0), pipeline_mode=pl.Buffered(1))

    return pl.pallas_call(
        functools.partial(_ffn_kernel, final),
        grid=(m // tm,),
        in_specs=[half, half, full, resident(wo.shape), resident(n2w.shape), resident(wg.shape),
                  resident(wu.shape), resident(wd.shape), resident(fnw.shape)],
        out_specs=full,
        out_shape=jax.ShapeDtypeStruct((m, D_MODEL), F32),
        compiler_params=pltpu.CompilerParams(dimension_semantics=("parallel",),
                                             vmem_limit_bytes=VMEM_LIMIT),
    )(mix_g, mix_f, x2, wo, n2w, wg, wu, wd, fnw)


def _lane_vec(parts):
    v = jnp.concatenate([p.astype(F32).reshape(-1) for p in parts])
    return jnp.pad(v, (0, N_SMALL - v.shape[0])).reshape(1, N_SMALL)


def kernel(x, norm1_w, w_in, gdn_conv_w, gdn_A_log, gdn_dt_bias, gdn_out_norm_w, fox_f_bias, fox_q_norm_w, fox_k_norm_w, w_out, norm2_w, w_ffn_gate, w_ffn_up, w_ffn_down, final_norm_w):
    batch, seq, _ = x.shape
    m = batch * seq
    depth = norm1_w.shape[0]
    zeros8 = jnp.zeros((HEADS,), F32)
    r2 = jnp.arange(MXU_DIM) // HEAD_DIM
    ones_bd = (r2[:, None] == r2[None, :]).astype(BF16)

    x2 = x.reshape(m, D_MODEL)
    for l in range(depth):
        w = w_in[l]
        big = 4 * WIDTH
        g_small = w[:, big:big + 2 * HEADS]
        f_start = big + 2 * HEADS
        f_small = w[:, f_start + big:f_start + big + HEADS]
        w_gdn = w[:, :big].astype(BF16)
        w_fox = w[:, f_start:f_start + big].astype(BF16)
        w_gates = jnp.concatenate(
            [g_small, f_small, jnp.zeros((D_MODEL, N_SMALL - 3 * HEADS), w.dtype)], axis=1).astype(BF16)
        bias_vec = _lane_vec([zeros8, gdn_dt_bias[l], fox_f_bias[l]])
        alog_vec = _lane_vec([zeros8, gdn_A_log[l], zeros8])
        fqw = (jnp.tile(fox_q_norm_w[l].astype(F32), HEADS) * (HEAD_DIM ** -0.5 * LOG2E)).reshape(1, WIDTH)
        fkw = jnp.tile(fox_k_norm_w[l].astype(F32), HEADS).reshape(1, WIDTH)

        gq, gk, gv, gz, fq, fk, fv, fg, small, kx, qx = _inproj_call(
            x2, norm1_w[l].reshape(1, D_MODEL), w_gdn, w_fox, w_gates, gdn_conv_w[l].astype(F32), bias_vec, alog_vec,
            fqw, fkw, ones_bd, seq=seq, tm=512)

        g_cum = small[:, HEADS:2 * HEADS]
        grow = g_cum.reshape(m // CHUNK, CHUNK, HEADS).transpose(0, 2, 1).reshape(m // CHUNK, 1, WIDTH)

        u, wmat, amat, qg, kd, dl = _gdn_wy_call(gq, gk, gv, small, grow, rows=512)
        nw = jnp.tile(gdn_out_norm_w[l].astype(F32), HEADS).reshape(1, WIDTH)
        mix_g = _gdn_scan_call(u, wmat, amat, qg, kd, dl, gz, nw, ones_bd, seq=seq, n_seq=2, rows=512)
        mix_f = _fox_call(fq, qx, fk, kx, fv, fg, seq=seq, tq=FOX_TQ, tk=FOX_TK)

        x2 = _ffn_call(mix_g, mix_f, x2, w_out[l].astype(BF16), norm2_w[l].reshape(1, D_MODEL),
                       w_ffn_gate[l].astype(BF16), w_ffn_up[l].astype(BF16), w_ffn_down[l].astype(BF16),
                       final_norm_w.reshape(1, D_MODEL), tm=512, final=(l == depth - 1))
    return x2.reshape(batch, seq, D_MODEL)
```

```python
import functools

import numpy as np
import jax
import jax.numpy as jnp
from jax import lax
from jax.experimental import pallas as pl
from jax.experimental.pallas import tpu as pltpu

D_MODEL = 1024
HEADS = 8
HEAD_DIM = 64
WIDTH = HEADS * HEAD_DIM
CONV_K = 4
CHUNK = 64
D_FF = 2816
EPS = 1e-6

LANES = 128
MXU_DIM = 256
PACK = MXU_DIM // HEAD_DIM
N_SMALL = LANES
VMEM_LIMIT = 56 * 1024 * 1024

F32 = jnp.float32
BF16 = jnp.bfloat16
NEG_BIG = -1e30
LOG2E = 1.4426950408889634
EXT_STRIDE = 16


def _const_spec(shape):
    nd = len(shape)
    return pl.BlockSpec(shape, lambda *_: (0,) * nd)


def _sigmoid(x):
    return 1.0 / (1.0 + jnp.exp(-x))


def _softplus(x):
    return jnp.maximum(x, 0.0) + jnp.log1p(jnp.exp(-jnp.abs(x)))


def _head_sums(y2, ones_bd):
    yb = y2.astype(BF16)
    parts = [jnp.dot(yb[:, c * MXU_DIM:(c + 1) * MXU_DIM], ones_bd, preferred_element_type=F32)
             for c in range(WIDTH // MXU_DIM)]
    return jnp.concatenate(parts, axis=1)


def _seg_cumsum(v, seg):
    pos = lax.broadcasted_iota(jnp.int32, v.shape, 0) & (seg - 1)
    d = 1
    while d < seg:
        v = v + jnp.where(pos >= d, pltpu.roll(v, d, 0), 0.0)
        d *= 2
    return v


def _inproj_kernel(tiles_per_seq, x_ref, n1_ref, wg_ref, wf_ref, ws_ref, cw_ref, bias_ref, alog_ref, fqw_ref, fkw_ref,
                   ones_ref, sel_ref, one_ref,
                   gq_ref, gk_ref, gv_ref, gz_ref, fq_ref, fk_ref, fv_ref, fg_ref, sm_ref, kx_ref, qx_ref,
                   tail_ref, carry_ref):
    i = pl.program_id(0)
    tm = x_ref.shape[0]

    @pl.when(i % tiles_per_seq == 0)
    def _():
        tail_ref[:, 0:8, :] = jnp.zeros((3, 8, WIDTH), F32)
        carry_ref[...] = jnp.zeros_like(carry_ref)

    x = x_ref[...]
    h = (x * lax.rsqrt(jnp.mean(x * x, axis=-1, keepdims=True) + EPS) * n1_ref[...]).astype(BF16)
    ones_bd = ones_ref[...]

    def proj(s):
        if s == 8:
            return jnp.dot(h, ws_ref[...], preferred_element_type=F32)
        w_ref = wg_ref if s < 4 else wf_ref
        return jnp.dot(h, w_ref[:, (s % 4) * WIDTH:(s % 4 + 1) * WIDTH], preferred_element_type=F32)

    ys = [proj(s) for s in range(9)]

    for s, out_ref in enumerate((gq_ref, gk_ref, gv_ref)):
        y = ys[s]
        cw = cw_ref[:, s * WIDTH:(s + 1) * WIDTH]
        tail_ref[s, 8:8 + tm] = y
        acc = y * cw[CONV_K - 1:CONV_K]
        for k in range(1, CONV_K):
            acc = acc + tail_ref[s, 8 - k:8 - k + tm] * cw[CONV_K - 1 - k:CONV_K - k]
        tail_ref[s, 0:8] = y[tm - 8:]
        act = acc * _sigmoid(acc)
        if s < 2:
            act = act * lax.rsqrt(_head_sums(act * act, ones_bd) + EPS)
        out_ref[...] = act.astype(BF16)

    z = ys[3]
    gz_ref[...] = (z * _sigmoid(z)).astype(BF16)

    for s, out_ref, w_norm in ((4, fq_ref, fqw_ref), (5, fk_ref, fkw_ref)):
        y = ys[s]
        ms = _head_sums(y * y, ones_bd) * (1.0 / HEAD_DIM)
        out_ref[...] = (y * lax.rsqrt(ms + EPS) * w_norm[...]).astype(BF16)

    fv_ref[...] = ys[6].astype(BF16)
    fg_ref[...] = _sigmoid(ys[7]).astype(BF16)

    t = ys[8] + bias_ref[...]
    lane = lax.broadcasted_iota(jnp.int32, t.shape, 1)
    beta = _sigmoid(t)
    g_log = -jnp.exp(alog_ref[...]) * _softplus(t)
    log_f = -_softplus(-t)
    g_cum = _seg_cumsum(g_log, CHUNK)
    f_cum = _seg_cumsum(log_f, tm) + carry_ref[0:1, :]
    carry_ref[0:1, :] = f_cum[tm - 1:tm, :]
    sm_ref[...] = jnp.where(lane < HEADS, beta, jnp.where(lane < 2 * HEADS, g_cum, f_cum))

    f2 = f_cum * LOG2E
    hi = f2.astype(BF16)
    r1 = f2 - hi.astype(F32)
    mid = r1.astype(BF16)
    lo = (r1 - mid.astype(F32)).astype(BF16)
    ext = jnp.dot(jnp.concatenate([hi, mid, lo], axis=1), sel_ref[...], preferred_element_type=F32) + one_ref[...]
    kx_ref[...] = ext[:, :LANES].astype(BF16)
    qx_ref[...] = ext[:, LANES:].astype(BF16)


def _bias_selectors():
    sel = np.zeros((3 * LANES, 2 * LANES), np.float32)
    one = np.zeros((1, 2 * LANES), np.float32)
    for h in range(HEADS):
        src = 2 * HEADS + h
        for part in range(3):
            sel[part * LANES + src, EXT_STRIDE * h + part] = -1.0
            sel[part * LANES + src, LANES + EXT_STRIDE * h + 3 + part] = 1.0
            one[0, EXT_STRIDE * h + 3 + part] = 1.0
            one[0, LANES + EXT_STRIDE * h + part] = 1.0
    return jnp.asarray(sel, BF16), jnp.asarray(one, F32)


def _inproj_call(x2, n1w, w_gdn, w_fox, w_gates, conv_w, bias_vec, alog_vec, fqw, fkw, ones_bd, *, seq, tm):
    m = x2.shape[0]
    assert seq % tm == 0 and tm % CHUNK == 0 and (tm & (tm - 1)) == 0
    sel, one = _bias_selectors()
    wide = pl.BlockSpec((tm, WIDTH), lambda i: (i, 0))
    narrow = pl.BlockSpec((tm, LANES), lambda i: (i, 0))
    out_shape = ([jax.ShapeDtypeStruct((m, WIDTH), BF16)] * 8 + [jax.ShapeDtypeStruct((m, N_SMALL), F32)]
                 + [jax.ShapeDtypeStruct((m, LANES), BF16)] * 2)
    return pl.pallas_call(
        functools.partial(_inproj_kernel, seq // tm),
        grid=(m // tm,),
        in_specs=[pl.BlockSpec((tm, D_MODEL), lambda i: (i, 0)),
                  _const_spec(n1w.shape), _const_spec(w_gdn.shape), _const_spec(w_fox.shape),
                  _const_spec(w_gates.shape), _const_spec(conv_w.shape),
                  _const_spec(bias_vec.shape), _const_spec(alog_vec.shape),
                  _const_spec(fqw.shape), _const_spec(fkw.shape), _const_spec(ones_bd.shape),
                  _const_spec(sel.shape), _const_spec(one.shape)],
        out_specs=[wide] * 8 + [narrow] * 3,
        out_shape=out_shape,
        scratch_shapes=[pltpu.VMEM((3, tm + 8, WIDTH), F32), pltpu.VMEM((8, N_SMALL), F32)],
        compiler_params=pltpu.CompilerParams(dimension_semantics=("arbitrary",),
                                             vmem_limit_bytes=VMEM_LIMIT),
    )(x2, n1w, w_gdn, w_fox, w_gates, conv_w, bias_vec, alog_vec, fqw, fkw, ones_bd, sel, one)


def _lane_lo_mask():
    return lax.broadcasted_iota(jnp.int32, (CHUNK, LANES), 1) < HEAD_DIM


def _block_diag(x, lane_lo):
    xb = x.astype(BF16)
    zero = jnp.zeros((CHUNK, LANES), BF16)
    blocks = []
    for h in range(PACK):
        half = xb[:, (h // 2) * LANES:(h // 2 + 1) * LANES]
        keep = jnp.where(lane_lo if h % 2 == 0 else jnp.logical_not(lane_lo), half, zero)
        blocks.append(jnp.concatenate([keep, zero] if h < 2 else [zero, keep], axis=1))
    return jnp.concatenate(blocks, axis=0)


def _mm(lhs, rhs_bd):
    return jnp.dot(lhs.astype(BF16), rhs_bd, preferred_element_type=F32)


def _expand_heads(sm, first_lane):
    lane_lo = lax.broadcasted_iota(jnp.int32, (sm.shape[0], LANES), 1) < HEAD_DIM
    parts = []
    for p in range(HEADS // 2):
        c = first_lane + 2 * p
        even = jnp.broadcast_to(sm[:, c:c + 1], (sm.shape[0], LANES))
        odd = jnp.broadcast_to(sm[:, c + 1:c + 2], (sm.shape[0], LANES))
        parts.append(jnp.where(lane_lo, even, odd))
    return jnp.concatenate(parts, axis=1)


def _gdn_wy_kernel(q_ref, k_ref, v_ref, sm_ref, grow_ref,
                   u_ref, w_ref, a_ref, qg_ref, kd_ref, dl_ref):
    n_chunks = q_ref.shape[0] // CHUNK
    n_groups = WIDTH // MXU_DIM
    shape = (CHUNK, MXU_DIM)
    row = lax.broadcasted_iota(jnp.int32, shape, 0)
    col = lax.broadcasted_iota(jnp.int32, shape, 1) & (HEAD_DIM - 1)
    causal = row >= col
    strict = row > col
    eye = (row == col).astype(F32)
    bd_mask = _lane_lo_mask()
    scale = HEAD_DIM ** -0.5

    units = [(slice(c * CHUNK, (c + 1) * CHUNK), slice(g * MXU_DIM, (g + 1) * MXU_DIM), c)
             for c in range(n_chunks) for g in range(n_groups)]
    cs = range(len(units))
    bd = lambda xs: [_block_diag(x, bd_mask) for x in xs]
    mm = lambda ls, rs_: [_mm(a, b) for a, b in zip(ls, rs_)]
    stack = lambda xs, ys: [jnp.concatenate([x, y], axis=0) for x, y in zip(xs, ys)]

    sm = sm_ref[...]
    bx_all = _expand_heads(sm, 0)
    gx_all = _expand_heads(sm, HEADS)
    q = [q_ref[r, l].astype(F32) for r, l, _ in units]
    k = [k_ref[r, l].astype(F32) for r, l, _ in units]
    bx = [bx_all[r, l] for r, l, _ in units]
    gx = [gx_all[r, l] for r, l, _ in units]
    kb = [k[c] * bx[c] for c in cs]
    decay = [jnp.exp(jnp.where(causal, gx[c] - grow_ref[ch, :, l], NEG_BIG))
             for c, (_, l, ch) in enumerate(units)]

    bd_k = bd(k)
    s1 = [lax.dot_general(jnp.concatenate([kb[c], q[c]], axis=0).astype(BF16), bd_k[c],
                          (((1,), (1,)), ((), ())), preferred_element_type=F32) for c in cs]
    lmat = [jnp.where(strict, s1[c][:CHUNK] * decay[c], 0.0) for c in cs]
    for c, (r, l, _) in enumerate(units):
        a_ref[r, l] = (s1[c][CHUNK:] * decay[c] * scale).astype(BF16)

    blk8 = (row >> 3) == (col >> 3)
    n8 = [jnp.where(blk8, -l, 0.0) for l in lmat]
    t0 = [eye + n for n in n8]
    p1 = mm(n8, bd(n8))
    r = mm(stack(p1, t0), bd(p1))
    ta = [t0[c] + r[c][CHUNK:] for c in cs]
    tinv = [ta[c] + z for c, z in enumerate(mm(ta, bd([x[:CHUNK] for x in r])))]
    for ls in (3, 4, 5):
        off = ((row >> (ls + 1)) == (col >> (ls + 1))) & ((row >> ls) == (col >> ls) + 1)
        y = mm([jnp.where(off, l, 0.0) for l in lmat], bd(tinv))
        tinv = [tinv[c] - z for c, z in enumerate(mm(tinv, bd(y)))]

    eg = [jnp.exp(g) for g in gx]
    u = mm(tinv, bd([v_ref[r, l].astype(F32) * bx[c] for c, (r, l, _) in enumerate(units)]))
    w = mm(tinv, bd([kb[c] * eg[c] for c in cs]))
    for c, (r, l, ch) in enumerate(units):
        u_ref[r, l] = u[c].astype(BF16)
        w_ref[r, l] = w[c].astype(BF16)
        qg_ref[r, l] = (q[c] * eg[c] * scale).astype(BF16)
        g_last = gx[c][CHUNK - 1:CHUNK, :]
        kd_ref[r, l] = (k[c] * jnp.exp(g_last - gx[c])).astype(BF16)
        dl_ref[ch, :, l] = jnp.exp(g_last)


def _gdn_wy_call(gq, gk, gv, small, grow, *, rows):
    m = gq.shape[0]
    cpb = rows // CHUNK
    blk = pl.BlockSpec((rows, WIDTH), lambda i: (i, 0))
    rowblk = pl.BlockSpec((cpb, 1, WIDTH), lambda i: (i, 0, 0))
    bf = jax.ShapeDtypeStruct((m, WIDTH), BF16)
    return pl.pallas_call(
        _gdn_wy_kernel,
        grid=(m // rows,),
        in_specs=[blk] * 3 + [pl.BlockSpec((rows, N_SMALL), lambda i: (i, 0)), rowblk],
        out_specs=[blk] * 5 + [rowblk],
        out_shape=[bf] * 5 + [jax.ShapeDtypeStruct((m // CHUNK, 1, WIDTH), F32)],
        compiler_params=pltpu.CompilerParams(dimension_semantics=("parallel",),
                                             vmem_limit_bytes=VMEM_LIMIT),
    )(gq, gk, gv, small, grow)


def _gdn_scan_kernel(u_ref, w_ref, a_ref, qg_ref, kd_ref, dl_ref, gz_ref, nw_ref, ones_ref,
                     o_ref, s_ref, oacc_ref):
    n_seq, rows, _ = u_ref.shape
    n_groups = WIDTH // MXU_DIM
    chains = [(b, g, slice(g * MXU_DIM, (g + 1) * MXU_DIM)) for b in range(n_seq) for g in range(n_groups)]
    r2 = lax.broadcasted_iota(jnp.int32, (MXU_DIM, MXU_DIM), 0)
    c2 = lax.broadcasted_iota(jnp.int32, (MXU_DIM, MXU_DIM), 1)
    bd_mask = (r2 // HEAD_DIM) == (c2 // HEAD_DIM)
    lane_lo = _lane_lo_mask()

    @pl.when(pl.program_id(1) == 0)
    def _():
        s_ref[...] = jnp.zeros_like(s_ref)

    def body(c, carry):
        rs = pl.ds(pl.multiple_of(c * CHUNK, CHUNK), CHUNK)
        state = [s_ref[b, g] for b, g, _ in chains]
        r = [jnp.dot(jnp.concatenate([w_ref[b, rs, l], qg_ref[b, rs, l]], axis=0),
                     state[i].astype(BF16), preferred_element_type=F32) for i, (b, _, l) in enumerate(chains)]
        v_new = [u_ref[b, rs, l].astype(F32) - r[i][:CHUNK] for i, (b, _, l) in enumerate(chains)]
        upd = [lax.dot_general(kd_ref[b, rs, l], v_new[i].astype(BF16), (((0,), (0,)), ((), ())),
                               preferred_element_type=F32) for i, (b, _, l) in enumerate(chains)]
        for i, (b, g, l) in enumerate(chains):
            s_ref[b, g] = state[i] * dl_ref[b, c, :, l] + jnp.where(bd_mask, upd[i], 0.0)
        for i, (b, _, l) in enumerate(chains):
            oacc_ref[b, rs, l] = r[i][CHUNK:] + jnp.dot(a_ref[b, rs, l], _block_diag(v_new[i], lane_lo),
                                                        preferred_element_type=F32)
        return carry

    lax.fori_loop(0, rows // CHUNK, body, 0)

    for b in range(n_seq):
        o = oacc_ref[b]
        ms = _head_sums(o * o, ones_ref[...]) * (1.0 / HEAD_DIM)
        o_ref[b] = (o * lax.rsqrt(ms + EPS) * nw_ref[...] * gz_ref[b].astype(F32)).astype(BF16)


def _gdn_scan_call(u, w, a, qg, kd, dl, gz, nw, ones_bd, *, seq, n_seq, rows):
    batch = u.shape[0] // seq
    as3d = lambda t: t.reshape(batch, seq, WIDTH)
    blk = pl.BlockSpec((n_seq, rows, WIDTH), lambda b, t: (b, t, 0))
    dl4 = dl.reshape(batch, seq // CHUNK, 1, WIDTH)
    out = pl.pallas_call(
        _gdn_scan_kernel,
        grid=(batch // n_seq, seq // rows),
        in_specs=[blk] * 5 + [pl.BlockSpec((n_seq, rows // CHUNK, 1, WIDTH), lambda b, t: (b, t, 0, 0)), blk,
                              _const_spec(nw.shape), _const_spec(ones_bd.shape)],
        out_specs=blk,
        out_shape=jax.ShapeDtypeStruct((batch, seq, WIDTH), BF16),
        scratch_shapes=[pltpu.VMEM((n_seq, WIDTH // MXU_DIM, MXU_DIM, MXU_DIM), F32),
                        pltpu.VMEM((n_seq, rows, WIDTH), F32)],
        compiler_params=pltpu.CompilerParams(dimension_semantics=("parallel", "arbitrary"),
                                             vmem_limit_bytes=VMEM_LIMIT),
    )(as3d(u), as3d(w), as3d(a), as3d(qg), as3d(kd), dl4, as3d(gz), nw, ones_bd)
    return out.reshape(batch * seq, WIDTH)


FOX_TQ = 256
FOX_TK = 256
VT_ROWS = HEAD_DIM + 16


def _fox_kernel(tk, q_ref, qx_ref, k_ref, kx_ref, v_ref, gate_ref, o_ref,
                vt_ref, kp_ref, s_ref, m_ref, acc_ref):
    qi = pl.program_id(1)
    tq = q_ref.shape[0]
    assert tk == tq
    heads = range(HEADS)

    @pl.when(qi == 0)
    def _():
        v_t = v_ref[...].T
        for h in heads:
            vt_ref[h * VT_ROWS:h * VT_ROWS + HEAD_DIM] = v_t[h * HEAD_DIM:(h + 1) * HEAD_DIM]
            vt_ref[h * VT_ROWS + HEAD_DIM:(h + 1) * VT_ROWS] = jnp.ones((VT_ROWS - HEAD_DIM, v_t.shape[1]), BF16)
        lane = lax.broadcasted_iota(jnp.int32, (k_ref.shape[0], LANES), 1)
        kx = kx_ref[...].astype(F32)
        for p in range(HEADS // 2):
            pair = k_ref[:, p * LANES:(p + 1) * LANES].astype(F32)
            for h in (2 * p, 2 * p + 1):
                own = pair if h % 2 == 0 else pltpu.roll(pair, HEAD_DIM, 1)
                ext = pltpu.roll(kx, (HEAD_DIM - EXT_STRIDE * h) % LANES, 1)
                blk = jnp.where(lane < HEAD_DIM, own, jnp.where(lane < HEAD_DIM + EXT_STRIDE, ext, 0.0))
                kp_ref[:, h * LANES:(h + 1) * LANES] = blk.astype(BF16)

    qx_t = qx_ref[...].T
    q_t = []
    for h in heads:
        pair_t = q_ref[:, (h // 2) * LANES:(h // 2 + 1) * LANES].T
        q_t.append(jnp.concatenate(
            [pair_t[(h % 2) * HEAD_DIM:(h % 2 + 1) * HEAD_DIM], qx_t[EXT_STRIDE * h:EXT_STRIDE * (h + 1)],
             jnp.zeros((LANES - HEAD_DIM - EXT_STRIDE, tq), BF16)], axis=0))
    kv_pos = lax.broadcasted_iota(jnp.int32, (tk, tq), 0)
    q_pos = lax.broadcasted_iota(jnp.int32, (tk, tq), 1)

    def score(h, j, masked):
        k0 = pl.multiple_of(j * tk, tk)
        s_h = jnp.dot(kp_ref[pl.ds(k0, tk), h * LANES:(h + 1) * LANES], q_t[h], preferred_element_type=F32)
        if masked:
            s_h = jnp.where(kv_pos <= q_pos, s_h, NEG_BIG)
        s_ref[h] = s_h

    def absorb(h, j):
        k0 = pl.multiple_of(j * tk, tk)
        m_prev = m_ref[h]
        s_h = s_ref[h]
        m_new = jnp.maximum(m_prev, jnp.max(s_h, axis=0, keepdims=True))
        p = jnp.exp2(s_h - m_new).astype(BF16)
        pv = jnp.dot(vt_ref[h * VT_ROWS:(h + 1) * VT_ROWS, pl.ds(k0, tk)], p, preferred_element_type=F32)
        m_ref[h] = m_new
        acc_ref[h] = jnp.exp2(m_prev - m_new) * acc_ref[h] + pv

    for h in heads:
        m_ref[h] = jnp.full((1, tq), NEG_BIG, F32)
        acc_ref[h] = jnp.zeros((VT_ROWS, tq), F32)
        score(h, qi, True)

    def step(j, pending):
        for h in heads:
            absorb(h, pending)
            score(h, j, False)
        return j

    pending = lax.fori_loop(0, qi, step, qi)
    for h in heads:
        absorb(h, pending)
    o_t = jnp.concatenate([acc_ref[h, 0:HEAD_DIM] / acc_ref[h, HEAD_DIM:HEAD_DIM + 1] for h in heads], axis=0)
    o_ref[...] = (o_t.T * gate_ref[...].astype(F32)).astype(BF16)


def _fox_call(fq, qx, fk, kx, fv, gate, *, seq, tq, tk):
    m = fq.shape[0]
    nq = seq // tq
    qblk = pl.BlockSpec((tq, WIDTH), lambda b, i: (b * nq + i, 0))
    kvblk = pl.BlockSpec((seq, WIDTH), lambda b, i: (b, 0))
    return pl.pallas_call(
        functools.partial(_fox_kernel, tk),
        grid=(m // seq, nq),
        in_specs=[qblk, pl.BlockSpec((tq, LANES), lambda b, i: (b * nq + i, 0)),
                  kvblk, pl.BlockSpec((seq, LANES), lambda b, i: (b, 0)), kvblk, qblk],
        out_specs=qblk,
        out_shape=jax.ShapeDtypeStruct((m, WIDTH), BF16),
        scratch_shapes=[pltpu.VMEM((HEADS * VT_ROWS, seq), BF16), pltpu.VMEM((seq, HEADS * LANES), BF16),
                        pltpu.VMEM((HEADS, tk, tq), F32),
                        pltpu.VMEM((HEADS, 1, tq), F32), pltpu.VMEM((HEADS, VT_ROWS, tq), F32)],
        compiler_params=pltpu.CompilerParams(dimension_semantics=("parallel", "arbitrary"),
                                             vmem_limit_bytes=VMEM_LIMIT),
    )(fq, qx, fk, kx, fv, gate)


FF_SPLITS = ((0, 1024), (1024, 1024), (2048, 768))


def _ffn_kernel(final, mg_ref, mf_ref, x_ref, wo_ref, n2_ref, wg_ref, wu_ref, wd_ref, fn_ref, o_ref):
    x1 = (x_ref[...]
          + jnp.dot(mg_ref[...], wo_ref[0:WIDTH, :], preferred_element_type=F32)
          + jnp.dot(mf_ref[...], wo_ref[WIDTH:2 * WIDTH, :], preferred_element_type=F32))
    h = (x1 * lax.rsqrt(jnp.mean(x1 * x1, axis=-1, keepdims=True) + EPS) * n2_ref[...]).astype(BF16)
    ffn = None
    for start, size in FF_SPLITS:
        gate = jnp.dot(h, wg_ref[:, start:start + size], preferred_element_type=F32)
        up = jnp.dot(h, wu_ref[:, start:start + size], preferred_element_type=F32)
        act = (gate * _sigmoid(gate) * up).astype(BF16)
        down = jnp.dot(act, wd_ref[start:start + size, :], preferred_element_type=F32)
        ffn = down if ffn is None else ffn + down
    y = x1 + ffn
    if final:
        y = y * lax.rsqrt(jnp.mean(y * y, axis=-1, keepdims=True) + EPS) * fn_ref[...]
    o_ref[...] = y


def _ffn_call(mix_g, mix_f, x2, wo, n2w, wg, wu, wd, fnw, *, tm, final):
    m = x2.shape[0]
    half = pl.BlockSpec((tm, WIDTH), lambda i: (i, 0))
    full = pl.BlockSpec((tm, D_MODEL), lambda i: (i, 0))

    def resident(shape):
        return pl.BlockSpec(shape, lambda i: (0, 0), pipeline_mode=pl.Buffered(1))

    return pl.pallas_call(
        functools.partial(_ffn_kernel, final),
        grid=(m // tm,),
        in_specs=[half, half, full, resident(wo.shape), resident(n2w.shape), resident(wg.shape),
                  resident(wu.shape), resident(wd.shape), resident(fnw.shape)],
        out_specs=full,
        out_shape=jax.ShapeDtypeStruct((m, D_MODEL), F32),
        compiler_params=pltpu.CompilerParams(dimension_semantics=("parallel",),
                                             vmem_limit_bytes=VMEM_LIMIT),
    )(mix_g, mix_f, x2, wo, n2w, wg, wu, wd, fnw)


def _lane_vec(parts):
    v = jnp.concatenate([p.astype(F32).reshape(-1) for p in parts])
    return jnp.pad(v, (0, N_SMALL - v.shape[0])).reshape(1, N_SMALL)


def kernel(x, norm1_w, w_in, gdn_conv_w, gdn_A_log, gdn_dt_bias, gdn_out_norm_w, fox_f_bias, fox_q_norm_w, fox_k_norm_w, w_out, norm2_w, w_ffn_gate, w_ffn_up, w_ffn_down, final_norm_w):
    batch, seq, _ = x.shape
    m = batch * seq
    depth = norm1_w.shape[0]
    zeros8 = jnp.zeros((HEADS,), F32)
    r2 = jnp.arange(MXU_DIM) // HEAD_DIM
    ones_bd = (r2[:, None] == r2[None, :]).astype(BF16)

    x2 = x.reshape(m, D_MODEL)
    for l in range(depth):
        w = w_in[l]
        big = 4 * WIDTH
        g_small = w[:, big:big + 2 * HEADS]
        f_start = big + 2 * HEADS
        f_small = w[:, f_start + big:f_start + big + HEADS]
        w_gdn = w[:, :big].astype(BF16)
        w_fox = w[:, f_start:f_start + big].astype(BF16)
        w_gates = jnp.concatenate(
            [g_small, f_small, jnp.zeros((D_MODEL, N_SMALL - 3 * HEADS), w.dtype)], axis=1).astype(BF16)
        bias_vec = _lane_vec([zeros8, gdn_dt_bias[l], fox_f_bias[l]])
        alog_vec = _lane_vec([zeros8, gdn_A_log[l], zeros8])
        fqw = (jnp.tile(fox_q_norm_w[l].astype(F32), HEADS) * (HEAD_DIM ** -0.5 * LOG2E)).reshape(1, WIDTH)
        fkw = jnp.tile(fox_k_norm_w[l].astype(F32), HEADS).reshape(1, WIDTH)

        gq, gk, gv, gz, fq, fk, fv, fg, small, kx, qx = _inproj_call(
            x2, norm1_w[l].reshape(1, D_MODEL), w_gdn, w_fox, w_gates, gdn_conv_w[l].astype(F32), bias_vec, alog_vec,
            fqw, fkw, ones_bd, seq=seq, tm=512)

        g_cum = small[:, HEADS:2 * HEADS]
        grow = g_cum.reshape(m // CHUNK, CHUNK, HEADS).transpose(0, 2, 1).reshape(m // CHUNK, 1, WIDTH)

        u, wmat, amat, qg, kd, dl = _gdn_wy_call(gq, gk, gv, small, grow, rows=512)
        nw = jnp.tile(gdn_out_norm_w[l].astype(F32), HEADS).reshape(1, WIDTH)
        mix_g = _gdn_scan_call(u, wmat, amat, qg, kd, dl, gz, nw, ones_bd, seq=seq, n_seq=2, rows=512)
        mix_f = _fox_call(fq, qx, fk, kx, fv, fg, seq=seq, tq=FOX_TQ, tk=FOX_TK)

        x2 = _ffn_call(mix_g, mix_f, x2, w_out[l].astype(BF16), norm2_w[l].reshape(1, D_MODEL),
                       w_ffn_gate[l].astype(BF16), w_ffn_up[l].astype(BF16), w_ffn_down[l].astype(BF16),
                       final_norm_w.reshape(1, D_MODEL), tm=512, final=(l == depth - 1))
    return x2.reshape(batch, seq, D_MODEL)
```

```python
import functools
import math

import numpy as np
import jax
import jax.numpy as jnp
from jax import lax
from jax.experimental import pallas as pl
from jax.experimental.pallas import tpu as pltpu

D_MODEL = 1024
HEADS = 8
HEAD_DIM = 64
WIDTH = HEADS * HEAD_DIM
CONV_K = 4
CHUNK = 64
D_FF = 2816
EPS = 1e-6

LANES = 128
MXU_DIM = 256
PACK = MXU_DIM // HEAD_DIM
N_SMALL = LANES
VMEM_LIMIT = 56 * 1024 * 1024

F32 = jnp.float32
BF16 = jnp.bfloat16
NEG_BIG = -1e30
LOG2E = 1.4426950408889634
EXT_STRIDE = 16


def _const_spec(shape):
    nd = len(shape)
    return pl.BlockSpec(shape, lambda *_: (0,) * nd)


def _sigmoid(x):
    return 1.0 / (1.0 + jnp.exp(-x))


def _softplus(x):
    return jnp.maximum(x, 0.0) + jnp.log1p(jnp.exp(-jnp.abs(x)))


def _head_sums(y2, ones_bd):
    yb = y2.astype(BF16)
    parts = [jnp.dot(yb[:, c * MXU_DIM:(c + 1) * MXU_DIM], ones_bd, preferred_element_type=F32)
             for c in range(WIDTH // MXU_DIM)]
    return jnp.concatenate(parts, axis=1)


def _seg_cumsum(v, seg):
    pos = lax.broadcasted_iota(jnp.int32, v.shape, 0) & (seg - 1)
    d = 1
    while d < seg:
        v = v + jnp.where(pos >= d, pltpu.roll(v, d, 0), 0.0)
        d *= 2
    return v


def _inproj_kernel(tiles_per_seq, x_ref, n1_ref, wg_ref, wf_ref, ws_ref, cw_ref, bias_ref, alog_ref, fqw_ref, fkw_ref,
                   ones_ref, sel_ref, one_ref,
                   gq_ref, gk_ref, gv_ref, gz_ref, fq_ref, fk_ref, fv_ref, fg_ref, sm_ref, kx_ref, qx_ref,
                   tail_ref, carry_ref):
    i = pl.program_id(0)
    tm = x_ref.shape[0]

    @pl.when(i % tiles_per_seq == 0)
    def _():
        tail_ref[:, 0:8, :] = jnp.zeros((3, 8, WIDTH), F32)
        carry_ref[...] = jnp.zeros_like(carry_ref)

    x = x_ref[...]
    h = (x * lax.rsqrt(jnp.mean(x * x, axis=-1, keepdims=True) + EPS) * n1_ref[...]).astype(BF16)
    ones_bd = ones_ref[...]

    def proj(s):
        if s == 8:
            return jnp.dot(h, ws_ref[...], preferred_element_type=F32)
        w_ref = wg_ref if s < 4 else wf_ref
        return jnp.dot(h, w_ref[:, (s % 4) * WIDTH:(s % 4 + 1) * WIDTH], preferred_element_type=F32)

    ys = [proj(s) for s in range(9)]

    for s, out_ref in enumerate((gq_ref, gk_ref, gv_ref)):
        y = ys[s]
        cw = cw_ref[:, s * WIDTH:(s + 1) * WIDTH]
        tail_ref[s, 8:8 + tm] = y
        acc = y * cw[CONV_K - 1:CONV_K]
        for k in range(1, CONV_K):
            acc = acc + tail_ref[s, 8 - k:8 - k + tm] * cw[CONV_K - 1 - k:CONV_K - k]
        tail_ref[s, 0:8] = y[tm - 8:]
        act = acc * _sigmoid(acc)
        if s < 2:
            act = act * lax.rsqrt(_head_sums(act * act, ones_bd) + EPS)
        out_ref[...] = act.astype(BF16)

    z = ys[3]
    gz_ref[...] = (z * _sigmoid(z)).astype(BF16)

    for s, out_ref, w_norm in ((4, fq_ref, fqw_ref), (5, fk_ref, fkw_ref)):
        y = ys[s]
        ms = _head_sums(y * y, ones_bd) * (1.0 / HEAD_DIM)
        out_ref[...] = (y * lax.rsqrt(ms + EPS) * w_norm[...]).astype(BF16)

    fv_ref[...] = ys[6].astype(BF16)
    fg_ref[...] = _sigmoid(ys[7]).astype(BF16)

    t = ys[8] + bias_ref[...]
    lane = lax.broadcasted_iota(jnp.int32, t.shape, 1)
    beta = _sigmoid(t)
    g_log = -jnp.exp(alog_ref[...]) * _softplus(t)
    log_f = -_softplus(-t)
    g_cum = _seg_cumsum(g_log, CHUNK)
    f_cum = _seg_cumsum(log_f, tm) + carry_ref[0:1, :]
    carry_ref[0:1, :] = f_cum[tm - 1:tm, :]
    sm_ref[...] = jnp.where(lane < HEADS, beta, jnp.where(lane < 2 * HEADS, g_cum, f_cum))

    f2 = f_cum * LOG2E
    hi = f2.astype(BF16)
    r1 = f2 - hi.astype(F32)
    mid = r1.astype(BF16)
    lo = (r1 - mid.astype(F32)).astype(BF16)
    ext = jnp.dot(jnp.concatenate([hi, mid, lo], axis=1), sel_ref[...], preferred_element_type=F32) + one_ref[...]
    kx_ref[...] = ext[:, :LANES].astype(BF16)
    qx_ref[...] = ext[:, LANES:].astype(BF16)


def _bias_selectors():
    sel = np.zeros((3 * LANES, 2 * LANES), np.float32)
    one = np.zeros((1, 2 * LANES), np.float32)
    for h in range(HEADS):
        src = 2 * HEADS + h
        for part in range(3):
            sel[part * LANES + src, EXT_STRIDE * h + part] = -1.0
            sel[part * LANES + src, LANES + EXT_STRIDE * h + 3 + part] = 1.0
            one[0, EXT_STRIDE * h + 3 + part] = 1.0
            one[0, LANES + EXT_STRIDE * h + part] = 1.0
    return jnp.asarray(sel, BF16), jnp.asarray(one, F32)


def _inproj_call(x2, n1w, w_gdn, w_fox, w_gates, conv_w, bias_vec, alog_vec, fqw, fkw, ones_bd, *, seq, tm):
    m = x2.shape[0]
    assert seq % tm == 0 and tm % CHUNK == 0 and (tm & (tm - 1)) == 0
    sel, one = _bias_selectors()
    wide = pl.BlockSpec((tm, WIDTH), lambda i: (i, 0))
    narrow = pl.BlockSpec((tm, LANES), lambda i: (i, 0))
    out_shape = ([jax.ShapeDtypeStruct((m, WIDTH), BF16)] * 8 + [jax.ShapeDtypeStruct((m, N_SMALL), F32)]
                 + [jax.ShapeDtypeStruct((m, LANES), BF16)] * 2)
    return pl.pallas_call(
        functools.partial(_inproj_kernel, seq // tm),
        grid=(m // tm,),
        in_specs=[pl.BlockSpec((tm, D_MODEL), lambda i: (i, 0)),
                  _const_spec(n1w.shape), _const_spec(w_gdn.shape), _const_spec(w_fox.shape),
                  _const_spec(w_gates.shape), _const_spec(conv_w.shape),
                  _const_spec(bias_vec.shape), _const_spec(alog_vec.shape),
                  _const_spec(fqw.shape), _const_spec(fkw.shape), _const_spec(ones_bd.shape),
                  _const_spec(sel.shape), _const_spec(one.shape)],
        out_specs=[wide] * 8 + [narrow] * 3,
        out_shape=out_shape,
        scratch_shapes=[pltpu.VMEM((3, tm + 8, WIDTH), F32), pltpu.VMEM((8, N_SMALL), F32)],
        compiler_params=pltpu.CompilerParams(dimension_semantics=("arbitrary",),
                                             vmem_limit_bytes=VMEM_LIMIT),
    )(x2, n1w, w_gdn, w_fox, w_gates, conv_w, bias_vec, alog_vec, fqw, fkw, ones_bd, sel, one)


def _lane_lo_mask():
    return lax.broadcasted_iota(jnp.int32, (CHUNK, LANES), 1) < HEAD_DIM


def _block_diag(x, lane_lo):
    xb = x.astype(BF16)
    zero = jnp.zeros((CHUNK, LANES), BF16)
    blocks = []
    for h in range(PACK):
        half = xb[:, (h // 2) * LANES:(h // 2 + 1) * LANES]
        keep = jnp.where(lane_lo if h % 2 == 0 else jnp.logical_not(lane_lo), half, zero)
        blocks.append(jnp.concatenate([keep, zero] if h < 2 else [zero, keep], axis=1))
    return jnp.concatenate(blocks, axis=0)


def _mm(lhs, rhs_bd):
    return jnp.dot(lhs.astype(BF16), rhs_bd, preferred_element_type=F32)


def _expand_heads(sm, first_lane):
    lane_lo = lax.broadcasted_iota(jnp.int32, (sm.shape[0], LANES), 1) < HEAD_DIM
    parts = []
    for p in range(HEADS // 2):
        c = first_lane + 2 * p
        even = jnp.broadcast_to(sm[:, c:c + 1], (sm.shape[0], LANES))
        odd = jnp.broadcast_to(sm[:, c + 1:c + 2], (sm.shape[0], LANES))
        parts.append(jnp.where(lane_lo, even, odd))
    return jnp.concatenate(parts, axis=1)


def _gdn_wy_kernel(q_ref, k_ref, v_ref, sm_ref, grow_ref,
                   u_ref, w_ref, a_ref, qg_ref, kd_ref, dl_ref):
    n_chunks = q_ref.shape[0] // CHUNK
    n_groups = WIDTH // MXU_DIM
    shape = (CHUNK, MXU_DIM)
    row = lax.broadcasted_iota(jnp.int32, shape, 0)
    col = lax.broadcasted_iota(jnp.int32, shape, 1) & (HEAD_DIM - 1)
    causal = row >= col
    strict = row > col
    eye = (row == col).astype(F32)
    bd_mask = _lane_lo_mask()
    scale = HEAD_DIM ** -0.5

    units = [(slice(c * CHUNK, (c + 1) * CHUNK), slice(g * MXU_DIM, (g + 1) * MXU_DIM), c)
             for c in range(n_chunks) for g in range(n_groups)]
    cs = range(len(units))
    bd = lambda xs: [_block_diag(x, bd_mask) for x in xs]
    mm = lambda ls, rs_: [_mm(a, b) for a, b in zip(ls, rs_)]
    stack = lambda xs, ys: [jnp.concatenate([x, y], axis=0) for x, y in zip(xs, ys)]

    sm = sm_ref[...]
    bx_all = _expand_heads(sm, 0)
    gx_all = _expand_heads(sm, HEADS)
    q = [q_ref[r, l].astype(F32) for r, l, _ in units]
    k = [k_ref[r, l].astype(F32) for r, l, _ in units]
    bx = [bx_all[r, l] for r, l, _ in units]
    gx = [gx_all[r, l] for r, l, _ in units]
    kb = [k[c] * bx[c] for c in cs]
    decay = [jnp.exp(jnp.where(causal, gx[c] - grow_ref[ch, :, l], NEG_BIG))
             for c, (_, l, ch) in enumerate(units)]

    bd_k = bd(k)
    s1 = [lax.dot_general(jnp.concatenate([kb[c], q[c]], axis=0).astype(BF16), bd_k[c],
                          (((1,), (1,)), ((), ())), preferred_element_type=F32) for c in cs]
    lmat = [jnp.where(strict, s1[c][:CHUNK] * decay[c], 0.0) for c in cs]
    for c, (r, l, _) in enumerate(units):
        a_ref[r, l] = (s1[c][CHUNK:] * decay[c] * scale).astype(BF16)

    blk8 = (row >> 3) == (col >> 3)
    n8 = [jnp.where(blk8, -l, 0.0) for l in lmat]
    t0 = [eye + n for n in n8]
    p1 = mm(n8, bd(n8))
    r = mm(stack(p1, t0), bd(p1))
    ta = [t0[c] + r[c][CHUNK:] for c in cs]
    tinv = [ta[c] + z for c, z in enumerate(mm(ta, bd([x[:CHUNK] for x in r])))]
    for ls in (3, 4, 5):
        off = ((row >> (ls + 1)) == (col >> (ls + 1))) & ((row >> ls) == (col >> ls) + 1)
        y = mm([jnp.where(off, l, 0.0) for l in lmat], bd(tinv))
        tinv = [tinv[c] - z for c, z in enumerate(mm(tinv, bd(y)))]

    eg = [jnp.exp(g) for g in gx]
    u = mm(tinv, bd([v_ref[r, l].astype(F32) * bx[c] for c, (r, l, _) in enumerate(units)]))
    w = mm(tinv, bd([kb[c] * eg[c] for c in cs]))
    for c, (r, l, ch) in enumerate(units):
        u_ref[r, l] = u[c].astype(BF16)
        w_ref[r, l] = w[c].astype(BF16)
        qg_ref[r, l] = (q[c] * eg[c] * scale).astype(BF16)
        g_last = gx[c][CHUNK - 1:CHUNK, :]
        kd_ref[r, l] = (k[c] * jnp.exp(g_last - gx[c])).astype(BF16)
        dl_ref[ch, :, l] = jnp.exp(g_last)


def _gdn_wy_call(gq, gk, gv, small, grow, *, rows):
    m = gq.shape[0]
    cpb = rows // CHUNK
    blk = pl.BlockSpec((rows, WIDTH), lambda i: (i, 0))
    rowblk = pl.BlockSpec((cpb, 1, WIDTH), lambda i: (i, 0, 0))
    bf = jax.ShapeDtypeStruct((m, WIDTH), BF16)
    return pl.pallas_call(
        _gdn_wy_kernel,
        grid=(m // rows,),
        in_specs=[blk] * 3 + [pl.BlockSpec((rows, N_SMALL), lambda i: (i, 0)), rowblk],
        out_specs=[blk] * 5 + [rowblk],
        out_shape=[bf] * 5 + [jax.ShapeDtypeStruct((m // CHUNK, 1, WIDTH), F32)],
        compiler_params=pltpu.CompilerParams(dimension_semantics=("parallel",),
                                             vmem_limit_bytes=VMEM_LIMIT),
    )(gq, gk, gv, small, grow)


SCAN_SEQS = 8


def _gdn_scan_kernel(u_ref, w_ref, a_ref, qg_ref, kd_ref, dl_ref, gz_ref, nw_ref, ones_ref,
                     o_ref, s_ref, oacc_ref):
    n_seq, rows, _ = u_ref.shape
    n_groups = WIDTH // MXU_DIM
    chains = [(b, g, slice(g * MXU_DIM, (g + 1) * MXU_DIM)) for b in range(n_seq) for g in range(n_groups)]
    lane_lo = _lane_lo_mask()
    lane_hi = jnp.logical_not(lane_lo)
    zero_half = jnp.zeros((HEAD_DIM, LANES), BF16)

    @pl.when(pl.program_id(1) == 0)
    def _():
        s_ref[...] = jnp.zeros_like(s_ref)

    def pair_lanes(h):
        return slice((h // 2) * LANES, (h // 2 + 1) * LANES)

    def as_block_diag(pieces):
        rows_ = [jnp.concatenate([p.astype(BF16), zero_half] if h < 2 else [zero_half, p.astype(BF16)], axis=1)
                 for h, p in enumerate(pieces)]
        return jnp.concatenate(rows_, axis=0)

    def body(c, carry):
        rs = pl.ds(pl.multiple_of(c * CHUNK, CHUNK), CHUNK)
        state = [[s_ref[b, g, h] for h in range(PACK)] for b, g, _ in chains]
        r = [jnp.dot(jnp.concatenate([w_ref[b, rs, l], qg_ref[b, rs, l]], axis=0),
                     as_block_diag(state[i]), preferred_element_type=F32) for i, (b, _, l) in enumerate(chains)]
        v_new = [u_ref[b, rs, l].astype(F32) - r[i][:CHUNK] for i, (b, _, l) in enumerate(chains)]
        upd = [lax.dot_general(kd_ref[b, rs, l], v_new[i].astype(BF16), (((0,), (0,)), ((), ())),
                               preferred_element_type=F32) for i, (b, _, l) in enumerate(chains)]
        for i, (b, g, l) in enumerate(chains):
            decay = dl_ref[b, c, :, l]
            for h in range(PACK):
                own = upd[i][h * HEAD_DIM:(h + 1) * HEAD_DIM, pair_lanes(h)]
                s_ref[b, g, h] = (state[i][h] * decay[:, pair_lanes(h)]
                                  + jnp.where(lane_lo if h % 2 == 0 else lane_hi, own, 0.0))
        for i, (b, _, l) in enumerate(chains):
            oacc_ref[b, rs, l] = r[i][CHUNK:] + jnp.dot(a_ref[b, rs, l], _block_diag(v_new[i], lane_lo),
                                                        preferred_element_type=F32)
        return carry

    lax.fori_loop(0, rows // CHUNK, body, 0)

    for b in range(n_seq):
        o = oacc_ref[b]
        ms = _head_sums(o * o, ones_ref[...]) * (1.0 / HEAD_DIM)
        o_ref[b] = (o * lax.rsqrt(ms + EPS) * nw_ref[...] * gz_ref[b].astype(F32)).astype(BF16)


def _gdn_scan_call(u, w, a, qg, kd, dl, gz, nw, ones_bd, *, seq, n_seq, rows):
    batch = u.shape[0] // seq
    as3d = lambda t: t.reshape(batch, seq, WIDTH)
    blk = pl.BlockSpec((n_seq, rows, WIDTH), lambda b, t: (b, t, 0))
    dl4 = dl.reshape(batch, seq // CHUNK, 1, WIDTH)
    out = pl.pallas_call(
        _gdn_scan_kernel,
        grid=(batch // n_seq, seq // rows),
        in_specs=[blk] * 5 + [pl.BlockSpec((n_seq, rows // CHUNK, 1, WIDTH), lambda b, t: (b, t, 0, 0)), blk,
                              _const_spec(nw.shape), _const_spec(ones_bd.shape)],
        out_specs=blk,
        out_shape=jax.ShapeDtypeStruct((batch, seq, WIDTH), BF16),
        scratch_shapes=[pltpu.VMEM((n_seq, WIDTH // MXU_DIM, PACK, HEAD_DIM, LANES), F32),
                        pltpu.VMEM((n_seq, rows, WIDTH), F32)],
        compiler_params=pltpu.CompilerParams(dimension_semantics=("parallel", "arbitrary"),
                                             vmem_limit_bytes=VMEM_LIMIT),
    )(as3d(u), as3d(w), as3d(a), as3d(qg), as3d(kd), dl4, as3d(gz), nw, ones_bd)
    return out.reshape(batch * seq, WIDTH)


FOX_TQ = 256
FOX_TK = 256
VT_ROWS = HEAD_DIM + 16


def _fox_kernel(tk, q_ref, qx_ref, k_ref, kx_ref, v_ref, gate_ref, o_ref,
                vt_ref, kp_ref, s_ref, m_ref, acc_ref):
    qi = pl.program_id(1)
    tq = q_ref.shape[0]
    assert tk == tq
    heads = range(HEADS)

    @pl.when(qi == 0)
    def _():
        v_t = v_ref[...].T
        for h in heads:
            vt_ref[h * VT_ROWS:h * VT_ROWS + HEAD_DIM] = v_t[h * HEAD_DIM:(h + 1) * HEAD_DIM]
            vt_ref[h * VT_ROWS + HEAD_DIM:(h + 1) * VT_ROWS] = jnp.ones((VT_ROWS - HEAD_DIM, v_t.shape[1]), BF16)
        lane = lax.broadcasted_iota(jnp.int32, (k_ref.shape[0], LANES), 1)
        kx = kx_ref[...].astype(F32)
        for p in range(HEADS // 2):
            pair = k_ref[:, p * LANES:(p + 1) * LANES].astype(F32)
            for h in (2 * p, 2 * p + 1):
                own = pair if h % 2 == 0 else pltpu.roll(pair, HEAD_DIM, 1)
                ext = pltpu.roll(kx, (HEAD_DIM - EXT_STRIDE * h) % LANES, 1)
                blk = jnp.where(lane < HEAD_DIM, own, jnp.where(lane < HEAD_DIM + EXT_STRIDE, ext, 0.0))
                kp_ref[:, h * LANES:(h + 1) * LANES] = blk.astype(BF16)

    qx_t = qx_ref[...].T
    q_t = []
    for h in heads:
        pair_t = q_ref[:, (h // 2) * LANES:(h // 2 + 1) * LANES].T
        q_t.append(jnp.concatenate(
            [pair_t[(h % 2) * HEAD_DIM:(h % 2 + 1) * HEAD_DIM], qx_t[EXT_STRIDE * h:EXT_STRIDE * (h + 1)],
             jnp.zeros((LANES - HEAD_DIM - EXT_STRIDE, tq), BF16)], axis=0))
    kv_pos = lax.broadcasted_iota(jnp.int32, (tk, tq), 0)
    q_pos = lax.broadcasted_iota(jnp.int32, (tk, tq), 1)

    def score(h, j, masked):
        k0 = pl.multiple_of(j * tk, tk)
        s_h = jnp.dot(kp_ref[pl.ds(k0, tk), h * LANES:(h + 1) * LANES], q_t[h], preferred_element_type=F32)
        if masked:
            s_h = jnp.where(kv_pos <= q_pos, s_h, NEG_BIG)
        s_ref[h] = s_h

    def absorb(h, j):
        k0 = pl.multiple_of(j * tk, tk)
        m_prev = m_ref[h]
        s_h = s_ref[h]
        m_new = jnp.maximum(m_prev, jnp.max(s_h, axis=0, keepdims=True))
        p = jnp.exp2(s_h - m_new).astype(BF16)
        pv = jnp.dot(vt_ref[h * VT_ROWS:(h + 1) * VT_ROWS, pl.ds(k0, tk)], p, preferred_element_type=F32)
        m_ref[h] = m_new
        acc_ref[h] = jnp.exp2(m_prev - m_new) * acc_ref[h] + pv

    for h in heads:
        m_ref[h] = jnp.full((1, tq), NEG_BIG, F32)
        acc_ref[h] = jnp.zeros((VT_ROWS, tq), F32)
        score(h, qi, True)

    def step(j, pending):
        for h in heads:
            absorb(h, pending)
            score(h, j, False)
        return j

    pending = lax.fori_loop(0, qi, step, qi)
    for p in range(HEADS // 2):
        lanes = slice(p * LANES, (p + 1) * LANES)
        absorb(2 * p, pending)
        absorb(2 * p + 1, pending)
        o_t = jnp.concatenate([acc_ref[h, 0:HEAD_DIM] / acc_ref[h, HEAD_DIM:HEAD_DIM + 1]
                               for h in (2 * p, 2 * p + 1)], axis=0)
        o_ref[:, lanes] = (o_t.T * gate_ref[:, lanes].astype(F32)).astype(BF16)


def _fox_call(fq, qx, fk, kx, fv, gate, *, seq, tq, tk):
    m = fq.shape[0]
    nq = seq // tq
    qblk = pl.BlockSpec((tq, WIDTH), lambda b, i: (b * nq + i, 0))
    kvblk = pl.BlockSpec((seq, WIDTH), lambda b, i: (b, 0))
    return pl.pallas_call(
        functools.partial(_fox_kernel, tk),
        grid=(m // seq, nq),
        in_specs=[qblk, pl.BlockSpec((tq, LANES), lambda b, i: (b * nq + i, 0)),
                  kvblk, pl.BlockSpec((seq, LANES), lambda b, i: (b, 0)), kvblk, qblk],
        out_specs=qblk,
        out_shape=jax.ShapeDtypeStruct((m, WIDTH), BF16),
        scratch_shapes=[pltpu.VMEM((HEADS * VT_ROWS, seq), BF16), pltpu.VMEM((seq, HEADS * LANES), BF16),
                        pltpu.VMEM((HEADS, tk, tq), F32),
                        pltpu.VMEM((HEADS, 1, tq), F32), pltpu.VMEM((HEADS, VT_ROWS, tq), F32)],
        compiler_params=pltpu.CompilerParams(dimension_semantics=("parallel", "arbitrary"),
                                             vmem_limit_bytes=VMEM_LIMIT),
    )(fq, qx, fk, kx, fv, gate)


FF_SPLITS = ((0, 1024), (1024, 1024), (2048, 768))


def _ffn_kernel(final, mg_ref, mf_ref, x_ref, wo_ref, n2_ref, wg_ref, wu_ref, wd_ref, fn_ref, o_ref):
    x1 = (x_ref[...]
          + jnp.dot(mg_ref[...], wo_ref[0:WIDTH, :], preferred_element_type=F32)
          + jnp.dot(mf_ref[...], wo_ref[WIDTH:2 * WIDTH, :], preferred_element_type=F32))
    h = (x1 * lax.rsqrt(jnp.mean(x1 * x1, axis=-1, keepdims=True) + EPS) * n2_ref[...]).astype(BF16)
    ffn = None
    for start, size in FF_SPLITS:
        gate = jnp.dot(h, wg_ref[:, start:start + size], preferred_element_type=F32)
        up = jnp.dot(h, wu_ref[:, start:start + size], preferred_element_type=F32)
        act = (gate * _sigmoid(gate) * up).astype(BF16)
        down = jnp.dot(act, wd_ref[start:start + size, :], preferred_element_type=F32)
        ffn = down if ffn is None else ffn + down
    y = x1 + ffn
    if final:
        y = y * lax.rsqrt(jnp.mean(y * y, axis=-1, keepdims=True) + EPS) * fn_ref[...]
    o_ref[...] = y


def _ffn_call(mix_g, mix_f, x2, wo, n2w, wg, wu, wd, fnw, *, tm, final):
    m = x2.shape[0]
    half = pl.BlockSpec((tm, WIDTH), lambda i: (i, 0))
    full = pl.BlockSpec((tm, D_MODEL), lambda i: (i, 0))

    def resident(shape):
        return pl.BlockSpec(shape, lambda i: (0, 0), pipeline_mode=pl.Buffered(1))

    return pl.pallas_call(
        functools.partial(_ffn_kernel, final),
        grid=(m // tm,),
        in_specs=[half, half, full, resident(wo.shape), resident(n2w.shape), resident(wg.shape),
                  resident(wu.shape), resident(wd.shape), resident(fnw.shape)],
        out_specs=full,
        out_shape=jax.ShapeDtypeStruct((m, D_MODEL), F32),
        compiler_params=pltpu.CompilerParams(dimension_semantics=("parallel",),
                                             vmem_limit_bytes=VMEM_LIMIT),
    )(mix_g, mix_f, x2, wo, n2w, wg, wu, wd, fnw)


def _lane_vec(parts):
    v = jnp.concatenate([p.astype(F32).reshape(-1) for p in parts])
    return jnp.pad(v, (0, N_SMALL - v.shape[0])).reshape(1, N_SMALL)


def kernel(x, norm1_w, w_in, gdn_conv_w, gdn_A_log, gdn_dt_bias, gdn_out_norm_w, fox_f_bias, fox_q_norm_w, fox_k_norm_w, w_out, norm2_w, w_ffn_gate, w_ffn_up, w_ffn_down, final_norm_w):
    batch, seq, _ = x.shape
    m = batch * seq
    depth = norm1_w.shape[0]
    zeros8 = jnp.zeros((HEADS,), F32)
    r2 = jnp.arange(MXU_DIM) // HEAD_DIM
    ones_bd = (r2[:, None] == r2[None, :]).astype(BF16)

    x2 = x.reshape(m, D_MODEL)
    for l in range(depth):
        w = w_in[l].astype(BF16)
        big = 4 * WIDTH
        g_small = w[:, big:big + 2 * HEADS]
        f_start = big + 2 * HEADS
        f_small = w[:, f_start + big:f_start + big + HEADS]
        w_gdn = w[:, :big]
        w_fox = w[:, f_start:f_start + big]
        w_gates = jnp.concatenate(
            [g_small, f_small, jnp.zeros((D_MODEL, N_SMALL - 3 * HEADS), BF16)], axis=1)
        bias_vec = _lane_vec([zeros8, gdn_dt_bias[l], fox_f_bias[l]])
        alog_vec = _lane_vec([zeros8, gdn_A_log[l], zeros8])
        fqw = (jnp.tile(fox_q_norm_w[l].astype(F32), HEADS) * (HEAD_DIM ** -0.5 * LOG2E)).reshape(1, WIDTH)
        fkw = jnp.tile(fox_k_norm_w[l].astype(F32), HEADS).reshape(1, WIDTH)

        gq, gk, gv, gz, fq, fk, fv, fg, small, kx, qx = _inproj_call(
            x2, norm1_w[l].reshape(1, D_MODEL), w_gdn, w_fox, w_gates, gdn_conv_w[l].astype(F32), bias_vec, alog_vec,
            fqw, fkw, ones_bd, seq=seq, tm=512)

        g_cum = small[:, HEADS:2 * HEADS]
        grow = g_cum.reshape(m // CHUNK, CHUNK, HEADS).transpose(0, 2, 1).reshape(m // CHUNK, 1, WIDTH)

        u, wmat, amat, qg, kd, dl = _gdn_wy_call(gq, gk, gv, small, grow, rows=512)
        nw = jnp.tile(gdn_out_norm_w[l].astype(F32), HEADS).reshape(1, WIDTH)
        mix_g = _gdn_scan_call(u, wmat, amat, qg, kd, dl, gz, nw, ones_bd, seq=seq, n_seq=math.gcd(batch, SCAN_SEQS), rows=256)
        mix_f = _fox_call(fq, qx, fk, kx, fv, fg, seq=seq, tq=FOX_TQ, tk=FOX_TK)

        x2 = _ffn_call(mix_g, mix_f, x2, w_out[l].astype(BF16), norm2_w[l].reshape(1, D_MODEL),
                       w_ffn_gate[l].astype(BF16), w_ffn_up[l].astype(BF16), w_ffn_down[l].astype(BF16),
                       final_norm_w.reshape(1, D_MODEL), tm=512, final=(l == depth - 1))
    return x2.reshape(batch, seq, D_MODEL)
```

```python
import functools
import math

import numpy as np
import jax
import jax.numpy as jnp
from jax import lax
from jax.experimental import pallas as pl
from jax.experimental.pallas import tpu as pltpu

D_MODEL = 1024
HEADS = 8
HEAD_DIM = 64
WIDTH = HEADS * HEAD_DIM
CONV_K = 4
CHUNK = 64
D_FF = 2816
EPS = 1e-6

LANES = 128
MXU_DIM = 256
PACK = MXU_DIM // HEAD_DIM
N_SMALL = LANES
VMEM_LIMIT = 56 * 1024 * 1024

F32 = jnp.float32
BF16 = jnp.bfloat16
NEG_BIG = -1e30
LOG2E = 1.4426950408889634
EXT_STRIDE = 16


def _const_spec(shape):
    nd = len(shape)
    return pl.BlockSpec(shape, lambda *_: (0,) * nd)


def _sigmoid(x):
    return 1.0 / (1.0 + jnp.exp(-x))


def _softplus(x):
    return jnp.maximum(x, 0.0) + jnp.log1p(jnp.exp(-jnp.abs(x)))


def _head_sums(y2, ones_bd):
    yb = y2.astype(BF16)
    parts = [jnp.dot(yb[:, c * MXU_DIM:(c + 1) * MXU_DIM], ones_bd, preferred_element_type=F32)
             for c in range(WIDTH // MXU_DIM)]
    return jnp.concatenate(parts, axis=1)


def _seg_cumsum(v, seg):
    pos = lax.broadcasted_iota(jnp.int32, v.shape, 0) & (seg - 1)
    d = 1
    while d < seg:
        v = v + jnp.where(pos >= d, pltpu.roll(v, d, 0), 0.0)
        d *= 2
    return v


def _inproj_kernel(tiles_per_seq, x_ref, n1_ref, wg_ref, wf_ref, ws_ref, cw_ref, bias_ref, alog_ref, fqw_ref, fkw_ref,
                   ones_ref, sel_ref, one_ref,
                   gq_ref, gk_ref, gv_ref, gz_ref, fq_ref, fk_ref, fv_ref, fg_ref, sm_ref, kx_ref, qx_ref, gt_ref,
                   tail_ref, carry_ref):
    i = pl.program_id(0)
    tm = x_ref.shape[0]

    @pl.when(i % tiles_per_seq == 0)
    def _():
        tail_ref[:, 0:8, :] = jnp.zeros((3, 8, WIDTH), F32)
        carry_ref[...] = jnp.zeros_like(carry_ref)

    x = x_ref[...]
    h = (x * lax.rsqrt(jnp.mean(x * x, axis=-1, keepdims=True) + EPS) * n1_ref[...]).astype(BF16)
    ones_bd = ones_ref[...]

    def proj(s):
        if s == 8:
            return jnp.dot(h, ws_ref[...], preferred_element_type=F32)
        w_ref = wg_ref if s < 4 else wf_ref
        return jnp.dot(h, w_ref[:, (s % 4) * WIDTH:(s % 4 + 1) * WIDTH], preferred_element_type=F32)

    ys = [proj(s) for s in range(9)]

    for s, out_ref in enumerate((gq_ref, gk_ref, gv_ref)):
        y = ys[s]
        cw = cw_ref[:, s * WIDTH:(s + 1) * WIDTH]
        tail_ref[s, 8:8 + tm] = y
        acc = y * cw[CONV_K - 1:CONV_K]
        for k in range(1, CONV_K):
            acc = acc + tail_ref[s, 8 - k:8 - k + tm] * cw[CONV_K - 1 - k:CONV_K - k]
        tail_ref[s, 0:8] = y[tm - 8:]
        act = acc * _sigmoid(acc)
        if s < 2:
            act = act * lax.rsqrt(_head_sums(act * act, ones_bd) + EPS)
        out_ref[...] = act.astype(BF16)

    z = ys[3]
    gz_ref[...] = (z * _sigmoid(z)).astype(BF16)

    for s, out_ref, w_norm in ((4, fq_ref, fqw_ref), (5, fk_ref, fkw_ref)):
        y = ys[s]
        ms = _head_sums(y * y, ones_bd) * (1.0 / HEAD_DIM)
        out_ref[...] = (y * lax.rsqrt(ms + EPS) * w_norm[...]).astype(BF16)

    fv_ref[...] = ys[6].astype(BF16)
    fg_ref[...] = _sigmoid(ys[7]).astype(BF16)

    t = ys[8] + bias_ref[...]
    lane = lax.broadcasted_iota(jnp.int32, t.shape, 1)
    beta = _sigmoid(t)
    g_log = -jnp.exp(alog_ref[...]) * _softplus(t)
    log_f = -_softplus(-t)
    g_cum = _seg_cumsum(g_log, CHUNK)
    f_cum = _seg_cumsum(log_f, tm) + carry_ref[0:1, :]
    carry_ref[0:1, :] = f_cum[tm - 1:tm, :]
    sm_ref[...] = jnp.where(lane < HEADS, beta, jnp.where(lane < 2 * HEADS, g_cum, f_cum))
    gt_ref[...] = g_cum.T[HEADS:2 * HEADS, :]

    f2 = f_cum * LOG2E
    hi = f2.astype(BF16)
    r1 = f2 - hi.astype(F32)
    mid = r1.astype(BF16)
    lo = (r1 - mid.astype(F32)).astype(BF16)
    ext = jnp.dot(jnp.concatenate([hi, mid, lo], axis=1), sel_ref[...], preferred_element_type=F32) + one_ref[...]
    kx_ref[...] = ext[:, :LANES].astype(BF16)
    qx_ref[...] = ext[:, LANES:].astype(BF16)


def _bias_selectors():
    sel = np.zeros((3 * LANES, 2 * LANES), np.float32)
    one = np.zeros((1, 2 * LANES), np.float32)
    for h in range(HEADS):
        src = 2 * HEADS + h
        for part in range(3):
            sel[part * LANES + src, EXT_STRIDE * h + part] = -1.0
            sel[part * LANES + src, LANES + EXT_STRIDE * h + 3 + part] = 1.0
            one[0, EXT_STRIDE * h + 3 + part] = 1.0
            one[0, LANES + EXT_STRIDE * h + part] = 1.0
    return jnp.asarray(sel, BF16), jnp.asarray(one, F32)


def _inproj_call(x2, n1w, w_gdn, w_fox, w_gates, conv_w, bias_vec, alog_vec, fqw, fkw, ones_bd, *, seq, tm):
    m = x2.shape[0]
    assert seq % tm == 0 and tm % CHUNK == 0 and (tm & (tm - 1)) == 0
    sel, one = _bias_selectors()
    wide = pl.BlockSpec((tm, WIDTH), lambda i: (i, 0))
    narrow = pl.BlockSpec((tm, LANES), lambda i: (i, 0))
    out_shape = ([jax.ShapeDtypeStruct((m, WIDTH), BF16)] * 8 + [jax.ShapeDtypeStruct((m, N_SMALL), F32)]
                 + [jax.ShapeDtypeStruct((m, LANES), BF16)] * 2 + [jax.ShapeDtypeStruct((HEADS, m), F32)])
    return pl.pallas_call(
        functools.partial(_inproj_kernel, seq // tm),
        grid=(m // tm,),
        in_specs=[pl.BlockSpec((tm, D_MODEL), lambda i: (i, 0)),
                  _const_spec(n1w.shape), _const_spec(w_gdn.shape), _const_spec(w_fox.shape),
                  _const_spec(w_gates.shape), _const_spec(conv_w.shape),
                  _const_spec(bias_vec.shape), _const_spec(alog_vec.shape),
                  _const_spec(fqw.shape), _const_spec(fkw.shape), _const_spec(ones_bd.shape),
                  _const_spec(sel.shape), _const_spec(one.shape)],
        out_specs=[wide] * 8 + [narrow] * 3 + [pl.BlockSpec((HEADS, tm), lambda i: (0, i))],
        out_shape=out_shape,
        scratch_shapes=[pltpu.VMEM((3, tm + 8, WIDTH), F32), pltpu.VMEM((8, N_SMALL), F32)],
        compiler_params=pltpu.CompilerParams(dimension_semantics=("arbitrary",),
                                             vmem_limit_bytes=VMEM_LIMIT),
    )(x2, n1w, w_gdn, w_fox, w_gates, conv_w, bias_vec, alog_vec, fqw, fkw, ones_bd, sel, one)


def _lane_lo_mask():
    return lax.broadcasted_iota(jnp.int32, (CHUNK, LANES), 1) < HEAD_DIM


def _block_diag(x, lane_lo):
    xb = x.astype(BF16)
    zero = jnp.zeros((CHUNK, LANES), BF16)
    blocks = []
    for h in range(PACK):
        half = xb[:, (h // 2) * LANES:(h // 2 + 1) * LANES]
        keep = jnp.where(lane_lo if h % 2 == 0 else jnp.logical_not(lane_lo), half, zero)
        blocks.append(jnp.concatenate([keep, zero] if h < 2 else [zero, keep], axis=1))
    return jnp.concatenate(blocks, axis=0)


def _mm(lhs, rhs_bd):
    return jnp.dot(lhs.astype(BF16), rhs_bd, preferred_element_type=F32)


def _expand_heads(sm, first_lane):
    lane_lo = lax.broadcasted_iota(jnp.int32, (sm.shape[0], LANES), 1) < HEAD_DIM
    parts = []
    for p in range(HEADS // 2):
        c = first_lane + 2 * p
        even = jnp.broadcast_to(sm[:, c:c + 1], (sm.shape[0], LANES))
        odd = jnp.broadcast_to(sm[:, c + 1:c + 2], (sm.shape[0], LANES))
        parts.append(jnp.where(lane_lo, even, odd))
    return jnp.concatenate(parts, axis=1)


def _gdn_wy_kernel(q_ref, k_ref, v_ref, sm_ref, gt_ref,
                   u_ref, w_ref, a_ref, qg_ref, kd_ref, dl_ref):
    n_chunks = q_ref.shape[0] // CHUNK
    n_groups = WIDTH // MXU_DIM
    shape = (CHUNK, MXU_DIM)
    row = lax.broadcasted_iota(jnp.int32, shape, 0)
    col = lax.broadcasted_iota(jnp.int32, shape, 1) & (HEAD_DIM - 1)
    causal = row >= col
    strict = row > col
    eye = (row == col).astype(F32)
    bd_mask = _lane_lo_mask()
    scale = HEAD_DIM ** -0.5

    units = [(slice(c * CHUNK, (c + 1) * CHUNK), slice(g * MXU_DIM, (g + 1) * MXU_DIM), c)
             for c in range(n_chunks) for g in range(n_groups)]
    cs = range(len(units))
    bd = lambda xs: [_block_diag(x, bd_mask) for x in xs]
    mm = lambda ls, rs_: [_mm(a, b) for a, b in zip(ls, rs_)]
    stack = lambda xs, ys: [jnp.concatenate([x, y], axis=0) for x, y in zip(xs, ys)]

    sm = sm_ref[...]
    bx_all = _expand_heads(sm, 0)
    gx_all = _expand_heads(sm, HEADS)
    q = [q_ref[r, l].astype(F32) for r, l, _ in units]
    k = [k_ref[r, l].astype(F32) for r, l, _ in units]
    bx = [bx_all[r, l] for r, l, _ in units]
    gx = [gx_all[r, l] for r, l, _ in units]
    kb = [k[c] * bx[c] for c in cs]
    def key_row(ch, g):
        return jnp.concatenate([gt_ref[g * PACK + h:g * PACK + h + 1, ch * CHUNK:(ch + 1) * CHUNK]
                                for h in range(PACK)], axis=1)

    decay = [jnp.exp(jnp.where(causal, gx[c] - key_row(ch, l.start // MXU_DIM), NEG_BIG))
             for c, (_, l, ch) in enumerate(units)]

    bd_k = bd(k)
    s1 = [lax.dot_general(jnp.concatenate([kb[c], q[c]], axis=0).astype(BF16), bd_k[c],
                          (((1,), (1,)), ((), ())), preferred_element_type=F32) for c in cs]
    lmat = [jnp.where(strict, s1[c][:CHUNK] * decay[c], 0.0) for c in cs]
    for c, (r, l, _) in enumerate(units):
        a_ref[r, l] = (s1[c][CHUNK:] * decay[c] * scale).astype(BF16)

    blk8 = (row >> 3) == (col >> 3)
    n8 = [jnp.where(blk8, -l, 0.0) for l in lmat]
    t0 = [eye + n for n in n8]
    p1 = mm(n8, bd(n8))
    r = mm(stack(p1, t0), bd(p1))
    ta = [t0[c] + r[c][CHUNK:] for c in cs]
    tinv = [ta[c] + z for c, z in enumerate(mm(ta, bd([x[:CHUNK] for x in r])))]
    for ls in (3, 4, 5):
        off = ((row >> (ls + 1)) == (col >> (ls + 1))) & ((row >> ls) == (col >> ls) + 1)
        y = mm([jnp.where(off, l, 0.0) for l in lmat], bd(tinv))
        tinv = [tinv[c] - z for c, z in enumerate(mm(tinv, bd(y)))]

    eg = [jnp.exp(g) for g in gx]
    u = mm(tinv, bd([v_ref[r, l].astype(F32) * bx[c] for c, (r, l, _) in enumerate(units)]))
    w = mm(tinv, bd([kb[c] * eg[c] for c in cs]))
    for c, (r, l, ch) in enumerate(units):
        u_ref[r, l] = u[c].astype(BF16)
        w_ref[r, l] = w[c].astype(BF16)
        qg_ref[r, l] = (q[c] * eg[c] * scale).astype(BF16)
        g_last = gx[c][CHUNK - 1:CHUNK, :]
        kd_ref[r, l] = (k[c] * jnp.exp(g_last - gx[c])).astype(BF16)
        dl_ref[ch, :, l] = jnp.exp(g_last)


def _gdn_wy_call(gq, gk, gv, small, g_t, *, rows):
    m = gq.shape[0]
    cpb = rows // CHUNK
    blk = pl.BlockSpec((rows, WIDTH), lambda i: (i, 0))
    rowblk = pl.BlockSpec((cpb, 1, WIDTH), lambda i: (i, 0, 0))
    bf = jax.ShapeDtypeStruct((m, WIDTH), BF16)
    return pl.pallas_call(
        _gdn_wy_kernel,
        grid=(m // rows,),
        in_specs=[blk] * 3 + [pl.BlockSpec((rows, N_SMALL), lambda i: (i, 0)),
                              pl.BlockSpec((HEADS, rows), lambda i: (0, i))],
        out_specs=[blk] * 5 + [rowblk],
        out_shape=[bf] * 5 + [jax.ShapeDtypeStruct((m // CHUNK, 1, WIDTH), F32)],
        compiler_params=pltpu.CompilerParams(dimension_semantics=("parallel",),
                                             vmem_limit_bytes=VMEM_LIMIT),
    )(gq, gk, gv, small, g_t)


SCAN_SEQS = 8


def _gdn_scan_kernel(n_cast, u_ref, w_ref, a_ref, qg_ref, kd_ref, dl_ref, gz_ref, nw_ref, ones_ref, *refs):
    cast_in, o_ref, cast_out = refs[:n_cast], refs[n_cast], refs[n_cast + 1:2 * n_cast + 1]
    s_ref, oacc_ref = refs[2 * n_cast + 1:]
    for src, dst in zip(cast_in, cast_out):
        dst[...] = src[...].astype(BF16)
    n_seq, rows, _ = u_ref.shape
    n_groups = WIDTH // MXU_DIM
    chains = [(b, g, slice(g * MXU_DIM, (g + 1) * MXU_DIM)) for b in range(n_seq) for g in range(n_groups)]
    lane_lo = _lane_lo_mask()
    lane_hi = jnp.logical_not(lane_lo)
    zero_half = jnp.zeros((HEAD_DIM, LANES), BF16)

    @pl.when(pl.program_id(1) == 0)
    def _():
        s_ref[...] = jnp.zeros_like(s_ref)

    def pair_lanes(h):
        return slice((h // 2) * LANES, (h // 2 + 1) * LANES)

    def as_block_diag(pieces):
        rows_ = [jnp.concatenate([p.astype(BF16), zero_half] if h < 2 else [zero_half, p.astype(BF16)], axis=1)
                 for h, p in enumerate(pieces)]
        return jnp.concatenate(rows_, axis=0)

    def body(c, carry):
        rs = pl.ds(pl.multiple_of(c * CHUNK, CHUNK), CHUNK)
        state = [[s_ref[b, g, h] for h in range(PACK)] for b, g, _ in chains]
        r = [jnp.dot(jnp.concatenate([w_ref[b, rs, l], qg_ref[b, rs, l]], axis=0),
                     as_block_diag(state[i]), preferred_element_type=F32) for i, (b, _, l) in enumerate(chains)]
        v_new = [u_ref[b, rs, l].astype(F32) - r[i][:CHUNK] for i, (b, _, l) in enumerate(chains)]
        upd = [lax.dot_general(kd_ref[b, rs, l], v_new[i].astype(BF16), (((0,), (0,)), ((), ())),
                               preferred_element_type=F32) for i, (b, _, l) in enumerate(chains)]
        for i, (b, g, l) in enumerate(chains):
            decay = dl_ref[b, c, :, l]
            for h in range(PACK):
                own = upd[i][h * HEAD_DIM:(h + 1) * HEAD_DIM, pair_lanes(h)]
                s_ref[b, g, h] = (state[i][h] * decay[:, pair_lanes(h)]
                                  + jnp.where(lane_lo if h % 2 == 0 else lane_hi, own, 0.0))
        for i, (b, _, l) in enumerate(chains):
            oacc_ref[b, rs, l] = r[i][CHUNK:] + jnp.dot(a_ref[b, rs, l], _block_diag(v_new[i], lane_lo),
                                                        preferred_element_type=F32)
        return carry

    lax.fori_loop(0, rows // CHUNK, body, 0)

    for b in range(n_seq):
        o = oacc_ref[b]
        ms = _head_sums(o * o, ones_ref[...]) * (1.0 / HEAD_DIM)
        o_ref[b] = (o * lax.rsqrt(ms + EPS) * nw_ref[...] * gz_ref[b].astype(F32)).astype(BF16)


def _gdn_scan_call(u, w, a, qg, kd, dl, gz, nw, ones_bd, weights_f32, *, seq, n_seq, rows):
    batch = u.shape[0] // seq
    steps = seq // rows
    as3d = lambda t: t.reshape(batch, seq, WIDTH)
    blk = pl.BlockSpec((n_seq, rows, WIDTH), lambda b, t: (b, t, 0))
    dl4 = dl.reshape(batch, seq // CHUNK, 1, WIDTH)
    slabs = [pl.BlockSpec((wt.shape[0] // steps, wt.shape[1]), lambda b, t: (t, 0)) for wt in weights_f32]
    assert all(wt.shape[0] % (16 * steps) == 0 for wt in weights_f32)
    outs = pl.pallas_call(
        functools.partial(_gdn_scan_kernel, len(weights_f32)),
        grid=(batch // n_seq, steps),
        in_specs=[blk] * 5 + [pl.BlockSpec((n_seq, rows // CHUNK, 1, WIDTH), lambda b, t: (b, t, 0, 0)), blk,
                              _const_spec(nw.shape), _const_spec(ones_bd.shape)] + slabs,
        out_specs=[blk] + slabs,
        out_shape=[jax.ShapeDtypeStruct((batch, seq, WIDTH), BF16)]
                  + [jax.ShapeDtypeStruct(wt.shape, BF16) for wt in weights_f32],
        scratch_shapes=[pltpu.VMEM((n_seq, WIDTH // MXU_DIM, PACK, HEAD_DIM, LANES), F32),
                        pltpu.VMEM((n_seq, rows, WIDTH), F32)],
        compiler_params=pltpu.CompilerParams(dimension_semantics=("arbitrary", "arbitrary"),
                                             vmem_limit_bytes=VMEM_LIMIT),
    )(as3d(u), as3d(w), as3d(a), as3d(qg), as3d(kd), dl4, as3d(gz), nw, ones_bd, *weights_f32)
    return outs[0].reshape(batch * seq, WIDTH), outs[1:]


FOX_TQ = 256
FOX_TK = 256
VT_ROWS = HEAD_DIM + 16


def _fox_kernel(tk, q_ref, qx_ref, k_ref, kx_ref, v_ref, gate_ref, o_ref,
                vt_ref, kp_ref, s_ref, m_ref, acc_ref):
    qi = pl.program_id(1)
    tq = q_ref.shape[0]
    assert tk == tq
    heads = range(HEADS)

    @pl.when(qi == 0)
    def _():
        v_t = v_ref[...].T
        for h in heads:
            vt_ref[h * VT_ROWS:h * VT_ROWS + HEAD_DIM] = v_t[h * HEAD_DIM:(h + 1) * HEAD_DIM]
            vt_ref[h * VT_ROWS + HEAD_DIM:(h + 1) * VT_ROWS] = jnp.ones((VT_ROWS - HEAD_DIM, v_t.shape[1]), BF16)
        lane = lax.broadcasted_iota(jnp.int32, (k_ref.shape[0], LANES), 1)
        kx = kx_ref[...].astype(F32)
        for p in range(HEADS // 2):
            pair = k_ref[:, p * LANES:(p + 1) * LANES].astype(F32)
            for h in (2 * p, 2 * p + 1):
                own = pair if h % 2 == 0 else pltpu.roll(pair, HEAD_DIM, 1)
                ext = pltpu.roll(kx, (HEAD_DIM - EXT_STRIDE * h) % LANES, 1)
                blk = jnp.where(lane < HEAD_DIM, own, jnp.where(lane < HEAD_DIM + EXT_STRIDE, ext, 0.0))
                kp_ref[:, h * LANES:(h + 1) * LANES] = blk.astype(BF16)

    qx_t = qx_ref[...].T
    q_t = []
    for h in heads:
        pair_t = q_ref[:, (h // 2) * LANES:(h // 2 + 1) * LANES].T
        q_t.append(jnp.concatenate(
            [pair_t[(h % 2) * HEAD_DIM:(h % 2 + 1) * HEAD_DIM], qx_t[EXT_STRIDE * h:EXT_STRIDE * (h + 1)],
             jnp.zeros((LANES - HEAD_DIM - EXT_STRIDE, tq), BF16)], axis=0))
    kv_pos = lax.broadcasted_iota(jnp.int32, (tk, tq), 0)
    q_pos = lax.broadcasted_iota(jnp.int32, (tk, tq), 1)

    def score(h, j, masked):
        k0 = pl.multiple_of(j * tk, tk)
        s_h = jnp.dot(kp_ref[pl.ds(k0, tk), h * LANES:(h + 1) * LANES], q_t[h], preferred_element_type=F32)
        if masked:
            s_h = jnp.where(kv_pos <= q_pos, s_h, NEG_BIG)
        s_ref[h] = s_h

    def absorb(h, j):
        k0 = pl.multiple_of(j * tk, tk)
        m_prev = m_ref[h]
        s_h = s_ref[h]
        m_new = jnp.maximum(m_prev, jnp.max(s_h, axis=0, keepdims=True))
        p = jnp.exp2(s_h - m_new).astype(BF16)
        pv = jnp.dot(vt_ref[h * VT_ROWS:(h + 1) * VT_ROWS, pl.ds(k0, tk)], p, preferred_element_type=F32)
        m_ref[h] = m_new
        acc_ref[h] = jnp.exp2(m_prev - m_new) * acc_ref[h] + pv

    for h in heads:
        m_ref[h] = jnp.full((1, tq), NEG_BIG, F32)
        acc_ref[h] = jnp.zeros((VT_ROWS, tq), F32)
        score(h, qi, True)

    def step(j, pending):
        for h in heads:
            absorb(h, pending)
            score(h, j, False)
        return j

    pending = lax.fori_loop(0, qi, step, qi)
    for p in range(HEADS // 2):
        lanes = slice(p * LANES, (p + 1) * LANES)
        absorb(2 * p, pending)
        absorb(2 * p + 1, pending)
        o_t = jnp.concatenate([acc_ref[h, 0:HEAD_DIM] / acc_ref[h, HEAD_DIM:HEAD_DIM + 1]
                               for h in (2 * p, 2 * p + 1)], axis=0)
        o_ref[:, lanes] = (o_t.T * gate_ref[:, lanes].astype(F32)).astype(BF16)


def _fox_call(fq, qx, fk, kx, fv, gate, *, seq, tq, tk):
    m = fq.shape[0]
    nq = seq // tq
    qblk = pl.BlockSpec((tq, WIDTH), lambda b, i: (b * nq + i, 0))
    kvblk = pl.BlockSpec((seq, WIDTH), lambda b, i: (b, 0))
    return pl.pallas_call(
        functools.partial(_fox_kernel, tk),
        grid=(m // seq, nq),
        in_specs=[qblk, pl.BlockSpec((tq, LANES), lambda b, i: (b * nq + i, 0)),
                  kvblk, pl.BlockSpec((seq, LANES), lambda b, i: (b, 0)), kvblk, qblk],
        out_specs=qblk,
        out_shape=jax.ShapeDtypeStruct((m, WIDTH), BF16),
        scratch_shapes=[pltpu.VMEM((HEADS * VT_ROWS, seq), BF16), pltpu.VMEM((seq, HEADS * LANES), BF16),
                        pltpu.VMEM((HEADS, tk, tq), F32),
                        pltpu.VMEM((HEADS, 1, tq), F32), pltpu.VMEM((HEADS, VT_ROWS, tq), F32)],
        compiler_params=pltpu.CompilerParams(dimension_semantics=("parallel", "arbitrary"),
                                             vmem_limit_bytes=VMEM_LIMIT),
    )(fq, qx, fk, kx, fv, gate)


FF_SPLITS = ((0, 1024), (1024, 1024), (2048, 768))


def _ffn_kernel(final, mg_ref, mf_ref, x_ref, wo_ref, n2_ref, wg_ref, wu_ref, wd_ref, fn_ref, o_ref):
    x1 = (x_ref[...]
          + jnp.dot(mg_ref[...], wo_ref[0:WIDTH, :], preferred_element_type=F32)
          + jnp.dot(mf_ref[...], wo_ref[WIDTH:2 * WIDTH, :], preferred_element_type=F32))
    h = (x1 * lax.rsqrt(jnp.mean(x1 * x1, axis=-1, keepdims=True) + EPS) * n2_ref[...]).astype(BF16)
    ffn = None
    for start, size in FF_SPLITS:
        gate = jnp.dot(h, wg_ref[:, start:start + size], preferred_element_type=F32)
        up = jnp.dot(h, wu_ref[:, start:start + size], preferred_element_type=F32)
        act = (gate * _sigmoid(gate) * up).astype(BF16)
        down = jnp.dot(act, wd_ref[start:start + size, :], preferred_element_type=F32)
        ffn = down if ffn is None else ffn + down
    y = x1 + ffn
    if final:
        y = y * lax.rsqrt(jnp.mean(y * y, axis=-1, keepdims=True) + EPS) * fn_ref[...]
    o_ref[...] = y


def _ffn_call(mix_g, mix_f, x2, wo, n2w, wg, wu, wd, fnw, *, tm, final):
    m = x2.shape[0]
    half = pl.BlockSpec((tm, WIDTH), lambda i: (i, 0))
    full = pl.BlockSpec((tm, D_MODEL), lambda i: (i, 0))

    def resident(shape):
        return pl.BlockSpec(shape, lambda i: (0, 0), pipeline_mode=pl.Buffered(1))

    return pl.pallas_call(
        functools.partial(_ffn_kernel, final),
        grid=(m // tm,),
        in_specs=[half, half, full, resident(wo.shape), resident(n2w.shape), resident(wg.shape),
                  resident(wu.shape), resident(wd.shape), resident(fnw.shape)],
        out_specs=full,
        out_shape=jax.ShapeDtypeStruct((m, D_MODEL), F32),
        compiler_params=pltpu.CompilerParams(dimension_semantics=("parallel",),
                                             vmem_limit_bytes=VMEM_LIMIT),
    )(mix_g, mix_f, x2, wo, n2w, wg, wu, wd, fnw)


def _lane_vec(parts):
    v = jnp.concatenate([p.astype(F32).reshape(-1) for p in parts])
    return jnp.pad(v, (0, N_SMALL - v.shape[0])).reshape(1, N_SMALL)


def kernel(x, norm1_w, w_in, gdn_conv_w, gdn_A_log, gdn_dt_bias, gdn_out_norm_w, fox_f_bias, fox_q_norm_w, fox_k_norm_w, w_out, norm2_w, w_ffn_gate, w_ffn_up, w_ffn_down, final_norm_w):
    batch, seq, _ = x.shape
    m = batch * seq
    depth = norm1_w.shape[0]
    zeros8 = jnp.zeros((HEADS,), F32)
    r2 = jnp.arange(MXU_DIM) // HEAD_DIM
    ones_bd = (r2[:, None] == r2[None, :]).astype(BF16)

    x2 = x.reshape(m, D_MODEL)
    for l in range(depth):
        w = w_in[l].astype(BF16)
        big = 4 * WIDTH
        g_small = w[:, big:big + 2 * HEADS]
        f_start = big + 2 * HEADS
        f_small = w[:, f_start + big:f_start + big + HEADS]
        w_gdn = w[:, :big]
        w_fox = w[:, f_start:f_start + big]
        w_gates = jnp.concatenate(
            [g_small, f_small, jnp.zeros((D_MODEL, N_SMALL - 3 * HEADS), BF16)], axis=1)
        bias_vec = _lane_vec([zeros8, gdn_dt_bias[l], fox_f_bias[l]])
        alog_vec = _lane_vec([zeros8, gdn_A_log[l], zeros8])
        fqw = (jnp.tile(fox_q_norm_w[l].astype(F32), HEADS) * (HEAD_DIM ** -0.5 * LOG2E)).reshape(1, WIDTH)
        fkw = jnp.tile(fox_k_norm_w[l].astype(F32), HEADS).reshape(1, WIDTH)

        gq, gk, gv, gz, fq, fk, fv, fg, small, kx, qx, g_t = _inproj_call(
            x2, norm1_w[l].reshape(1, D_MODEL), w_gdn, w_fox, w_gates, gdn_conv_w[l].astype(F32), bias_vec, alog_vec,
            fqw, fkw, ones_bd, seq=seq, tm=512)

        u, wmat, amat, qg, kd, dl = _gdn_wy_call(gq, gk, gv, small, g_t, rows=512)
        nw = jnp.tile(gdn_out_norm_w[l].astype(F32), HEADS).reshape(1, WIDTH)
        mix_g, (wo, wg, wu, wd) = _gdn_scan_call(
            u, wmat, amat, qg, kd, dl, gz, nw, ones_bd,
            [w_out[l], w_ffn_gate[l], w_ffn_up[l], w_ffn_down[l]],
            seq=seq, n_seq=math.gcd(batch, SCAN_SEQS), rows=256)
        mix_f = _fox_call(fq, qx, fk, kx, fv, fg, seq=seq, tq=FOX_TQ, tk=FOX_TK)

        x2 = _ffn_call(mix_g, mix_f, x2, wo, norm2_w[l].reshape(1, D_MODEL), wg, wu, wd,
                       final_norm_w.reshape(1, D_MODEL), tm=512, final=(l == depth - 1))
    return x2.reshape(batch, seq, D_MODEL)
```

```python
import functools
import math

import numpy as np
import jax
import jax.numpy as jnp
from jax import lax
from jax.experimental import pallas as pl
from jax.experimental.pallas import tpu as pltpu

D_MODEL = 1024
HEADS = 8
HEAD_DIM = 64
WIDTH = HEADS * HEAD_DIM
CONV_K = 4
CHUNK = 64
D_FF = 2816
EPS = 1e-6

LANES = 128
MXU_DIM = 256
PACK = MXU_DIM // HEAD_DIM
N_SMALL = LANES
VMEM_LIMIT = 56 * 1024 * 1024

F32 = jnp.float32
BF16 = jnp.bfloat16
NEG_BIG = -1e30
LOG2E = 1.4426950408889634
EXT_STRIDE = 16


def _const_spec(shape):
    nd = len(shape)
    return pl.BlockSpec(shape, lambda *_: (0,) * nd)


def _sigmoid(x):
    return 1.0 / (1.0 + jnp.exp(-x))


def _softplus(x):
    return jnp.maximum(x, 0.0) + jnp.log1p(jnp.exp(-jnp.abs(x)))


def _head_sums(y2, ones_bd):
    yb = y2.astype(BF16)
    parts = [jnp.dot(yb[:, c * MXU_DIM:(c + 1) * MXU_DIM], ones_bd, preferred_element_type=F32)
             for c in range(WIDTH // MXU_DIM)]
    return jnp.concatenate(parts, axis=1)


def _seg_cumsum(v, seg):
    pos = lax.broadcasted_iota(jnp.int32, v.shape, 0) & (seg - 1)
    d = 1
    while d < seg:
        v = v + jnp.where(pos >= d, pltpu.roll(v, d, 0), 0.0)
        d *= 2
    return v


def _inproj_kernel(tiles_per_seq, x_ref, n1_ref, wg_ref, wf_ref, ws_ref, cw_ref, bias_ref, alog_ref, fqw_ref, fkw_ref,
                   ones_ref, sel_ref, one_ref,
                   gq_ref, gk_ref, gv_ref, gz_ref, fq_ref, fk_ref, fv_ref, fg_ref, sm_ref, kx_ref, qx_ref, gt_ref,
                   tail_ref, carry_ref):
    i = pl.program_id(0)
    tm = x_ref.shape[0]

    @pl.when(i % tiles_per_seq == 0)
    def _():
        tail_ref[:, 0:8, :] = jnp.zeros((3, 8, WIDTH), F32)
        carry_ref[...] = jnp.zeros_like(carry_ref)

    x = x_ref[...]
    h = (x * lax.rsqrt(jnp.mean(x * x, axis=-1, keepdims=True) + EPS) * n1_ref[...]).astype(BF16)
    ones_bd = ones_ref[...]

    y_gdn = jnp.dot(h, wg_ref[...], preferred_element_type=F32)
    y_fox = jnp.dot(h, wf_ref[...], preferred_element_type=F32)
    ys = ([y_gdn[:, s * WIDTH:(s + 1) * WIDTH] for s in range(4)]
          + [y_fox[:, s * WIDTH:(s + 1) * WIDTH] for s in range(4)]
          + [jnp.dot(h, ws_ref[...], preferred_element_type=F32)])

    for s, out_ref in enumerate((gq_ref, gk_ref, gv_ref)):
        y = ys[s]
        cw = cw_ref[:, s * WIDTH:(s + 1) * WIDTH]
        tail_ref[s, 8:8 + tm] = y
        acc = y * cw[CONV_K - 1:CONV_K]
        for k in range(1, CONV_K):
            acc = acc + tail_ref[s, 8 - k:8 - k + tm] * cw[CONV_K - 1 - k:CONV_K - k]
        tail_ref[s, 0:8] = y[tm - 8:]
        act = acc * _sigmoid(acc)
        if s < 2:
            act = act * lax.rsqrt(_head_sums(act * act, ones_bd) + EPS)
        out_ref[...] = act.astype(BF16)

    z = ys[3]
    gz_ref[...] = (z * _sigmoid(z)).astype(BF16)

    for s, out_ref, w_norm in ((4, fq_ref, fqw_ref), (5, fk_ref, fkw_ref)):
        y = ys[s]
        ms = _head_sums(y * y, ones_bd) * (1.0 / HEAD_DIM)
        out_ref[...] = (y * lax.rsqrt(ms + EPS) * w_norm[...]).astype(BF16)

    fv_ref[...] = ys[6].astype(BF16)
    fg_ref[...] = _sigmoid(ys[7]).astype(BF16)

    t = ys[8] + bias_ref[...]
    lane = lax.broadcasted_iota(jnp.int32, t.shape, 1)
    beta = _sigmoid(t)
    g_log = -jnp.exp(alog_ref[...]) * _softplus(t)
    log_f = -_softplus(-t)
    g_cum = _seg_cumsum(g_log, CHUNK)
    f_cum = _seg_cumsum(log_f, tm) + carry_ref[0:1, :]
    carry_ref[0:1, :] = f_cum[tm - 1:tm, :]
    sm_ref[...] = jnp.where(lane < HEADS, beta, jnp.where(lane < 2 * HEADS, g_cum, f_cum))
    gt_ref[...] = g_cum.T[HEADS:2 * HEADS, :]

    f2 = f_cum * LOG2E
    hi = f2.astype(BF16)
    r1 = f2 - hi.astype(F32)
    mid = r1.astype(BF16)
    lo = (r1 - mid.astype(F32)).astype(BF16)
    ext = jnp.dot(jnp.concatenate([hi, mid, lo], axis=1), sel_ref[...], preferred_element_type=F32) + one_ref[...]
    kx_ref[...] = ext[:, :LANES].astype(BF16)
    qx_ref[...] = ext[:, LANES:].astype(BF16)


def _bias_selectors():
    sel = np.zeros((3 * LANES, 2 * LANES), np.float32)
    one = np.zeros((1, 2 * LANES), np.float32)
    for h in range(HEADS):
        src = 2 * HEADS + h
        for part in range(3):
            sel[part * LANES + src, EXT_STRIDE * h + part] = -1.0
            sel[part * LANES + src, LANES + EXT_STRIDE * h + 3 + part] = 1.0
            one[0, EXT_STRIDE * h + 3 + part] = 1.0
            one[0, LANES + EXT_STRIDE * h + part] = 1.0
    return jnp.asarray(sel, BF16), jnp.asarray(one, F32)


def _inproj_call(x2, n1w, w_gdn, w_fox, w_gates, conv_w, bias_vec, alog_vec, fqw, fkw, ones_bd, *, seq, tm):
    m = x2.shape[0]
    assert seq % tm == 0 and tm % CHUNK == 0 and (tm & (tm - 1)) == 0
    sel, one = _bias_selectors()
    wide = pl.BlockSpec((tm, WIDTH), lambda i: (i, 0))
    narrow = pl.BlockSpec((tm, LANES), lambda i: (i, 0))
    out_shape = ([jax.ShapeDtypeStruct((m, WIDTH), BF16)] * 8 + [jax.ShapeDtypeStruct((m, N_SMALL), F32)]
                 + [jax.ShapeDtypeStruct((m, LANES), BF16)] * 2 + [jax.ShapeDtypeStruct((HEADS, m), F32)])
    return pl.pallas_call(
        functools.partial(_inproj_kernel, seq // tm),
        grid=(m // tm,),
        in_specs=[pl.BlockSpec((tm, D_MODEL), lambda i: (i, 0)),
                  _const_spec(n1w.shape), _const_spec(w_gdn.shape), _const_spec(w_fox.shape),
                  _const_spec(w_gates.shape), _const_spec(conv_w.shape),
                  _const_spec(bias_vec.shape), _const_spec(alog_vec.shape),
                  _const_spec(fqw.shape), _const_spec(fkw.shape), _const_spec(ones_bd.shape),
                  _const_spec(sel.shape), _const_spec(one.shape)],
        out_specs=[wide] * 8 + [narrow] * 3 + [pl.BlockSpec((HEADS, tm), lambda i: (0, i))],
        out_shape=out_shape,
        scratch_shapes=[pltpu.VMEM((3, tm + 8, WIDTH), F32), pltpu.VMEM((8, N_SMALL), F32)],
        compiler_params=pltpu.CompilerParams(dimension_semantics=("arbitrary",),
                                             vmem_limit_bytes=VMEM_LIMIT),
    )(x2, n1w, w_gdn, w_fox, w_gates, conv_w, bias_vec, alog_vec, fqw, fkw, ones_bd, sel, one)


def _lane_lo_mask():
    return lax.broadcasted_iota(jnp.int32, (CHUNK, LANES), 1) < HEAD_DIM


def _block_diag(x, lane_lo):
    xb = x.astype(BF16)
    zero = jnp.zeros((CHUNK, LANES), BF16)
    blocks = []
    for h in range(PACK):
        half = xb[:, (h // 2) * LANES:(h // 2 + 1) * LANES]
        keep = jnp.where(lane_lo if h % 2 == 0 else jnp.logical_not(lane_lo), half, zero)
        blocks.append(jnp.concatenate([keep, zero] if h < 2 else [zero, keep], axis=1))
    return jnp.concatenate(blocks, axis=0)


def _mm(lhs, rhs_bd):
    return jnp.dot(lhs.astype(BF16), rhs_bd, preferred_element_type=F32)


def _expand_heads(sm, first_lane):
    lane_lo = lax.broadcasted_iota(jnp.int32, (sm.shape[0], LANES), 1) < HEAD_DIM
    parts = []
    for p in range(HEADS // 2):
        c = first_lane + 2 * p
        even = jnp.broadcast_to(sm[:, c:c + 1], (sm.shape[0], LANES))
        odd = jnp.broadcast_to(sm[:, c + 1:c + 2], (sm.shape[0], LANES))
        parts.append(jnp.where(lane_lo, even, odd))
    return jnp.concatenate(parts, axis=1)


def _gdn_wy_kernel(q_ref, k_ref, v_ref, sm_ref, gt_ref,
                   u_ref, w_ref, a_ref, qg_ref, kd_ref, dl_ref):
    n_chunks = q_ref.shape[0] // CHUNK
    n_groups = WIDTH // MXU_DIM
    shape = (CHUNK, MXU_DIM)
    row = lax.broadcasted_iota(jnp.int32, shape, 0)
    col = lax.broadcasted_iota(jnp.int32, shape, 1) & (HEAD_DIM - 1)
    causal = row >= col
    strict = row > col
    eye = (row == col).astype(F32)
    bd_mask = _lane_lo_mask()
    scale = HEAD_DIM ** -0.5

    units = [(slice(c * CHUNK, (c + 1) * CHUNK), slice(g * MXU_DIM, (g + 1) * MXU_DIM), c)
             for c in range(n_chunks) for g in range(n_groups)]
    cs = range(len(units))
    bd = lambda xs: [_block_diag(x, bd_mask) for x in xs]
    mm = lambda ls, rs_: [_mm(a, b) for a, b in zip(ls, rs_)]
    stack = lambda xs, ys: [jnp.concatenate([x, y], axis=0) for x, y in zip(xs, ys)]

    sm = sm_ref[...]
    bx_all = _expand_heads(sm, 0)
    gx_all = _expand_heads(sm, HEADS)
    q = [q_ref[r, l].astype(F32) for r, l, _ in units]
    k = [k_ref[r, l].astype(F32) for r, l, _ in units]
    bx = [bx_all[r, l] for r, l, _ in units]
    gx = [gx_all[r, l] for r, l, _ in units]
    kb = [k[c] * bx[c] for c in cs]
    def key_row(ch, g):
        return jnp.concatenate([gt_ref[g * PACK + h:g * PACK + h + 1, ch * CHUNK:(ch + 1) * CHUNK]
                                for h in range(PACK)], axis=1)

    decay = [jnp.exp(jnp.where(causal, gx[c] - key_row(ch, l.start // MXU_DIM), NEG_BIG))
             for c, (_, l, ch) in enumerate(units)]

    bd_k = bd(k)
    s1 = [lax.dot_general(jnp.concatenate([kb[c], q[c]], axis=0).astype(BF16), bd_k[c],
                          (((1,), (1,)), ((), ())), preferred_element_type=F32) for c in cs]
    lmat = [jnp.where(strict, s1[c][:CHUNK] * decay[c], 0.0) for c in cs]
    for c, (r, l, _) in enumerate(units):
        a_ref[r, l] = (s1[c][CHUNK:] * decay[c] * scale).astype(BF16)

    blk8 = (row >> 3) == (col >> 3)
    n8 = [jnp.where(blk8, -l, 0.0) for l in lmat]
    t0 = [eye + n for n in n8]
    p1 = mm(n8, bd(n8))
    r = mm(stack(p1, t0), bd(p1))
    ta = [t0[c] + r[c][CHUNK:] for c in cs]
    tinv = [ta[c] + z for c, z in enumerate(mm(ta, bd([x[:CHUNK] for x in r])))]
    for ls in (3, 4, 5):
        off = ((row >> (ls + 1)) == (col >> (ls + 1))) & ((row >> ls) == (col >> ls) + 1)
        y = mm([jnp.where(off, l, 0.0) for l in lmat], bd(tinv))
        tinv = [tinv[c] - z for c, z in enumerate(mm(tinv, bd(y)))]

    eg = [jnp.exp(g) for g in gx]
    u = mm(tinv, bd([v_ref[r, l].astype(F32) * bx[c] for c, (r, l, _) in enumerate(units)]))
    w = mm(tinv, bd([kb[c] * eg[c] for c in cs]))
    for c, (r, l, ch) in enumerate(units):
        u_ref[r, l] = u[c].astype(BF16)
        w_ref[r, l] = w[c].astype(BF16)
        qg_ref[r, l] = (q[c] * eg[c] * scale).astype(BF16)
        g_last = gx[c][CHUNK - 1:CHUNK, :]
        kd_ref[r, l] = (k[c] * jnp.exp(g_last - gx[c])).astype(BF16)
        dl_ref[ch, :, l] = jnp.exp(g_last)


def _gdn_wy_call(gq, gk, gv, small, g_t, *, rows):
    m = gq.shape[0]
    cpb = rows // CHUNK
    blk = pl.BlockSpec((rows, WIDTH), lambda i: (i, 0))
    rowblk = pl.BlockSpec((cpb, 1, WIDTH), lambda i: (i, 0, 0))
    bf = jax.ShapeDtypeStruct((m, WIDTH), BF16)
    return pl.pallas_call(
        _gdn_wy_kernel,
        grid=(m // rows,),
        in_specs=[blk] * 3 + [pl.BlockSpec((rows, N_SMALL), lambda i: (i, 0)),
                              pl.BlockSpec((HEADS, rows), lambda i: (0, i))],
        out_specs=[blk] * 5 + [rowblk],
        out_shape=[bf] * 5 + [jax.ShapeDtypeStruct((m // CHUNK, 1, WIDTH), F32)],
        compiler_params=pltpu.CompilerParams(dimension_semantics=("parallel",),
                                             vmem_limit_bytes=VMEM_LIMIT),
    )(gq, gk, gv, small, g_t)


SCAN_SEQS = 8


def _gdn_scan_kernel(n_cast, u_ref, w_ref, a_ref, qg_ref, kd_ref, dl_ref, gz_ref, nw_ref, ones_ref, *refs):
    cast_in, o_ref, cast_out = refs[:n_cast], refs[n_cast], refs[n_cast + 1:2 * n_cast + 1]
    s_ref, oacc_ref = refs[2 * n_cast + 1:]
    for src, dst in zip(cast_in, cast_out):
        dst[...] = src[...].astype(BF16)
    n_seq, rows, _ = u_ref.shape
    n_groups = WIDTH // MXU_DIM
    chains = [(b, g, slice(g * MXU_DIM, (g + 1) * MXU_DIM)) for b in range(n_seq) for g in range(n_groups)]
    lane_lo = _lane_lo_mask()
    lane_hi = jnp.logical_not(lane_lo)
    zero_half = jnp.zeros((HEAD_DIM, LANES), BF16)

    @pl.when(pl.program_id(1) == 0)
    def _():
        s_ref[...] = jnp.zeros_like(s_ref)

    def pair_lanes(h):
        return slice((h // 2) * LANES, (h // 2 + 1) * LANES)

    def as_block_diag(pieces):
        rows_ = [jnp.concatenate([p.astype(BF16), zero_half] if h < 2 else [zero_half, p.astype(BF16)], axis=1)
                 for h, p in enumerate(pieces)]
        return jnp.concatenate(rows_, axis=0)

    def body(c, carry):
        rs = pl.ds(pl.multiple_of(c * CHUNK, CHUNK), CHUNK)
        state = [[s_ref[b, g, h] for h in range(PACK)] for b, g, _ in chains]
        r = [jnp.dot(jnp.concatenate([w_ref[b, rs, l], qg_ref[b, rs, l]], axis=0),
                     as_block_diag(state[i]), preferred_element_type=F32) for i, (b, _, l) in enumerate(chains)]
        v_new = [u_ref[b, rs, l].astype(F32) - r[i][:CHUNK] for i, (b, _, l) in enumerate(chains)]
        upd = [lax.dot_general(kd_ref[b, rs, l], v_new[i].astype(BF16), (((0,), (0,)), ((), ())),
                               preferred_element_type=F32) for i, (b, _, l) in enumerate(chains)]
        for i, (b, g, l) in enumerate(chains):
            decay = dl_ref[b, c, :, l]
            for h in range(PACK):
                own = upd[i][h * HEAD_DIM:(h + 1) * HEAD_DIM, pair_lanes(h)]
                s_ref[b, g, h] = (state[i][h] * decay[:, pair_lanes(h)]
                                  + jnp.where(lane_lo if h % 2 == 0 else lane_hi, own, 0.0))
        for i, (b, _, l) in enumerate(chains):
            oacc_ref[b, rs, l] = r[i][CHUNK:] + jnp.dot(a_ref[b, rs, l], _block_diag(v_new[i], lane_lo),
                                                        preferred_element_type=F32)
        return carry

    lax.fori_loop(0, rows // CHUNK, body, 0)

    for b in range(n_seq):
        o = oacc_ref[b]
        ms = _head_sums(o * o, ones_ref[...]) * (1.0 / HEAD_DIM)
        o_ref[b] = (o * lax.rsqrt(ms + EPS) * nw_ref[...] * gz_ref[b].astype(F32)).astype(BF16)


def _gdn_scan_call(u, w, a, qg, kd, dl, gz, nw, ones_bd, weights_f32, *, seq, n_seq, rows):
    batch = u.shape[0] // seq
    steps = seq // rows
    as3d = lambda t: t.reshape(batch, seq, WIDTH)
    blk = pl.BlockSpec((n_seq, rows, WIDTH), lambda b, t: (b, t, 0))
    dl4 = dl.reshape(batch, seq // CHUNK, 1, WIDTH)
    slabs = [pl.BlockSpec((wt.shape[0] // steps, wt.shape[1]), lambda b, t: (t, 0)) for wt in weights_f32]
    assert all(wt.shape[0] % (16 * steps) == 0 for wt in weights_f32)
    outs = pl.pallas_call(
        functools.partial(_gdn_scan_kernel, len(weights_f32)),
        grid=(batch // n_seq, steps),
        in_specs=[blk] * 5 + [pl.BlockSpec((n_seq, rows // CHUNK, 1, WIDTH), lambda b, t: (b, t, 0, 0)), blk,
                              _const_spec(nw.shape), _const_spec(ones_bd.shape)] + slabs,
        out_specs=[blk] + slabs,
        out_shape=[jax.ShapeDtypeStruct((batch, seq, WIDTH), BF16)]
                  + [jax.ShapeDtypeStruct(wt.shape, BF16) for wt in weights_f32],
        scratch_shapes=[pltpu.VMEM((n_seq, WIDTH // MXU_DIM, PACK, HEAD_DIM, LANES), F32),
                        pltpu.VMEM((n_seq, rows, WIDTH), F32)],
        compiler_params=pltpu.CompilerParams(dimension_semantics=("arbitrary", "arbitrary"),
                                             vmem_limit_bytes=VMEM_LIMIT),
    )(as3d(u), as3d(w), as3d(a), as3d(qg), as3d(kd), dl4, as3d(gz), nw, ones_bd, *weights_f32)
    return outs[0].reshape(batch * seq, WIDTH), outs[1:]


FOX_TQ = 256
FOX_TK = 256
VT_ROWS = HEAD_DIM + 16


def _fox_kernel(tk, q_ref, qx_ref, k_ref, kx_ref, v_ref, gate_ref, o_ref,
                vt_ref, kp_ref, s_ref, m_ref, acc_ref):
    qi = pl.program_id(1)
    tq = q_ref.shape[0]
    assert tk == tq
    heads = range(HEADS)

    @pl.when(qi == 0)
    def _():
        v_t = v_ref[...].T
        for h in heads:
            vt_ref[h * VT_ROWS:h * VT_ROWS + HEAD_DIM] = v_t[h * HEAD_DIM:(h + 1) * HEAD_DIM]
            vt_ref[h * VT_ROWS + HEAD_DIM:(h + 1) * VT_ROWS] = jnp.ones((VT_ROWS - HEAD_DIM, v_t.shape[1]), BF16)
        lane = lax.broadcasted_iota(jnp.int32, (k_ref.shape[0], LANES), 1)
        kx = kx_ref[...].astype(F32)
        for p in range(HEADS // 2):
            pair = k_ref[:, p * LANES:(p + 1) * LANES].astype(F32)
            for h in (2 * p, 2 * p + 1):
                own = pair if h % 2 == 0 else pltpu.roll(pair, HEAD_DIM, 1)
                ext = pltpu.roll(kx, (HEAD_DIM - EXT_STRIDE * h) % LANES, 1)
                blk = jnp.where(lane < HEAD_DIM, own, jnp.where(lane < HEAD_DIM + EXT_STRIDE, ext, 0.0))
                kp_ref[:, h * LANES:(h + 1) * LANES] = blk.astype(BF16)

    qx_t = qx_ref[...].T
    q_t = []
    for h in heads:
        pair_t = q_ref[:, (h // 2) * LANES:(h // 2 + 1) * LANES].T
        q_t.append(jnp.concatenate(
            [pair_t[(h % 2) * HEAD_DIM:(h % 2 + 1) * HEAD_DIM], qx_t[EXT_STRIDE * h:EXT_STRIDE * (h + 1)],
             jnp.zeros((LANES - HEAD_DIM - EXT_STRIDE, tq), BF16)], axis=0))
    kv_pos = lax.broadcasted_iota(jnp.int32, (tk, tq), 0)
    q_pos = lax.broadcasted_iota(jnp.int32, (tk, tq), 1)

    def score(h, j, masked):
        k0 = pl.multiple_of(j * tk, tk)
        s_h = jnp.dot(kp_ref[pl.ds(k0, tk), h * LANES:(h + 1) * LANES], q_t[h], preferred_element_type=F32)
        if masked:
            s_h = jnp.where(kv_pos <= q_pos, s_h, NEG_BIG)
        s_ref[h] = s_h

    def absorb(h, j):
        k0 = pl.multiple_of(j * tk, tk)
        m_prev = m_ref[h]
        s_h = s_ref[h]
        m_new = jnp.maximum(m_prev, jnp.max(s_h, axis=0, keepdims=True))
        p = jnp.exp2(s_h - m_new).astype(BF16)
        pv = jnp.dot(vt_ref[h * VT_ROWS:(h + 1) * VT_ROWS, pl.ds(k0, tk)], p, preferred_element_type=F32)
        m_ref[h] = m_new
        acc_ref[h] = jnp.exp2(m_prev - m_new) * acc_ref[h] + pv

    for h in heads:
        m_ref[h] = jnp.full((1, tq), NEG_BIG, F32)
        acc_ref[h] = jnp.zeros((VT_ROWS, tq), F32)
        score(h, qi, True)

    def step(j, pending):
        for h in heads:
            absorb(h, pending)
            score(h, j, False)
        return j

    pending = lax.fori_loop(0, qi, step, qi)
    for p in range(HEADS // 2):
        lanes = slice(p * LANES, (p + 1) * LANES)
        absorb(2 * p, pending)
        absorb(2 * p + 1, pending)
        o_t = jnp.concatenate([acc_ref[h, 0:HEAD_DIM] / acc_ref[h, HEAD_DIM:HEAD_DIM + 1]
                               for h in (2 * p, 2 * p + 1)], axis=0)
        o_ref[:, lanes] = (o_t.T * gate_ref[:, lanes].astype(F32)).astype(BF16)


def _fox_call(fq, qx, fk, kx, fv, gate, *, seq, tq, tk):
    m = fq.shape[0]
    nq = seq // tq
    qblk = pl.BlockSpec((tq, WIDTH), lambda b, i: (b * nq + i, 0))
    kvblk = pl.BlockSpec((seq, WIDTH), lambda b, i: (b, 0))
    return pl.pallas_call(
        functools.partial(_fox_kernel, tk),
        grid=(m // seq, nq),
        in_specs=[qblk, pl.BlockSpec((tq, LANES), lambda b, i: (b * nq + i, 0)),
                  kvblk, pl.BlockSpec((seq, LANES), lambda b, i: (b, 0)), kvblk, qblk],
        out_specs=qblk,
        out_shape=jax.ShapeDtypeStruct((m, WIDTH), BF16),
        scratch_shapes=[pltpu.VMEM((HEADS * VT_ROWS, seq), BF16), pltpu.VMEM((seq, HEADS * LANES), BF16),
                        pltpu.VMEM((HEADS, tk, tq), F32),
                        pltpu.VMEM((HEADS, 1, tq), F32), pltpu.VMEM((HEADS, VT_ROWS, tq), F32)],
        compiler_params=pltpu.CompilerParams(dimension_semantics=("parallel", "arbitrary"),
                                             vmem_limit_bytes=VMEM_LIMIT),
    )(fq, qx, fk, kx, fv, gate)


FF_SPLITS = ((0, 1024), (1024, 1024), (2048, 768))


FFN_ROW_GROUPS = 2


def _ffn_kernel(final, mg_ref, mf_ref, x_ref, wo_ref, n2_ref, wg_ref, wu_ref, wd_ref, fn_ref, o_ref):
    rows = x_ref.shape[0] // FFN_ROW_GROUPS
    groups = [slice(r * rows, (r + 1) * rows) for r in range(FFN_ROW_GROUPS)]
    x1 = [x_ref[g, :]
          + jnp.dot(mg_ref[g, :], wo_ref[0:WIDTH, :], preferred_element_type=F32)
          + jnp.dot(mf_ref[g, :], wo_ref[WIDTH:2 * WIDTH, :], preferred_element_type=F32) for g in groups]
    h = [(v * lax.rsqrt(jnp.mean(v * v, axis=-1, keepdims=True) + EPS) * n2_ref[...]).astype(BF16) for v in x1]
    ffn = [None] * len(groups)
    for start, size in FF_SPLITS:
        gate = [jnp.dot(hv, wg_ref[:, start:start + size], preferred_element_type=F32) for hv in h]
        up = [jnp.dot(hv, wu_ref[:, start:start + size], preferred_element_type=F32) for hv in h]
        act = [(gv * _sigmoid(gv) * uv).astype(BF16) for gv, uv in zip(gate, up)]
        down = [jnp.dot(av, wd_ref[start:start + size, :], preferred_element_type=F32) for av in act]
        ffn = [d if f is None else f + d for f, d in zip(ffn, down)]
    for g, v, f in zip(groups, x1, ffn):
        y = v + f
        if final:
            y = y * lax.rsqrt(jnp.mean(y * y, axis=-1, keepdims=True) + EPS) * fn_ref[...]
        o_ref[g, :] = y


def _ffn_call(mix_g, mix_f, x2, wo, n2w, wg, wu, wd, fnw, *, tm, final):
    m = x2.shape[0]
    half = pl.BlockSpec((tm, WIDTH), lambda i: (i, 0))
    full = pl.BlockSpec((tm, D_MODEL), lambda i: (i, 0))

    def resident(shape):
        return pl.BlockSpec(shape, lambda i: (0, 0), pipeline_mode=pl.Buffered(1))

    return pl.pallas_call(
        functools.partial(_ffn_kernel, final),
        grid=(m // tm,),
        in_specs=[half, half, full, resident(wo.shape), resident(n2w.shape), resident(wg.shape),
                  resident(wu.shape), resident(wd.shape), resident(fnw.shape)],
        out_specs=full,
        out_shape=jax.ShapeDtypeStruct((m, D_MODEL), F32),
        compiler_params=pltpu.CompilerParams(dimension_semantics=("parallel",),
                                             vmem_limit_bytes=VMEM_LIMIT),
    )(mix_g, mix_f, x2, wo, n2w, wg, wu, wd, fnw)


def _lane_vec(parts):
    v = jnp.concatenate([p.astype(F32).reshape(-1) for p in parts])
    return jnp.pad(v, (0, N_SMALL - v.shape[0])).reshape(1, N_SMALL)


def kernel(x, norm1_w, w_in, gdn_conv_w, gdn_A_log, gdn_dt_bias, gdn_out_norm_w, fox_f_bias, fox_q_norm_w, fox_k_norm_w, w_out, norm2_w, w_ffn_gate, w_ffn_up, w_ffn_down, final_norm_w):
    batch, seq, _ = x.shape
    m = batch * seq
    depth = norm1_w.shape[0]
    zeros8 = jnp.zeros((HEADS,), F32)
    r2 = jnp.arange(MXU_DIM) // HEAD_DIM
    ones_bd = (r2[:, None] == r2[None, :]).astype(BF16)

    x2 = x.reshape(m, D_MODEL)
    for l in range(depth):
        w = w_in[l].astype(BF16)
        big = 4 * WIDTH
        g_small = w[:, big:big + 2 * HEADS]
        f_start = big + 2 * HEADS
        f_small = w[:, f_start + big:f_start + big + HEADS]
        w_gdn = w[:, :big]
        w_fox = w[:, f_start:f_start + big]
        w_gates = jnp.concatenate(
            [g_small, f_small, jnp.zeros((D_MODEL, N_SMALL - 3 * HEADS), BF16)], axis=1)
        bias_vec = _lane_vec([zeros8, gdn_dt_bias[l], fox_f_bias[l]])
        alog_vec = _lane_vec([zeros8, gdn_A_log[l], zeros8])
        fqw = (jnp.tile(fox_q_norm_w[l].astype(F32), HEADS) * (HEAD_DIM ** -0.5 * LOG2E)).reshape(1, WIDTH)
        fkw = jnp.tile(fox_k_norm_w[l].astype(F32), HEADS).reshape(1, WIDTH)

        gq, gk, gv, gz, fq, fk, fv, fg, small, kx, qx, g_t = _inproj_call(
            x2, norm1_w[l].reshape(1, D_MODEL), w_gdn, w_fox, w_gates, gdn_conv_w[l].astype(F32), bias_vec, alog_vec,
            fqw, fkw, ones_bd, seq=seq, tm=512)

        u, wmat, amat, qg, kd, dl = _gdn_wy_call(gq, gk, gv, small, g_t, rows=512)
        nw = jnp.tile(gdn_out_norm_w[l].astype(F32), HEADS).reshape(1, WIDTH)
        mix_g, (wo, wg, wu, wd) = _gdn_scan_call(
            u, wmat, amat, qg, kd, dl, gz, nw, ones_bd,
            [w_out[l], w_ffn_gate[l], w_ffn_up[l], w_ffn_down[l]],
            seq=seq, n_seq=math.gcd(batch, SCAN_SEQS), rows=256)
        mix_f = _fox_call(fq, qx, fk, kx, fv, fg, seq=seq, tq=FOX_TQ, tk=FOX_TK)

        x2 = _ffn_call(mix_g, mix_f, x2, wo, norm2_w[l].reshape(1, D_MODEL), wg, wu, wd,
                       final_norm_w.reshape(1, D_MODEL), tm=512, final=(l == depth - 1))
    return x2.reshape(batch, seq, D_MODEL)
```

```python
import functools
import math

import numpy as np
import jax
import jax.numpy as jnp
from jax import lax
from jax.experimental import pallas as pl
from jax.experimental.pallas import tpu as pltpu

D_MODEL = 1024
HEADS = 8
HEAD_DIM = 64
WIDTH = HEADS * HEAD_DIM
CONV_K = 4
CHUNK = 64
D_FF = 2816
EPS = 1e-6

LANES = 128
MXU_DIM = 256
PACK = MXU_DIM // HEAD_DIM
N_SMALL = LANES
VMEM_LIMIT = 56 * 1024 * 1024

F32 = jnp.float32
BF16 = jnp.bfloat16
NEG_BIG = -1e30
LOG2E = 1.4426950408889634
EXT_STRIDE = 16


def _const_spec(shape):
    nd = len(shape)
    return pl.BlockSpec(shape, lambda *_: (0,) * nd)


def _sigmoid(x):
    return 1.0 / (1.0 + jnp.exp(-x))


def _softplus(x):
    return jnp.maximum(x, 0.0) + jnp.log1p(jnp.exp(-jnp.abs(x)))


def _head_sums(y2, ones_bd):
    yb = y2.astype(BF16)
    parts = [jnp.dot(yb[:, c * MXU_DIM:(c + 1) * MXU_DIM], ones_bd, preferred_element_type=F32)
             for c in range(WIDTH // MXU_DIM)]
    return jnp.concatenate(parts, axis=1)


def _seg_cumsum(v, seg):
    pos = lax.broadcasted_iota(jnp.int32, v.shape, 0) & (seg - 1)
    d = 1
    while d < seg:
        v = v + jnp.where(pos >= d, pltpu.roll(v, d, 0), 0.0)
        d *= 2
    return v


def _inproj_kernel(tiles_per_seq, x_ref, n1_ref, wg_ref, wf_ref, ws_ref, cw_ref, bias_ref, alog_ref, fqw_ref, fkw_ref,
                   ones_ref, sel_ref, one_ref,
                   gq_ref, gk_ref, gv_ref, gz_ref, fq_ref, fk_ref, fv_ref, fg_ref, sm_ref, kx_ref, qx_ref, gt_ref,
                   tail_ref, carry_ref):
    i = pl.program_id(0)
    tm = x_ref.shape[0]

    @pl.when(i % tiles_per_seq == 0)
    def _():
        tail_ref[:, 0:8, :] = jnp.zeros((3, 8, WIDTH), F32)
        carry_ref[...] = jnp.zeros_like(carry_ref)

    x = x_ref[...]
    h = (x * lax.rsqrt(jnp.mean(x * x, axis=-1, keepdims=True) + EPS) * n1_ref[...]).astype(BF16)
    ones_bd = ones_ref[...]

    y_gdn = jnp.dot(h, wg_ref[...], preferred_element_type=F32)
    y_fox = jnp.dot(h, wf_ref[...], preferred_element_type=F32)
    ys = ([y_gdn[:, s * WIDTH:(s + 1) * WIDTH] for s in range(4)]
          + [y_fox[:, s * WIDTH:(s + 1) * WIDTH] for s in range(4)]
          + [jnp.dot(h, ws_ref[...], preferred_element_type=F32)])

    for s, out_ref in enumerate((gq_ref, gk_ref, gv_ref)):
        y = ys[s]
        cw = cw_ref[:, s * WIDTH:(s + 1) * WIDTH]
        tail_ref[s, 8:8 + tm] = y
        acc = y * cw[CONV_K - 1:CONV_K]
        for k in range(1, CONV_K):
            acc = acc + tail_ref[s, 8 - k:8 - k + tm] * cw[CONV_K - 1 - k:CONV_K - k]
        tail_ref[s, 0:8] = y[tm - 8:]
        act = acc * _sigmoid(acc)
        if s < 2:
            act = act * lax.rsqrt(_head_sums(act * act, ones_bd) + EPS)
        out_ref[...] = act.astype(BF16)

    z = ys[3]
    gz_ref[...] = (z * _sigmoid(z)).astype(BF16)

    for s, out_ref, w_norm in ((4, fq_ref, fqw_ref), (5, fk_ref, fkw_ref)):
        y = ys[s]
        ms = _head_sums(y * y, ones_bd) * (1.0 / HEAD_DIM)
        out_ref[...] = (y * lax.rsqrt(ms + EPS) * w_norm[...]).astype(BF16)

    fv_ref[...] = ys[6].astype(BF16)
    fg_ref[...] = _sigmoid(ys[7]).astype(BF16)

    t = ys[8] + bias_ref[...]
    lane = lax.broadcasted_iota(jnp.int32, t.shape, 1)
    beta = _sigmoid(t)
    g_log = -jnp.exp(alog_ref[...]) * _softplus(t)
    log_f = -_softplus(-t)
    g_cum = _seg_cumsum(g_log, CHUNK)
    f_cum = _seg_cumsum(log_f, tm) + carry_ref[0:1, :]
    carry_ref[0:1, :] = f_cum[tm - 1:tm, :]
    sm_ref[...] = jnp.where(lane < HEADS, beta, jnp.where(lane < 2 * HEADS, g_cum, f_cum))
    gt_ref[...] = g_cum.T[HEADS:2 * HEADS, :]

    f2 = f_cum * LOG2E
    hi = f2.astype(BF16)
    r1 = f2 - hi.astype(F32)
    mid = r1.astype(BF16)
    lo = (r1 - mid.astype(F32)).astype(BF16)
    ext = jnp.dot(jnp.concatenate([hi, mid, lo], axis=1), sel_ref[...], preferred_element_type=F32) + one_ref[...]
    kx_ref[...] = ext[:, :LANES].astype(BF16)
    qx_ref[...] = ext[:, LANES:].astype(BF16)


def _bias_selectors():
    sel = np.zeros((3 * LANES, 2 * LANES), np.float32)
    one = np.zeros((1, 2 * LANES), np.float32)
    for h in range(HEADS):
        src = 2 * HEADS + h
        for part in range(3):
            sel[part * LANES + src, EXT_STRIDE * h + part] = -1.0
            sel[part * LANES + src, LANES + EXT_STRIDE * h + 3 + part] = 1.0
            one[0, EXT_STRIDE * h + 3 + part] = 1.0
            one[0, LANES + EXT_STRIDE * h + part] = 1.0
    return jnp.asarray(sel, BF16), jnp.asarray(one, F32)


def _inproj_call(x2, n1w, w_gdn, w_fox, w_gates, conv_w, bias_vec, alog_vec, fqw, fkw, ones_bd, *, seq, tm):
    m = x2.shape[0]
    assert seq % tm == 0 and tm % CHUNK == 0 and (tm & (tm - 1)) == 0
    sel, one = _bias_selectors()
    wide = pl.BlockSpec((tm, WIDTH), lambda i: (i, 0))
    narrow = pl.BlockSpec((tm, LANES), lambda i: (i, 0))
    out_shape = ([jax.ShapeDtypeStruct((m, WIDTH), BF16)] * 8 + [jax.ShapeDtypeStruct((m, N_SMALL), F32)]
                 + [jax.ShapeDtypeStruct((m, LANES), BF16)] * 2 + [jax.ShapeDtypeStruct((HEADS, m), F32)])
    return pl.pallas_call(
        functools.partial(_inproj_kernel, seq // tm),
        grid=(m // tm,),
        in_specs=[pl.BlockSpec((tm, D_MODEL), lambda i: (i, 0)),
                  _const_spec(n1w.shape), _const_spec(w_gdn.shape), _const_spec(w_fox.shape),
                  _const_spec(w_gates.shape), _const_spec(conv_w.shape),
                  _const_spec(bias_vec.shape), _const_spec(alog_vec.shape),
                  _const_spec(fqw.shape), _const_spec(fkw.shape), _const_spec(ones_bd.shape),
                  _const_spec(sel.shape), _const_spec(one.shape)],
        out_specs=[wide] * 8 + [narrow] * 3 + [pl.BlockSpec((HEADS, tm), lambda i: (0, i))],
        out_shape=out_shape,
        scratch_shapes=[pltpu.VMEM((3, tm + 8, WIDTH), F32), pltpu.VMEM((8, N_SMALL), F32)],
        compiler_params=pltpu.CompilerParams(dimension_semantics=("arbitrary",),
                                             vmem_limit_bytes=VMEM_LIMIT),
    )(x2, n1w, w_gdn, w_fox, w_gates, conv_w, bias_vec, alog_vec, fqw, fkw, ones_bd, sel, one)


def _lane_lo_mask():
    return lax.broadcasted_iota(jnp.int32, (CHUNK, LANES), 1) < HEAD_DIM


def _block_diag(x, lane_lo):
    xb = x.astype(BF16)
    zero = jnp.zeros((CHUNK, LANES), BF16)
    blocks = []
    for h in range(PACK):
        half = xb[:, (h // 2) * LANES:(h // 2 + 1) * LANES]
        keep = jnp.where(lane_lo if h % 2 == 0 else jnp.logical_not(lane_lo), half, zero)
        blocks.append(jnp.concatenate([keep, zero] if h < 2 else [zero, keep], axis=1))
    return jnp.concatenate(blocks, axis=0)


def _mm(lhs, rhs_bd):
    return jnp.dot(lhs.astype(BF16), rhs_bd, preferred_element_type=F32)


def _expand_heads(sm, first_lane):
    lane_lo = lax.broadcasted_iota(jnp.int32, (sm.shape[0], LANES), 1) < HEAD_DIM
    parts = []
    for p in range(HEADS // 2):
        c = first_lane + 2 * p
        even = jnp.broadcast_to(sm[:, c:c + 1], (sm.shape[0], LANES))
        odd = jnp.broadcast_to(sm[:, c + 1:c + 2], (sm.shape[0], LANES))
        parts.append(jnp.where(lane_lo, even, odd))
    return jnp.concatenate(parts, axis=1)


def _gdn_wy_kernel(q_ref, k_ref, v_ref, sm_ref, gt_ref,
                   u_ref, w_ref, a_ref, qg_ref, kd_ref, dl_ref):
    n_chunks = q_ref.shape[0] // CHUNK
    n_groups = WIDTH // MXU_DIM
    shape = (CHUNK, MXU_DIM)
    row = lax.broadcasted_iota(jnp.int32, shape, 0)
    col = lax.broadcasted_iota(jnp.int32, shape, 1) & (HEAD_DIM - 1)
    causal = row >= col
    strict = row > col
    eye = (row == col).astype(F32)
    bd_mask = _lane_lo_mask()
    scale = HEAD_DIM ** -0.5

    units = [(slice(c * CHUNK, (c + 1) * CHUNK), slice(g * MXU_DIM, (g + 1) * MXU_DIM), c)
             for c in range(n_chunks) for g in range(n_groups)]
    cs = range(len(units))
    bd = lambda xs: [_block_diag(x, bd_mask) for x in xs]
    mm = lambda ls, rs_: [_mm(a, b) for a, b in zip(ls, rs_)]
    stack = lambda xs, ys: [jnp.concatenate([x, y], axis=0) for x, y in zip(xs, ys)]

    sm = sm_ref[...]
    bx_all = _expand_heads(sm, 0)
    gx_all = _expand_heads(sm, HEADS)
    q = [q_ref[r, l].astype(F32) for r, l, _ in units]
    k = [k_ref[r, l].astype(F32) for r, l, _ in units]
    bx = [bx_all[r, l] for r, l, _ in units]
    gx = [gx_all[r, l] for r, l, _ in units]
    kb = [k[c] * bx[c] for c in cs]
    def key_row(ch, g):
        return jnp.concatenate([gt_ref[g * PACK + h:g * PACK + h + 1, ch * CHUNK:(ch + 1) * CHUNK]
                                for h in range(PACK)], axis=1)

    decay = [jnp.exp(jnp.where(causal, gx[c] - key_row(ch, l.start // MXU_DIM), NEG_BIG))
             for c, (_, l, ch) in enumerate(units)]

    bd_k = bd(k)
    s1 = [lax.dot_general(jnp.concatenate([kb[c], q[c]], axis=0).astype(BF16), bd_k[c],
                          (((1,), (1,)), ((), ())), preferred_element_type=F32) for c in cs]
    lmat = [jnp.where(strict, s1[c][:CHUNK] * decay[c], 0.0) for c in cs]
    for c, (r, l, _) in enumerate(units):
        a_ref[r, l] = (s1[c][CHUNK:] * decay[c] * scale).astype(BF16)

    blk8 = (row >> 3) == (col >> 3)
    n8 = [jnp.where(blk8, -l, 0.0) for l in lmat]
    t0 = [eye + n for n in n8]
    p1 = mm(n8, bd(n8))
    r = mm(stack(p1, t0), bd(p1))
    ta = [t0[c] + r[c][CHUNK:] for c in cs]
    tinv = [ta[c] + z for c, z in enumerate(mm(ta, bd([x[:CHUNK] for x in r])))]
    for ls in (3, 4, 5):
        off = ((row >> (ls + 1)) == (col >> (ls + 1))) & ((row >> ls) == (col >> ls) + 1)
        y = mm([jnp.where(off, l, 0.0) for l in lmat], bd(tinv))
        tinv = [tinv[c] - z for c, z in enumerate(mm(tinv, bd(y)))]

    eg = [jnp.exp(g) for g in gx]
    u = mm(tinv, bd([v_ref[r, l].astype(F32) * bx[c] for c, (r, l, _) in enumerate(units)]))
    w = mm(tinv, bd([kb[c] * eg[c] for c in cs]))
    for c, (r, l, ch) in enumerate(units):
        u_ref[r, l] = u[c].astype(BF16)
        w_ref[r, l] = w[c].astype(BF16)
        qg_ref[r, l] = (q[c] * eg[c] * scale).astype(BF16)
        g_last = gx[c][CHUNK - 1:CHUNK, :]
        kd_ref[r, l] = (k[c] * jnp.exp(g_last - gx[c])).astype(BF16)
        dl_ref[ch, :, l] = jnp.exp(g_last)


def _gdn_wy_call(gq, gk, gv, small, g_t, *, rows):
    m = gq.shape[0]
    cpb = rows // CHUNK
    blk = pl.BlockSpec((rows, WIDTH), lambda i: (i, 0))
    rowblk = pl.BlockSpec((cpb, 1, WIDTH), lambda i: (i, 0, 0))
    bf = jax.ShapeDtypeStruct((m, WIDTH), BF16)
    return pl.pallas_call(
        _gdn_wy_kernel,
        grid=(m // rows,),
        in_specs=[blk] * 3 + [pl.BlockSpec((rows, N_SMALL), lambda i: (i, 0)),
                              pl.BlockSpec((HEADS, rows), lambda i: (0, i))],
        out_specs=[blk] * 5 + [rowblk],
        out_shape=[bf] * 5 + [jax.ShapeDtypeStruct((m // CHUNK, 1, WIDTH), F32)],
        compiler_params=pltpu.CompilerParams(dimension_semantics=("parallel",),
                                             vmem_limit_bytes=VMEM_LIMIT),
    )(gq, gk, gv, small, g_t)


SCAN_SEQS = 8


def _gdn_scan_kernel(n_cast, u_ref, w_ref, a_ref, qg_ref, kd_ref, dl_ref, gz_ref, nw_ref, ones_ref, *refs):
    cast_in, o_ref, cast_out = refs[:n_cast], refs[n_cast], refs[n_cast + 1:2 * n_cast + 1]
    s_ref, oacc_ref = refs[2 * n_cast + 1:]
    for src, dst in zip(cast_in, cast_out):
        dst[...] = src[...].astype(BF16)
    n_seq, rows, _ = u_ref.shape
    n_groups = WIDTH // MXU_DIM
    chains = [(b, g, slice(g * MXU_DIM, (g + 1) * MXU_DIM)) for b in range(n_seq) for g in range(n_groups)]
    lane_lo = _lane_lo_mask()
    lane_hi = jnp.logical_not(lane_lo)
    zero_half = jnp.zeros((HEAD_DIM, LANES), BF16)

    @pl.when(pl.program_id(1) == 0)
    def _():
        s_ref[...] = jnp.zeros_like(s_ref)

    def pair_lanes(h):
        return slice((h // 2) * LANES, (h // 2 + 1) * LANES)

    def as_block_diag(pieces):
        rows_ = [jnp.concatenate([p.astype(BF16), zero_half] if h < 2 else [zero_half, p.astype(BF16)], axis=1)
                 for h, p in enumerate(pieces)]
        return jnp.concatenate(rows_, axis=0)

    def body(c, carry):
        rs = pl.ds(pl.multiple_of(c * CHUNK, CHUNK), CHUNK)
        state = [[s_ref[b, g, h] for h in range(PACK)] for b, g, _ in chains]
        r = [jnp.dot(jnp.concatenate([w_ref[b, rs, l], qg_ref[b, rs, l]], axis=0),
                     as_block_diag(state[i]), preferred_element_type=F32) for i, (b, _, l) in enumerate(chains)]
        v_new = [u_ref[b, rs, l].astype(F32) - r[i][:CHUNK] for i, (b, _, l) in enumerate(chains)]
        upd = [lax.dot_general(kd_ref[b, rs, l], v_new[i].astype(BF16), (((0,), (0,)), ((), ())),
                               preferred_element_type=F32) for i, (b, _, l) in enumerate(chains)]
        for i, (b, g, l) in enumerate(chains):
            decay = dl_ref[b, c, :, l]
            for h in range(PACK):
                own = upd[i][h * HEAD_DIM:(h + 1) * HEAD_DIM, pair_lanes(h)]
                s_ref[b, g, h] = (state[i][h] * decay[:, pair_lanes(h)]
                                  + jnp.where(lane_lo if h % 2 == 0 else lane_hi, own, 0.0))
        for i, (b, _, l) in enumerate(chains):
            oacc_ref[b, rs, l] = r[i][CHUNK:] + jnp.dot(a_ref[b, rs, l], _block_diag(v_new[i], lane_lo),
                                                        preferred_element_type=F32)
        return carry

    lax.fori_loop(0, rows // CHUNK, body, 0)

    for b in range(n_seq):
        o = oacc_ref[b]
        ms = _head_sums(o * o, ones_ref[...]) * (1.0 / HEAD_DIM)
        o_ref[b] = (o * lax.rsqrt(ms + EPS) * nw_ref[...] * gz_ref[b].astype(F32)).astype(BF16)


def _gdn_scan_call(u, w, a, qg, kd, dl, gz, nw, ones_bd, weights_f32, *, seq, n_seq, rows):
    batch = u.shape[0] // seq
    steps = seq // rows
    as3d = lambda t: t.reshape(batch, seq, WIDTH)
    blk = pl.BlockSpec((n_seq, rows, WIDTH), lambda b, t: (b, t, 0))
    dl4 = dl.reshape(batch, seq // CHUNK, 1, WIDTH)
    slabs = [pl.BlockSpec((wt.shape[0] // steps, wt.shape[1]), lambda b, t: (t, 0)) for wt in weights_f32]
    assert all(wt.shape[0] % (16 * steps) == 0 for wt in weights_f32)
    outs = pl.pallas_call(
        functools.partial(_gdn_scan_kernel, len(weights_f32)),
        grid=(batch // n_seq, steps),
        in_specs=[blk] * 5 + [pl.BlockSpec((n_seq, rows // CHUNK, 1, WIDTH), lambda b, t: (b, t, 0, 0)), blk,
                              _const_spec(nw.shape), _const_spec(ones_bd.shape)] + slabs,
        out_specs=[blk] + slabs,
        out_shape=[jax.ShapeDtypeStruct((batch, seq, WIDTH), BF16)]
                  + [jax.ShapeDtypeStruct(wt.shape, BF16) for wt in weights_f32],
        scratch_shapes=[pltpu.VMEM((n_seq, WIDTH // MXU_DIM, PACK, HEAD_DIM, LANES), F32),
                        pltpu.VMEM((n_seq, rows, WIDTH), F32)],
        compiler_params=pltpu.CompilerParams(dimension_semantics=("arbitrary", "arbitrary"),
                                             vmem_limit_bytes=VMEM_LIMIT),
    )(as3d(u), as3d(w), as3d(a), as3d(qg), as3d(kd), dl4, as3d(gz), nw, ones_bd, *weights_f32)
    return outs[0].reshape(batch * seq, WIDTH), outs[1:]


FOX_TQ = 2048
FOX_TK = 256
VT_ROWS = HEAD_DIM + 16


def _fox_kernel(tk, q_ref, qx_ref, k_ref, kx_ref, v_ref, gate_ref, o_ref,
                vt_ref, kp_ref, s_ref, m_ref, acc_ref, qt_ref):
    qi = pl.program_id(1)
    slots = q_ref.shape[0] // tk
    tq = tk
    heads = range(HEADS)

    @pl.when(qi == 0)
    def _():
        v_t = v_ref[...].T
        for h in heads:
            vt_ref[h * VT_ROWS:h * VT_ROWS + HEAD_DIM] = v_t[h * HEAD_DIM:(h + 1) * HEAD_DIM]
            vt_ref[h * VT_ROWS + HEAD_DIM:(h + 1) * VT_ROWS] = jnp.ones((VT_ROWS - HEAD_DIM, v_t.shape[1]), BF16)
        lane = lax.broadcasted_iota(jnp.int32, (k_ref.shape[0], LANES), 1)
        kx = kx_ref[...].astype(F32)
        for p in range(HEADS // 2):
            pair = k_ref[:, p * LANES:(p + 1) * LANES].astype(F32)
            for h in (2 * p, 2 * p + 1):
                own = pair if h % 2 == 0 else pltpu.roll(pair, HEAD_DIM, 1)
                ext = pltpu.roll(kx, (HEAD_DIM - EXT_STRIDE * h) % LANES, 1)
                blk = jnp.where(lane < HEAD_DIM, own, jnp.where(lane < HEAD_DIM + EXT_STRIDE, ext, 0.0))
                kp_ref[:, h * LANES:(h + 1) * LANES] = blk.astype(BF16)

    for slot in range(slots):
        rows = slice(slot * tq, (slot + 1) * tq)
        qx_t = qx_ref[rows, :].T
        for h in heads:
            pair_t = q_ref[rows, (h // 2) * LANES:(h // 2 + 1) * LANES].T
            qt_ref[slot, h] = jnp.concatenate(
                [pair_t[(h % 2) * HEAD_DIM:(h % 2 + 1) * HEAD_DIM], qx_t[EXT_STRIDE * h:EXT_STRIDE * (h + 1)],
                 jnp.zeros((LANES - HEAD_DIM - EXT_STRIDE, tq), BF16)], axis=0)
            m_ref[slot, h] = jnp.full((1, tq), NEG_BIG, F32)
            acc_ref[slot, h] = jnp.zeros((VT_ROWS, tq), F32)
    kv_pos = lax.broadcasted_iota(jnp.int32, (tk, tq), 0)
    q_pos = lax.broadcasted_iota(jnp.int32, (tk, tq), 1)

    def score(h, slot, j, masked):
        k0 = pl.multiple_of(j * tk, tk)
        s_h = jnp.dot(kp_ref[pl.ds(k0, tk), h * LANES:(h + 1) * LANES], qt_ref[slot, h],
                      preferred_element_type=F32)
        if masked:
            s_h = jnp.where(kv_pos <= q_pos, s_h, NEG_BIG)
        s_ref[h] = s_h

    def absorb(h, slot, j):
        k0 = pl.multiple_of(j * tk, tk)
        m_prev = m_ref[slot, h]
        s_h = s_ref[h]
        m_new = jnp.maximum(m_prev, jnp.max(s_h, axis=0, keepdims=True))
        p = jnp.exp2(s_h - m_new).astype(BF16)
        pv = jnp.dot(vt_ref[h * VT_ROWS:(h + 1) * VT_ROWS, pl.ds(k0, tk)], p, preferred_element_type=F32)
        m_ref[slot, h] = m_new
        acc_ref[slot, h] = jnp.exp2(m_prev - m_new) * acc_ref[slot, h] + pv

    first = qi * slots
    for h in heads:
        score(h, 0, first, True)
    for slot in range(1, slots):
        for h in heads:
            absorb(h, slot - 1, first + slot - 1)
            score(h, slot, first + slot, True)

    def step(t, pending):
        p_slot, p_kv = pending
        n_slot = jnp.int32(0)
        n_kv = t
        for s in range(1, slots):
            start = s * first + (s * (s - 1)) // 2
            n_slot = jnp.where(t >= start, s, n_slot)
            n_kv = jnp.where(t >= start, t - start, n_kv)
        for h in heads:
            absorb(h, p_slot, p_kv)
            score(h, n_slot, n_kv, False)
        return n_slot, n_kv

    n_off = slots * first + (slots * (slots - 1)) // 2
    p_slot, p_kv = lax.fori_loop(0, n_off, step, (jnp.int32(slots - 1), first + slots - 1))
    for h in heads:
        absorb(h, p_slot, p_kv)
    for slot in range(slots):
        rows = slice(slot * tq, (slot + 1) * tq)
        o_t = jnp.concatenate([acc_ref[slot, h, 0:HEAD_DIM] / acc_ref[slot, h, HEAD_DIM:HEAD_DIM + 1]
                               for h in heads], axis=0)
        o_ref[rows, :] = (o_t.T * gate_ref[rows, :].astype(F32)).astype(BF16)


def _fox_call(fq, qx, fk, kx, fv, gate, *, seq, tq, tk):
    m = fq.shape[0]
    nq = seq // tq
    qblk = pl.BlockSpec((tq, WIDTH), lambda b, i: (b * nq + i, 0))
    kvblk = pl.BlockSpec((seq, WIDTH), lambda b, i: (b, 0))
    return pl.pallas_call(
        functools.partial(_fox_kernel, tk),
        grid=(m // seq, nq),
        in_specs=[qblk, pl.BlockSpec((tq, LANES), lambda b, i: (b * nq + i, 0)),
                  kvblk, pl.BlockSpec((seq, LANES), lambda b, i: (b, 0)), kvblk, qblk],
        out_specs=qblk,
        out_shape=jax.ShapeDtypeStruct((m, WIDTH), BF16),
        scratch_shapes=[pltpu.VMEM((HEADS * VT_ROWS, seq), BF16), pltpu.VMEM((seq, HEADS * LANES), BF16),
                        pltpu.VMEM((HEADS, tk, tk), F32),
                        pltpu.VMEM((tq // tk, HEADS, 1, tk), F32), pltpu.VMEM((tq // tk, HEADS, VT_ROWS, tk), F32),
                        pltpu.VMEM((tq // tk, HEADS, LANES, tk), BF16)],
        compiler_params=pltpu.CompilerParams(dimension_semantics=("parallel", "arbitrary"),
                                             vmem_limit_bytes=VMEM_LIMIT),
    )(fq, qx, fk, kx, fv, gate)


FF_CHUNK = 4 * MXU_DIM
FF_SPLITS = tuple((s, min(FF_CHUNK, D_FF - s)) for s in range(0, D_FF, FF_CHUNK))


FFN_ROW_GROUPS = 2


def _ffn_kernel(final, mg_ref, mf_ref, x_ref, wo_ref, n2_ref, wg_ref, wu_ref, wd_ref, fn_ref, o_ref):
    rows = x_ref.shape[0] // FFN_ROW_GROUPS
    groups = [slice(r * rows, (r + 1) * rows) for r in range(FFN_ROW_GROUPS)]
    x1 = [x_ref[g, :]
          + jnp.dot(mg_ref[g, :], wo_ref[0:WIDTH, :], preferred_element_type=F32)
          + jnp.dot(mf_ref[g, :], wo_ref[WIDTH:2 * WIDTH, :], preferred_element_type=F32) for g in groups]
    h = [(v * lax.rsqrt(jnp.mean(v * v, axis=-1, keepdims=True) + EPS) * n2_ref[...]).astype(BF16) for v in x1]
    ffn = [None] * len(groups)
    for start, size in FF_SPLITS:
        gate = [jnp.dot(hv, wg_ref[:, start:start + size], preferred_element_type=F32) for hv in h]
        up = [jnp.dot(hv, wu_ref[:, start:start + size], preferred_element_type=F32) for hv in h]
        act = [(gv * _sigmoid(gv) * uv).astype(BF16) for gv, uv in zip(gate, up)]
        down = [jnp.dot(av, wd_ref[start:start + size, :], preferred_element_type=F32) for av in act]
        ffn = [d if f is None else f + d for f, d in zip(ffn, down)]
    for g, v, f in zip(groups, x1, ffn):
        y = v + f
        if final:
            y = y * lax.rsqrt(jnp.mean(y * y, axis=-1, keepdims=True) + EPS) * fn_ref[...]
        o_ref[g, :] = y


def _ffn_call(mix_g, mix_f, x2, wo, n2w, wg, wu, wd, fnw, *, tm, final):
    m = x2.shape[0]
    half = pl.BlockSpec((tm, WIDTH), lambda i: (i, 0))
    full = pl.BlockSpec((tm, D_MODEL), lambda i: (i, 0))

    def resident(shape):
        return pl.BlockSpec(shape, lambda i: (0, 0), pipeline_mode=pl.Buffered(1))

    return pl.pallas_call(
        functools.partial(_ffn_kernel, final),
        grid=(m // tm,),
        in_specs=[half, half, full, resident(wo.shape), resident(n2w.shape), resident(wg.shape),
                  resident(wu.shape), resident(wd.shape), resident(fnw.shape)],
        out_specs=full,
        out_shape=jax.ShapeDtypeStruct((m, D_MODEL), F32),
        compiler_params=pltpu.CompilerParams(dimension_semantics=("parallel",),
                                             vmem_limit_bytes=VMEM_LIMIT),
    )(mix_g, mix_f, x2, wo, n2w, wg, wu, wd, fnw)


def _lane_vec(parts):
    v = jnp.concatenate([p.astype(F32).reshape(-1) for p in parts])
    return jnp.pad(v, (0, N_SMALL - v.shape[0])).reshape(1, N_SMALL)


def kernel(x, norm1_w, w_in, gdn_conv_w, gdn_A_log, gdn_dt_bias, gdn_out_norm_w, fox_f_bias, fox_q_norm_w, fox_k_norm_w, w_out, norm2_w, w_ffn_gate, w_ffn_up, w_ffn_down, final_norm_w):
    batch, seq, _ = x.shape
    m = batch * seq
    depth = norm1_w.shape[0]
    zeros8 = jnp.zeros((HEADS,), F32)
    r2 = jnp.arange(MXU_DIM) // HEAD_DIM
    ones_bd = (r2[:, None] == r2[None, :]).astype(BF16)

    x2 = x.reshape(m, D_MODEL)
    for l in range(depth):
        w = w_in[l].astype(BF16)
        big = 4 * WIDTH
        g_small = w[:, big:big + 2 * HEADS]
        f_start = big + 2 * HEADS
        f_small = w[:, f_start + big:f_start + big + HEADS]
        w_gdn = w[:, :big]
        w_fox = w[:, f_start:f_start + big]
        w_gates = jnp.concatenate(
            [g_small, f_small, jnp.zeros((D_MODEL, N_SMALL - 3 * HEADS), BF16)], axis=1)
        bias_vec = _lane_vec([zeros8, gdn_dt_bias[l], fox_f_bias[l]])
        alog_vec = _lane_vec([zeros8, gdn_A_log[l], zeros8])
        fqw = (jnp.tile(fox_q_norm_w[l].astype(F32), HEADS) * (HEAD_DIM ** -0.5 * LOG2E)).reshape(1, WIDTH)
        fkw = jnp.tile(fox_k_norm_w[l].astype(F32), HEADS).reshape(1, WIDTH)

        gq, gk, gv, gz, fq, fk, fv, fg, small, kx, qx, g_t = _inproj_call(
            x2, norm1_w[l].reshape(1, D_MODEL), w_gdn, w_fox, w_gates, gdn_conv_w[l].astype(F32), bias_vec, alog_vec,
            fqw, fkw, ones_bd, seq=seq, tm=512)

        u, wmat, amat, qg, kd, dl = _gdn_wy_call(gq, gk, gv, small, g_t, rows=512)
        nw = jnp.tile(gdn_out_norm_w[l].astype(F32), HEADS).reshape(1, WIDTH)
        mix_g, (wo, wg, wu, wd) = _gdn_scan_call(
            u, wmat, amat, qg, kd, dl, gz, nw, ones_bd,
            [w_out[l], w_ffn_gate[l], w_ffn_up[l], w_ffn_down[l]],
            seq=seq, n_seq=math.gcd(batch, SCAN_SEQS), rows=256)
        mix_f = _fox_call(fq, qx, fk, kx, fv, fg, seq=seq, tq=FOX_TQ, tk=FOX_TK)

        x2 = _ffn_call(mix_g, mix_f, x2, wo, norm2_w[l].reshape(1, D_MODEL), wg, wu, wd,
                       final_norm_w.reshape(1, D_MODEL), tm=512, final=(l == depth - 1))
    return x2.reshape(batch, seq, D_MODEL)
```

```python
import functools
import math

import numpy as np
import jax
import jax.numpy as jnp
from jax import lax
from jax.experimental import pallas as pl
from jax.experimental.pallas import tpu as pltpu

D_MODEL = 1024
HEADS = 8
HEAD_DIM = 64
WIDTH = HEADS * HEAD_DIM
CONV_K = 4
CHUNK = 64
D_FF = 2816
EPS = 1e-6

LANES = 128
MXU_DIM = 256
PACK = MXU_DIM // HEAD_DIM
N_SMALL = LANES
VMEM_LIMIT = 56 * 1024 * 1024

F32 = jnp.float32
BF16 = jnp.bfloat16
NEG_BIG = -1e30
LOG2E = 1.4426950408889634
EXT_STRIDE = 16


def _const_spec(shape):
    nd = len(shape)
    return pl.BlockSpec(shape, lambda *_: (0,) * nd)


def _sigmoid(x):
    return 1.0 / (1.0 + jnp.exp(-x))


def _softplus(x):
    return jnp.maximum(x, 0.0) + jnp.log1p(jnp.exp(-jnp.abs(x)))


def _head_sums(y2, ones_bd):
    yb = y2.astype(BF16)
    parts = [jnp.dot(yb[:, c * MXU_DIM:(c + 1) * MXU_DIM], ones_bd, preferred_element_type=F32)
             for c in range(WIDTH // MXU_DIM)]
    return jnp.concatenate(parts, axis=1)


def _seg_cumsum(v, seg):
    pos = lax.broadcasted_iota(jnp.int32, v.shape, 0) & (seg - 1)
    d = 1
    while d < seg:
        v = v + jnp.where(pos >= d, pltpu.roll(v, d, 0), 0.0)
        d *= 2
    return v


def _inproj_kernel(tiles_per_seq, x_ref, n1_ref, wg_ref, wf_ref, ws_ref, cw_ref, bias_ref, alog_ref, fqw_ref, fkw_ref,
                   ones_ref, sel_ref, one_ref,
                   gq_ref, gk_ref, gv_ref, gz_ref, fq_ref, fk_ref, fv_ref, fg_ref, sm_ref, kx_ref, qx_ref, gt_ref,
                   tail_ref, carry_ref):
    i = pl.program_id(0)
    tm = x_ref.shape[0]

    @pl.when(i % tiles_per_seq == 0)
    def _():
        tail_ref[:, 0:8, :] = jnp.zeros((3, 8, WIDTH), F32)
        carry_ref[...] = jnp.zeros_like(carry_ref)

    x = x_ref[...]
    h = (x * lax.rsqrt(jnp.mean(x * x, axis=-1, keepdims=True) + EPS) * n1_ref[...]).astype(BF16)
    ones_bd = ones_ref[...]

    y_gdn = jnp.dot(h, wg_ref[...], preferred_element_type=F32)
    y_fox = jnp.dot(h, wf_ref[...], preferred_element_type=F32)
    ys = ([y_gdn[:, s * WIDTH:(s + 1) * WIDTH] for s in range(4)]
          + [y_fox[:, s * WIDTH:(s + 1) * WIDTH] for s in range(4)]
          + [jnp.dot(h, ws_ref[...], preferred_element_type=F32)])

    for s, out_ref in enumerate((gq_ref, gk_ref, gv_ref)):
        y = ys[s]
        cw = cw_ref[:, s * WIDTH:(s + 1) * WIDTH]
        tail_ref[s, 8:8 + tm] = y
        acc = y * cw[CONV_K - 1:CONV_K]
        for k in range(1, CONV_K):
            acc = acc + tail_ref[s, 8 - k:8 - k + tm] * cw[CONV_K - 1 - k:CONV_K - k]
        tail_ref[s, 0:8] = y[tm - 8:]
        act = acc * _sigmoid(acc)
        if s < 2:
            act = act * lax.rsqrt(_head_sums(act * act, ones_bd) + EPS)
        out_ref[...] = act.astype(BF16)

    z = ys[3]
    gz_ref[...] = (z * _sigmoid(z)).astype(BF16)

    for s, out_ref, w_norm in ((4, fq_ref, fqw_ref), (5, fk_ref, fkw_ref)):
        y = ys[s]
        ms = _head_sums(y * y, ones_bd) * (1.0 / HEAD_DIM)
        out_ref[...] = (y * lax.rsqrt(ms + EPS) * w_norm[...]).astype(BF16)

    fv_ref[...] = ys[6].astype(BF16)
    fg_ref[...] = _sigmoid(ys[7]).astype(BF16)

    t = ys[8] + bias_ref[...]
    lane = lax.broadcasted_iota(jnp.int32, t.shape, 1)
    beta = _sigmoid(t)
    g_log = -jnp.exp(alog_ref[...]) * _softplus(t)
    log_f = -_softplus(-t)
    g_cum = _seg_cumsum(g_log, CHUNK)
    f_cum = _seg_cumsum(log_f, tm) + carry_ref[0:1, :]
    carry_ref[0:1, :] = f_cum[tm - 1:tm, :]
    sm_ref[...] = jnp.where(lane < HEADS, beta, jnp.where(lane < 2 * HEADS, g_cum, f_cum))
    gt_ref[...] = g_cum.T[HEADS:2 * HEADS, :]

    f2 = f_cum * LOG2E
    hi = f2.astype(BF16)
    r1 = f2 - hi.astype(F32)
    mid = r1.astype(BF16)
    lo = (r1 - mid.astype(F32)).astype(BF16)
    ext = jnp.dot(jnp.concatenate([hi, mid, lo], axis=1), sel_ref[...], preferred_element_type=F32) + one_ref[...]
    kx_ref[...] = ext[:, :LANES].astype(BF16)
    qx_ref[...] = ext[:, LANES:].astype(BF16)


def _bias_selectors():
    sel = np.zeros((3 * LANES, 2 * LANES), np.float32)
    one = np.zeros((1, 2 * LANES), np.float32)
    for h in range(HEADS):
        src = 2 * HEADS + h
        for part in range(3):
            sel[part * LANES + src, EXT_STRIDE * h + part] = -1.0
            sel[part * LANES + src, LANES + EXT_STRIDE * h + 3 + part] = 1.0
            one[0, EXT_STRIDE * h + 3 + part] = 1.0
            one[0, LANES + EXT_STRIDE * h + part] = 1.0
    return jnp.asarray(sel, BF16), jnp.asarray(one, F32)


def _inproj_call(x2, n1w, w_gdn, w_fox, w_gates, conv_w, bias_vec, alog_vec, fqw, fkw, ones_bd, *, seq, tm):
    m = x2.shape[0]
    assert seq % tm == 0 and tm % CHUNK == 0 and (tm & (tm - 1)) == 0
    sel, one = _bias_selectors()
    wide = pl.BlockSpec((tm, WIDTH), lambda i: (i, 0))
    narrow = pl.BlockSpec((tm, LANES), lambda i: (i, 0))
    out_shape = ([jax.ShapeDtypeStruct((m, WIDTH), BF16)] * 8 + [jax.ShapeDtypeStruct((m, N_SMALL), F32)]
                 + [jax.ShapeDtypeStruct((m, LANES), BF16)] * 2 + [jax.ShapeDtypeStruct((HEADS, m), F32)])
    return pl.pallas_call(
        functools.partial(_inproj_kernel, seq // tm),
        grid=(m // tm,),
        in_specs=[pl.BlockSpec((tm, D_MODEL), lambda i: (i, 0)),
                  _const_spec(n1w.shape), _const_spec(w_gdn.shape), _const_spec(w_fox.shape),
                  _const_spec(w_gates.shape), _const_spec(conv_w.shape),
                  _const_spec(bias_vec.shape), _const_spec(alog_vec.shape),
                  _const_spec(fqw.shape), _const_spec(fkw.shape), _const_spec(ones_bd.shape),
                  _const_spec(sel.shape), _const_spec(one.shape)],
        out_specs=[wide] * 8 + [narrow] * 3 + [pl.BlockSpec((HEADS, tm), lambda i: (0, i))],
        out_shape=out_shape,
        scratch_shapes=[pltpu.VMEM((3, tm + 8, WIDTH), F32), pltpu.VMEM((8, N_SMALL), F32)],
        compiler_params=pltpu.CompilerParams(dimension_semantics=("arbitrary",),
                                             vmem_limit_bytes=VMEM_LIMIT),
    )(x2, n1w, w_gdn, w_fox, w_gates, conv_w, bias_vec, alog_vec, fqw, fkw, ones_bd, sel, one)


def _lane_lo_mask():
    return lax.broadcasted_iota(jnp.int32, (CHUNK, LANES), 1) < HEAD_DIM


def _block_diag(x, lane_lo):
    xb = x.astype(BF16)
    zero = jnp.zeros((CHUNK, LANES), BF16)
    blocks = []
    for h in range(PACK):
        half = xb[:, (h // 2) * LANES:(h // 2 + 1) * LANES]
        keep = jnp.where(lane_lo if h % 2 == 0 else jnp.logical_not(lane_lo), half, zero)
        blocks.append(jnp.concatenate([keep, zero] if h < 2 else [zero, keep], axis=1))
    return jnp.concatenate(blocks, axis=0)


def _mm(lhs, rhs_bd):
    return jnp.dot(lhs.astype(BF16), rhs_bd, preferred_element_type=F32)


def _expand_heads(sm, first_lane):
    lane_lo = lax.broadcasted_iota(jnp.int32, (sm.shape[0], LANES), 1) < HEAD_DIM
    parts = []
    for p in range(HEADS // 2):
        c = first_lane + 2 * p
        even = jnp.broadcast_to(sm[:, c:c + 1], (sm.shape[0], LANES))
        odd = jnp.broadcast_to(sm[:, c + 1:c + 2], (sm.shape[0], LANES))
        parts.append(jnp.where(lane_lo, even, odd))
    return jnp.concatenate(parts, axis=1)


WY_GROUP_CHUNKS = 8


def _gdn_wy_kernel(q_ref, k_ref, v_ref, sm_ref, gt_ref,
                   u_ref, w_ref, a_ref, qg_ref, kd_ref, dl_ref):
    n_chunks = q_ref.shape[0] // CHUNK
    n_groups = WIDTH // MXU_DIM
    shape = (CHUNK, MXU_DIM)
    row = lax.broadcasted_iota(jnp.int32, shape, 0)
    col = lax.broadcasted_iota(jnp.int32, shape, 1) & (HEAD_DIM - 1)
    causal = row >= col
    strict = row > col
    eye = (row == col).astype(F32)
    bd_mask = _lane_lo_mask()
    scale = HEAD_DIM ** -0.5

    bd = lambda xs: [_block_diag(x, bd_mask) for x in xs]
    mm = lambda ls, rs_: [_mm(a, b) for a, b in zip(ls, rs_)]

    def key_row(ch, g):
        return jnp.concatenate([gt_ref[g * PACK + h:g * PACK + h + 1, ch * CHUNK:(ch + 1) * CHUNK]
                                for h in range(PACK)], axis=1)

    def prepare(ch):
        r = slice(ch * CHUNK, (ch + 1) * CHUNK)
        sm = sm_ref[r, :]
        bx_all = _expand_heads(sm, 0)
        gx_all = _expand_heads(sm, HEADS)
        out = []
        for g in range(n_groups):
            l = slice(g * MXU_DIM, (g + 1) * MXU_DIM)
            q = q_ref[r, l].astype(F32)
            k = k_ref[r, l].astype(F32)
            bx, gx = bx_all[:, l], gx_all[:, l]
            kb = k * bx
            decay = jnp.exp(jnp.where(causal, gx - key_row(ch, g), NEG_BIG))
            s1 = lax.dot_general(jnp.concatenate([kb, q], axis=0).astype(BF16), _block_diag(k, bd_mask),
                                 (((1,), (1,)), ((), ())), preferred_element_type=F32)
            a_ref[r, l] = (s1[CHUNK:] * decay * scale).astype(BF16)
            eg = jnp.exp(gx)
            g_last = gx[CHUNK - 1:CHUNK, :]
            qg_ref[r, l] = (q * eg * scale).astype(BF16)
            kd_ref[r, l] = (k * jnp.exp(g_last - gx)).astype(BF16)
            dl_ref[ch, :, l] = jnp.exp(g_last)
            out.append(dict(r=r, l=l, lmat=jnp.where(strict, s1[:CHUNK] * decay, 0.0),
                            vb=v_ref[r, l].astype(F32) * bx, kbg=kb * eg))
        return out

    def inverse_stages(units):
        st = {}
        lmat = [u["lmat"] for u in units]
        blk8 = (row >> 3) == (col >> 3)

        def base0():
            st["n8"] = [jnp.where(blk8, -l, 0.0) for l in lmat]
            st["t0"] = [eye + n for n in st["n8"]]
            st["p1"] = mm(st["n8"], bd(st["n8"]))

        def base1():
            r = mm([jnp.concatenate([p, t], axis=0) for p, t in zip(st["p1"], st["t0"])], bd(st["p1"]))
            st["p2"] = [x[:CHUNK] for x in r]
            st["t"] = [t + x[CHUNK:] for t, x in zip(st["t0"], r)]

        def base2():
            st["t"] = [t + z for t, z in zip(st["t"], mm(st["t"], bd(st["p2"])))]

        def merge_a(ls):
            off = ((row >> (ls + 1)) == (col >> (ls + 1))) & ((row >> ls) == (col >> ls) + 1)
            st["y"] = mm([jnp.where(off, l, 0.0) for l in lmat], bd(st["t"]))

        def merge_b():
            st["t"] = [t - z for t, z in zip(st["t"], mm(st["t"], bd(st["y"])))]

        def apply(name, ref):
            for u, x in zip(units, mm(st["t"], bd([u[name] for u in units]))):
                ref[u["r"], u["l"]] = x.astype(BF16)

        stages = [base0, base1, base2]
        for ls in (3, 4, 5):
            stages += [functools.partial(merge_a, ls), merge_b]
        return stages + [functools.partial(apply, "vb", u_ref), functools.partial(apply, "kbg", w_ref)]

    groups = [list(range(c, min(c + WY_GROUP_CHUNKS, n_chunks))) for c in range(0, n_chunks, WY_GROUP_CHUNKS)]
    units = [u for ch in groups[0] for u in prepare(ch)]
    for gi in range(len(groups)):
        upcoming = groups[gi + 1] if gi + 1 < len(groups) else []
        stages = inverse_stages(units)
        per_stage = -(-len(upcoming) // len(stages))
        units = []
        for si, stage in enumerate(stages):
            stage()
            for ch in upcoming[si * per_stage:(si + 1) * per_stage]:
                units += prepare(ch)


def _gdn_wy_call(gq, gk, gv, small, g_t, *, rows):
    m = gq.shape[0]
    cpb = rows // CHUNK
    blk = pl.BlockSpec((rows, WIDTH), lambda i: (i, 0))
    rowblk = pl.BlockSpec((cpb, 1, WIDTH), lambda i: (i, 0, 0))
    bf = jax.ShapeDtypeStruct((m, WIDTH), BF16)
    return pl.pallas_call(
        _gdn_wy_kernel,
        grid=(m // rows,),
        in_specs=[blk] * 3 + [pl.BlockSpec((rows, N_SMALL), lambda i: (i, 0)),
                              pl.BlockSpec((HEADS, rows), lambda i: (0, i))],
        out_specs=[blk] * 5 + [rowblk],
        out_shape=[bf] * 5 + [jax.ShapeDtypeStruct((m // CHUNK, 1, WIDTH), F32)],
        compiler_params=pltpu.CompilerParams(dimension_semantics=("parallel",),
                                             vmem_limit_bytes=VMEM_LIMIT),
    )(gq, gk, gv, small, g_t)


SCAN_SEQS = 8


def _gdn_scan_kernel(n_cast, u_ref, w_ref, a_ref, qg_ref, kd_ref, dl_ref, gz_ref, nw_ref, ones_ref, *refs):
    cast_in, o_ref, cast_out = refs[:n_cast], refs[n_cast], refs[n_cast + 1:2 * n_cast + 1]
    s_ref, oacc_ref = refs[2 * n_cast + 1:]
    for src, dst in zip(cast_in, cast_out):
        dst[...] = src[...].astype(BF16)
    n_seq, rows, _ = u_ref.shape
    n_groups = WIDTH // MXU_DIM
    chains = [(b, g, slice(g * MXU_DIM, (g + 1) * MXU_DIM)) for b in range(n_seq) for g in range(n_groups)]
    lane_lo = _lane_lo_mask()
    lane_hi = jnp.logical_not(lane_lo)
    zero_half = jnp.zeros((HEAD_DIM, LANES), BF16)

    @pl.when(pl.program_id(1) == 0)
    def _():
        s_ref[...] = jnp.zeros_like(s_ref)

    def pair_lanes(h):
        return slice((h // 2) * LANES, (h // 2 + 1) * LANES)

    def as_block_diag(pieces):
        rows_ = [jnp.concatenate([p.astype(BF16), zero_half] if h < 2 else [zero_half, p.astype(BF16)], axis=1)
                 for h, p in enumerate(pieces)]
        return jnp.concatenate(rows_, axis=0)

    def body(c, carry):
        rs = pl.ds(pl.multiple_of(c * CHUNK, CHUNK), CHUNK)
        state = [[s_ref[b, g, h] for h in range(PACK)] for b, g, _ in chains]
        r = [jnp.dot(jnp.concatenate([w_ref[b, rs, l], qg_ref[b, rs, l]], axis=0),
                     as_block_diag(state[i]), preferred_element_type=F32) for i, (b, _, l) in enumerate(chains)]
        v_new = [u_ref[b, rs, l].astype(F32) - r[i][:CHUNK] for i, (b, _, l) in enumerate(chains)]
        upd = [lax.dot_general(kd_ref[b, rs, l], v_new[i].astype(BF16), (((0,), (0,)), ((), ())),
                               preferred_element_type=F32) for i, (b, _, l) in enumerate(chains)]
        for i, (b, g, l) in enumerate(chains):
            decay = dl_ref[b, c, :, l]
            for h in range(PACK):
                own = upd[i][h * HEAD_DIM:(h + 1) * HEAD_DIM, pair_lanes(h)]
                s_ref[b, g, h] = (state[i][h] * decay[:, pair_lanes(h)]
                                  + jnp.where(lane_lo if h % 2 == 0 else lane_hi, own, 0.0))
        for i, (b, _, l) in enumerate(chains):
            oacc_ref[b, rs, l] = r[i][CHUNK:] + jnp.dot(a_ref[b, rs, l], _block_diag(v_new[i], lane_lo),
                                                        preferred_element_type=F32)
        return carry

    lax.fori_loop(0, rows // CHUNK, body, 0)

    for b in range(n_seq):
        o = oacc_ref[b]
        ms = _head_sums(o * o, ones_ref[...]) * (1.0 / HEAD_DIM)
        o_ref[b] = (o * lax.rsqrt(ms + EPS) * nw_ref[...] * gz_ref[b].astype(F32)).astype(BF16)


def _gdn_scan_call(u, w, a, qg, kd, dl, gz, nw, ones_bd, weights_f32, *, seq, n_seq, rows):
    batch = u.shape[0] // seq
    steps = seq // rows
    as3d = lambda t: t.reshape(batch, seq, WIDTH)
    blk = pl.BlockSpec((n_seq, rows, WIDTH), lambda b, t: (b, t, 0))
    dl4 = dl.reshape(batch, seq // CHUNK, 1, WIDTH)
    slabs = [pl.BlockSpec((wt.shape[0] // steps, wt.shape[1]), lambda b, t: (t, 0)) for wt in weights_f32]
    assert all(wt.shape[0] % (16 * steps) == 0 for wt in weights_f32)
    outs = pl.pallas_call(
        functools.partial(_gdn_scan_kernel, len(weights_f32)),
        grid=(batch // n_seq, steps),
        in_specs=[blk] * 5 + [pl.BlockSpec((n_seq, rows // CHUNK, 1, WIDTH), lambda b, t: (b, t, 0, 0)), blk,
                              _const_spec(nw.shape), _const_spec(ones_bd.shape)] + slabs,
        out_specs=[blk] + slabs,
        out_shape=[jax.ShapeDtypeStruct((batch, seq, WIDTH), BF16)]
                  + [jax.ShapeDtypeStruct(wt.shape, BF16) for wt in weights_f32],
        scratch_shapes=[pltpu.VMEM((n_seq, WIDTH // MXU_DIM, PACK, HEAD_DIM, LANES), F32),
                        pltpu.VMEM((n_seq, rows, WIDTH), F32)],
        compiler_params=pltpu.CompilerParams(dimension_semantics=("arbitrary", "arbitrary"),
                                             vmem_limit_bytes=VMEM_LIMIT),
    )(as3d(u), as3d(w), as3d(a), as3d(qg), as3d(kd), dl4, as3d(gz), nw, ones_bd, *weights_f32)
    return outs[0].reshape(batch * seq, WIDTH), outs[1:]


FOX_TQ = 2048
FOX_TK = 256
VT_ROWS = HEAD_DIM + 16


def _fox_kernel(tk, q_ref, qx_ref, k_ref, kx_ref, v_ref, gate_ref, o_ref,
                vt_ref, kp_ref, s_ref, m_ref, acc_ref, qt_ref):
    qi = pl.program_id(1)
    slots = q_ref.shape[0] // tk
    tq = tk
    heads = range(HEADS)

    @pl.when(qi == 0)
    def _():
        v_t = v_ref[...].T
        for h in heads:
            vt_ref[h * VT_ROWS:h * VT_ROWS + HEAD_DIM] = v_t[h * HEAD_DIM:(h + 1) * HEAD_DIM]
            vt_ref[h * VT_ROWS + HEAD_DIM:(h + 1) * VT_ROWS] = jnp.ones((VT_ROWS - HEAD_DIM, v_t.shape[1]), BF16)
        lane = lax.broadcasted_iota(jnp.int32, (k_ref.shape[0], LANES), 1)
        kx = kx_ref[...].astype(F32)
        for p in range(HEADS // 2):
            pair = k_ref[:, p * LANES:(p + 1) * LANES].astype(F32)
            for h in (2 * p, 2 * p + 1):
                own = pair if h % 2 == 0 else pltpu.roll(pair, HEAD_DIM, 1)
                ext = pltpu.roll(kx, (HEAD_DIM - EXT_STRIDE * h) % LANES, 1)
                blk = jnp.where(lane < HEAD_DIM, own, jnp.where(lane < HEAD_DIM + EXT_STRIDE, ext, 0.0))
                kp_ref[:, h * LANES:(h + 1) * LANES] = blk.astype(BF16)

    for slot in range(slots):
        rows = slice(slot * tq, (slot + 1) * tq)
        qx_t = qx_ref[rows, :].T
        for h in heads:
            pair_t = q_ref[rows, (h // 2) * LANES:(h // 2 + 1) * LANES].T
            qt_ref[slot, h] = jnp.concatenate(
                [pair_t[(h % 2) * HEAD_DIM:(h % 2 + 1) * HEAD_DIM], qx_t[EXT_STRIDE * h:EXT_STRIDE * (h + 1)],
                 jnp.zeros((LANES - HEAD_DIM - EXT_STRIDE, tq), BF16)], axis=0)
            m_ref[slot, h] = jnp.full((1, tq), NEG_BIG, F32)
            acc_ref[slot, h] = jnp.zeros((VT_ROWS, tq), F32)
    kv_pos = lax.broadcasted_iota(jnp.int32, (tk, tq), 0)
    q_pos = lax.broadcasted_iota(jnp.int32, (tk, tq), 1)

    def score(h, slot, j, masked):
        k0 = pl.multiple_of(j * tk, tk)
        s_h = jnp.dot(kp_ref[pl.ds(k0, tk), h * LANES:(h + 1) * LANES], qt_ref[slot, h],
                      preferred_element_type=F32)
        if masked:
            s_h = jnp.where(kv_pos <= q_pos, s_h, NEG_BIG)
        s_ref[h] = s_h

    def absorb(h, slot, j):
        k0 = pl.multiple_of(j * tk, tk)
        m_prev = m_ref[slot, h]
        s_h = s_ref[h]
        m_new = jnp.maximum(m_prev, jnp.max(s_h, axis=0, keepdims=True))
        p = jnp.exp2(s_h - m_new).astype(BF16)
        pv = jnp.dot(vt_ref[h * VT_ROWS:(h + 1) * VT_ROWS, pl.ds(k0, tk)], p, preferred_element_type=F32)
        m_ref[slot, h] = m_new
        acc_ref[slot, h] = jnp.exp2(m_prev - m_new) * acc_ref[slot, h] + pv

    first = qi * slots
    for h in heads:
        score(h, 0, first, True)
    for slot in range(1, slots):
        for h in heads:
            absorb(h, slot - 1, first + slot - 1)
            score(h, slot, first + slot, True)

    def step(t, pending):
        p_slot, p_kv = pending
        n_slot = jnp.int32(0)
        n_kv = t
        for s in range(1, slots):
            start = s * first + (s * (s - 1)) // 2
            n_slot = jnp.where(t >= start, s, n_slot)
            n_kv = jnp.where(t >= start, t - start, n_kv)
        for h in heads:
            absorb(h, p_slot, p_kv)
            score(h, n_slot, n_kv, False)
        return n_slot, n_kv

    n_off = slots * first + (slots * (slots - 1)) // 2
    p_slot, p_kv = lax.fori_loop(0, n_off, step, (jnp.int32(slots - 1), first + slots - 1))
    for h in heads:
        absorb(h, p_slot, p_kv)
    for slot in range(slots):
        rows = slice(slot * tq, (slot + 1) * tq)
        o_t = jnp.concatenate([acc_ref[slot, h, 0:HEAD_DIM] / acc_ref[slot, h, HEAD_DIM:HEAD_DIM + 1]
                               for h in heads], axis=0)
        o_ref[rows, :] = (o_t.T * gate_ref[rows, :].astype(F32)).astype(BF16)


def _fox_call(fq, qx, fk, kx, fv, gate, *, seq, tq, tk):
    m = fq.shape[0]
    nq = seq // tq
    qblk = pl.BlockSpec((tq, WIDTH), lambda b, i: (b * nq + i, 0))
    kvblk = pl.BlockSpec((seq, WIDTH), lambda b, i: (b, 0))
    return pl.pallas_call(
        functools.partial(_fox_kernel, tk),
        grid=(m // seq, nq),
        in_specs=[qblk, pl.BlockSpec((tq, LANES), lambda b, i: (b * nq + i, 0)),
                  kvblk, pl.BlockSpec((seq, LANES), lambda b, i: (b, 0)), kvblk, qblk],
        out_specs=qblk,
        out_shape=jax.ShapeDtypeStruct((m, WIDTH), BF16),
        scratch_shapes=[pltpu.VMEM((HEADS * VT_ROWS, seq), BF16), pltpu.VMEM((seq, HEADS * LANES), BF16),
                        pltpu.VMEM((HEADS, tk, tk), F32),
                        pltpu.VMEM((tq // tk, HEADS, 1, tk), F32), pltpu.VMEM((tq // tk, HEADS, VT_ROWS, tk), F32),
                        pltpu.VMEM((tq // tk, HEADS, LANES, tk), BF16)],
        compiler_params=pltpu.CompilerParams(dimension_semantics=("parallel", "arbitrary"),
                                             vmem_limit_bytes=VMEM_LIMIT),
    )(fq, qx, fk, kx, fv, gate)


FF_CHUNK = 4 * MXU_DIM
FF_SPLITS = tuple((s, min(FF_CHUNK, D_FF - s)) for s in range(0, D_FF, FF_CHUNK))


FFN_ROW_GROUPS = 2


def _ffn_kernel(final, mg_ref, mf_ref, x_ref, wo_ref, n2_ref, wg_ref, wu_ref, wd_ref, fn_ref, o_ref):
    rows = x_ref.shape[0] // FFN_ROW_GROUPS
    groups = [slice(r * rows, (r + 1) * rows) for r in range(FFN_ROW_GROUPS)]
    x1 = [x_ref[g, :]
          + jnp.dot(mg_ref[g, :], wo_ref[0:WIDTH, :], preferred_element_type=F32)
          + jnp.dot(mf_ref[g, :], wo_ref[WIDTH:2 * WIDTH, :], preferred_element_type=F32) for g in groups]
    h = [(v * lax.rsqrt(jnp.mean(v * v, axis=-1, keepdims=True) + EPS) * n2_ref[...]).astype(BF16) for v in x1]
    ffn = [None] * len(groups)
    for start, size in FF_SPLITS:
        gate = [jnp.dot(hv, wg_ref[:, start:start + size], preferred_element_type=F32) for hv in h]
        up = [jnp.dot(hv, wu_ref[:, start:start + size], preferred_element_type=F32) for hv in h]
        act = [(gv * _sigmoid(gv) * uv).astype(BF16) for gv, uv in zip(gate, up)]
        down = [jnp.dot(av, wd_ref[start:start + size, :], preferred_element_type=F32) for av in act]
        ffn = [d if f is None else f + d for f, d in zip(ffn, down)]
    for g, v, f in zip(groups, x1, ffn):
        y = v + f
        if final:
            y = y * lax.rsqrt(jnp.mean(y * y, axis=-1, keepdims=True) + EPS) * fn_ref[...]
        o_ref[g, :] = y


def _ffn_call(mix_g, mix_f, x2, wo, n2w, wg, wu, wd, fnw, *, tm, final):
    m = x2.shape[0]
    half = pl.BlockSpec((tm, WIDTH), lambda i: (i, 0))
    full = pl.BlockSpec((tm, D_MODEL), lambda i: (i, 0))

    def resident(shape):
        return pl.BlockSpec(shape, lambda i: (0, 0), pipeline_mode=pl.Buffered(1))

    return pl.pallas_call(
        functools.partial(_ffn_kernel, final),
        grid=(m // tm,),
        in_specs=[half, half, full, resident(wo.shape), resident(n2w.shape), resident(wg.shape),
                  resident(wu.shape), resident(wd.shape), resident(fnw.shape)],
        out_specs=full,
        out_shape=jax.ShapeDtypeStruct((m, D_MODEL), F32),
        compiler_params=pltpu.CompilerParams(dimension_semantics=("parallel",),
                                             vmem_limit_bytes=VMEM_LIMIT),
    )(mix_g, mix_f, x2, wo, n2w, wg, wu, wd, fnw)


def _lane_vec(parts):
    v = jnp.concatenate([p.astype(F32).reshape(-1) for p in parts])
    return jnp.pad(v, (0, N_SMALL - v.shape[0])).reshape(1, N_SMALL)


def kernel(x, norm1_w, w_in, gdn_conv_w, gdn_A_log, gdn_dt_bias, gdn_out_norm_w, fox_f_bias, fox_q_norm_w, fox_k_norm_w, w_out, norm2_w, w_ffn_gate, w_ffn_up, w_ffn_down, final_norm_w):
    batch, seq, _ = x.shape
    m = batch * seq
    depth = norm1_w.shape[0]
    zeros8 = jnp.zeros((HEADS,), F32)
    r2 = jnp.arange(MXU_DIM) // HEAD_DIM
    ones_bd = (r2[:, None] == r2[None, :]).astype(BF16)

    x2 = x.reshape(m, D_MODEL)
    for l in range(depth):
        w = w_in[l].astype(BF16)
        big = 4 * WIDTH
        g_small = w[:, big:big + 2 * HEADS]
        f_start = big + 2 * HEADS
        f_small = w[:, f_start + big:f_start + big + HEADS]
        w_gdn = w[:, :big]
        w_fox = w[:, f_start:f_start + big]
        w_gates = jnp.concatenate(
            [g_small, f_small, jnp.zeros((D_MODEL, N_SMALL - 3 * HEADS), BF16)], axis=1)
        bias_vec = _lane_vec([zeros8, gdn_dt_bias[l], fox_f_bias[l]])
        alog_vec = _lane_vec([zeros8, gdn_A_log[l], zeros8])
        fqw = (jnp.tile(fox_q_norm_w[l].astype(F32), HEADS) * (HEAD_DIM ** -0.5 * LOG2E)).reshape(1, WIDTH)
        fkw = jnp.tile(fox_k_norm_w[l].astype(F32), HEADS).reshape(1, WIDTH)

        gq, gk, gv, gz, fq, fk, fv, fg, small, kx, qx, g_t = _inproj_call(
            x2, norm1_w[l].reshape(1, D_MODEL), w_gdn, w_fox, w_gates, gdn_conv_w[l].astype(F32), bias_vec, alog_vec,
            fqw, fkw, ones_bd, seq=seq, tm=512)

        u, wmat, amat, qg, kd, dl = _gdn_wy_call(gq, gk, gv, small, g_t, rows=2048)
        nw = jnp.tile(gdn_out_norm_w[l].astype(F32), HEADS).reshape(1, WIDTH)
        mix_g, (wo, wg, wu, wd) = _gdn_scan_call(
            u, wmat, amat, qg, kd, dl, gz, nw, ones_bd,
            [w_out[l], w_ffn_gate[l], w_ffn_up[l], w_ffn_down[l]],
            seq=seq, n_seq=math.gcd(batch, SCAN_SEQS), rows=256)
        mix_f = _fox_call(fq, qx, fk, kx, fv, fg, seq=seq, tq=FOX_TQ, tk=FOX_TK)

        x2 = _ffn_call(mix_g, mix_f, x2, wo, norm2_w[l].reshape(1, D_MODEL), wg, wu, wd,
                       final_norm_w.reshape(1, D_MODEL), tm=512, final=(l == depth - 1))
    return x2.reshape(batch, seq, D_MODEL)
```

```python
import functools
import math

import numpy as np
import jax
import jax.numpy as jnp
from jax import lax
from jax.experimental import pallas as pl
from jax.experimental.pallas import tpu as pltpu

D_MODEL = 1024
HEADS = 8
HEAD_DIM = 64
WIDTH = HEADS * HEAD_DIM
CONV_K = 4
CHUNK = 64
D_FF = 2816
EPS = 1e-6

LANES = 128
MXU_DIM = 256
PACK = MXU_DIM // HEAD_DIM
N_SMALL = LANES
VMEM_LIMIT = 56 * 1024 * 1024

F32 = jnp.float32
BF16 = jnp.bfloat16
NEG_BIG = -1e30
LOG2E = 1.4426950408889634
EXT_STRIDE = 16


def _const_spec(shape):
    nd = len(shape)
    return pl.BlockSpec(shape, lambda *_: (0,) * nd)


def _sigmoid(x):
    return 1.0 / (1.0 + jnp.exp(-x))


def _head_sums(y2, ones_bd):
    yb = y2.astype(BF16)
    parts = [jnp.dot(yb[:, c * MXU_DIM:(c + 1) * MXU_DIM], ones_bd, preferred_element_type=F32)
             for c in range(WIDTH // MXU_DIM)]
    return jnp.concatenate(parts, axis=1)


def _seg_cumsum(v, pos, longest):
    d = 1
    while d < longest:
        v = v + jnp.where(pos >= d, pltpu.roll(v, d, 0), 0.0)
        d *= 2
    return v


def _inproj_kernel(tiles_per_seq, x_ref, n1_ref, wg_ref, wf_ref, ws_ref, cw_ref, bias_ref, alog_ref, fqw_ref, fkw_ref,
                   ones_ref, sel_ref, one_ref,
                   gq_ref, gk_ref, gv_ref, gz_ref, fq_ref, fk_ref, fv_ref, fg_ref, sm_ref, kx_ref, qx_ref, gt_ref,
                   tail_ref, carry_ref):
    i = pl.program_id(0)
    tm = x_ref.shape[0]

    @pl.when(i % tiles_per_seq == 0)
    def _():
        tail_ref[:, 0:8, :] = jnp.zeros((3, 8, WIDTH), F32)
        carry_ref[...] = jnp.zeros_like(carry_ref)

    x = x_ref[...]
    h = (x * lax.rsqrt(jnp.mean(x * x, axis=-1, keepdims=True) + EPS) * n1_ref[...]).astype(BF16)
    ones_bd = ones_ref[...]

    y_gates = jnp.dot(h, ws_ref[...], preferred_element_type=F32)
    y_gdn = jnp.dot(h, wg_ref[...], preferred_element_type=F32)
    y_fox = jnp.dot(h, wf_ref[...], preferred_element_type=F32)
    ys = ([y_gdn[:, s * WIDTH:(s + 1) * WIDTH] for s in range(4)]
          + [y_fox[:, s * WIDTH:(s + 1) * WIDTH] for s in range(4)] + [y_gates])

    for s, out_ref in enumerate((gq_ref, gk_ref, gv_ref)):
        y = ys[s]
        cw = cw_ref[:, s * WIDTH:(s + 1) * WIDTH]
        tail_ref[s, 8:8 + tm] = y
        acc = y * cw[CONV_K - 1:CONV_K]
        for k in range(1, CONV_K):
            acc = acc + tail_ref[s, 8 - k:8 - k + tm] * cw[CONV_K - 1 - k:CONV_K - k]
        tail_ref[s, 0:8] = y[tm - 8:]
        act = acc * _sigmoid(acc)
        if s < 2:
            act = act * lax.rsqrt(_head_sums(act * act, ones_bd) + EPS)
        out_ref[...] = act.astype(BF16)

    z = ys[3]
    gz_ref[...] = (z * _sigmoid(z)).astype(BF16)

    for s, out_ref, w_norm in ((4, fq_ref, fqw_ref), (5, fk_ref, fkw_ref)):
        y = ys[s]
        ms = _head_sums(y * y, ones_bd) * (1.0 / HEAD_DIM)
        out_ref[...] = (y * lax.rsqrt(ms + EPS) * w_norm[...]).astype(BF16)

    fv_ref[...] = ys[6].astype(BF16)
    fg_ref[...] = _sigmoid(ys[7]).astype(BF16)

    t = ys[8] + bias_ref[...]
    lane = lax.broadcasted_iota(jnp.int32, t.shape, 1)
    beta = _sigmoid(t)
    tail = jnp.log1p(jnp.exp(-jnp.abs(t)))
    g_log = -jnp.exp(alog_ref[...]) * (jnp.maximum(t, 0.0) + tail)
    log_f = -(jnp.maximum(-t, 0.0) + tail)
    row = lax.broadcasted_iota(jnp.int32, t.shape, 0)
    seg_pos = jnp.where(lane < 2 * HEADS, row & (CHUNK - 1), row)
    cum = _seg_cumsum(jnp.where(lane < 2 * HEADS, g_log, log_f), seg_pos, tm)
    f_cum = cum + carry_ref[0:1, :]
    carry_ref[0:1, :] = f_cum[tm - 1:tm, :]
    sm_ref[...] = jnp.where(lane < HEADS, beta, jnp.where(lane < 2 * HEADS, cum, f_cum))
    gt_ref[...] = cum.T[HEADS:2 * HEADS, :]

    f2 = f_cum * LOG2E
    hi = f2.astype(BF16)
    r1 = f2 - hi.astype(F32)
    mid = r1.astype(BF16)
    lo = (r1 - mid.astype(F32)).astype(BF16)
    ext = jnp.dot(jnp.concatenate([hi, mid, lo], axis=1), sel_ref[...], preferred_element_type=F32) + one_ref[...]
    kx_ref[...] = ext[:, :LANES].astype(BF16)
    qx_ref[...] = ext[:, LANES:].astype(BF16)


def _bias_selectors():
    sel = np.zeros((3 * LANES, 2 * LANES), np.float32)
    one = np.zeros((1, 2 * LANES), np.float32)
    for h in range(HEADS):
        src = 2 * HEADS + h
        for part in range(3):
            sel[part * LANES + src, EXT_STRIDE * h + part] = -1.0
            sel[part * LANES + src, LANES + EXT_STRIDE * h + 3 + part] = 1.0
            one[0, EXT_STRIDE * h + 3 + part] = 1.0
            one[0, LANES + EXT_STRIDE * h + part] = 1.0
    return jnp.asarray(sel, BF16), jnp.asarray(one, F32)


def _inproj_call(x2, n1w, w_gdn, w_fox, w_gates, conv_w, bias_vec, alog_vec, fqw, fkw, ones_bd, *, seq, tm):
    m = x2.shape[0]
    assert seq % tm == 0 and tm % CHUNK == 0 and (tm & (tm - 1)) == 0
    sel, one = _bias_selectors()
    wide = pl.BlockSpec((tm, WIDTH), lambda i: (i, 0))
    narrow = pl.BlockSpec((tm, LANES), lambda i: (i, 0))
    out_shape = ([jax.ShapeDtypeStruct((m, WIDTH), BF16)] * 8 + [jax.ShapeDtypeStruct((m, N_SMALL), F32)]
                 + [jax.ShapeDtypeStruct((m, LANES), BF16)] * 2 + [jax.ShapeDtypeStruct((HEADS, m), F32)])
    return pl.pallas_call(
        functools.partial(_inproj_kernel, seq // tm),
        grid=(m // tm,),
        in_specs=[pl.BlockSpec((tm, D_MODEL), lambda i: (i, 0)),
                  _const_spec(n1w.shape), _const_spec(w_gdn.shape), _const_spec(w_fox.shape),
                  _const_spec(w_gates.shape), _const_spec(conv_w.shape),
                  _const_spec(bias_vec.shape), _const_spec(alog_vec.shape),
                  _const_spec(fqw.shape), _const_spec(fkw.shape), _const_spec(ones_bd.shape),
                  _const_spec(sel.shape), _const_spec(one.shape)],
        out_specs=[wide] * 8 + [narrow] * 3 + [pl.BlockSpec((HEADS, tm), lambda i: (0, i))],
        out_shape=out_shape,
        scratch_shapes=[pltpu.VMEM((3, tm + 8, WIDTH), F32), pltpu.VMEM((8, N_SMALL), F32)],
        compiler_params=pltpu.CompilerParams(dimension_semantics=("arbitrary",),
                                             vmem_limit_bytes=VMEM_LIMIT),
    )(x2, n1w, w_gdn, w_fox, w_gates, conv_w, bias_vec, alog_vec, fqw, fkw, ones_bd, sel, one)


def _lane_lo_mask():
    return lax.broadcasted_iota(jnp.int32, (CHUNK, LANES), 1) < HEAD_DIM


def _block_diag(x, lane_lo):
    xb = x.astype(BF16)
    zero = jnp.zeros((CHUNK, LANES), BF16)
    blocks = []
    for h in range(PACK):
        half = xb[:, (h // 2) * LANES:(h // 2 + 1) * LANES]
        keep = jnp.where(lane_lo if h % 2 == 0 else jnp.logical_not(lane_lo), half, zero)
        blocks.append(jnp.concatenate([keep, zero] if h < 2 else [zero, keep], axis=1))
    return jnp.concatenate(blocks, axis=0)


def _mm(lhs, rhs_bd):
    return jnp.dot(lhs.astype(BF16), rhs_bd, preferred_element_type=F32)


def _expand_heads(sm, first_lane):
    lane_lo = lax.broadcasted_iota(jnp.int32, (sm.shape[0], LANES), 1) < HEAD_DIM
    parts = []
    for p in range(HEADS // 2):
        c = first_lane + 2 * p
        even = jnp.broadcast_to(sm[:, c:c + 1], (sm.shape[0], LANES))
        odd = jnp.broadcast_to(sm[:, c + 1:c + 2], (sm.shape[0], LANES))
        parts.append(jnp.where(lane_lo, even, odd))
    return jnp.concatenate(parts, axis=1)


WY_GROUP_CHUNKS = 8


def _gdn_wy_kernel(q_ref, k_ref, v_ref, sm_ref, gt_ref,
                   u_ref, w_ref, a_ref, qg_ref, kd_ref, dl_ref):
    n_chunks = q_ref.shape[0] // CHUNK
    n_groups = WIDTH // MXU_DIM
    shape = (CHUNK, MXU_DIM)
    row = lax.broadcasted_iota(jnp.int32, shape, 0)
    col = lax.broadcasted_iota(jnp.int32, shape, 1) & (HEAD_DIM - 1)
    causal = row >= col
    strict = row > col
    eye = (row == col).astype(F32)
    bd_mask = _lane_lo_mask()
    scale = HEAD_DIM ** -0.5

    bd = lambda xs: [_block_diag(x, bd_mask) for x in xs]
    mm = lambda ls, rs_: [_mm(a, b) for a, b in zip(ls, rs_)]

    def key_row(ch, g):
        return jnp.concatenate([gt_ref[g * PACK + h:g * PACK + h + 1, ch * CHUNK:(ch + 1) * CHUNK]
                                for h in range(PACK)], axis=1)

    def prepare(ch):
        r = slice(ch * CHUNK, (ch + 1) * CHUNK)
        sm = sm_ref[r, :]
        bx_all = _expand_heads(sm, 0)
        gx_all = _expand_heads(sm, HEADS)
        out = []
        for g in range(n_groups):
            l = slice(g * MXU_DIM, (g + 1) * MXU_DIM)
            q = q_ref[r, l].astype(F32)
            k = k_ref[r, l].astype(F32)
            bx, gx = bx_all[:, l], gx_all[:, l]
            kb = k * bx
            decay = jnp.exp(jnp.where(causal, gx - key_row(ch, g), NEG_BIG))
            s1 = lax.dot_general(jnp.concatenate([kb, q], axis=0).astype(BF16), _block_diag(k, bd_mask),
                                 (((1,), (1,)), ((), ())), preferred_element_type=F32)
            a_ref[r, l] = (s1[CHUNK:] * decay * scale).astype(BF16)
            eg = jnp.exp(gx)
            g_last = gx[CHUNK - 1:CHUNK, :]
            qg_ref[r, l] = (q * eg * scale).astype(BF16)
            kd_ref[r, l] = (k * jnp.exp(g_last - gx)).astype(BF16)
            dl_ref[ch, :, l] = jnp.exp(g_last)
            out.append(dict(r=r, l=l, lmat=jnp.where(strict, s1[:CHUNK] * decay, 0.0),
                            vb=v_ref[r, l].astype(F32) * bx, kbg=kb * eg))
        return out

    def inverse_stages(units):
        st = {}
        lmat = [u["lmat"] for u in units]
        blk8 = (row >> 3) == (col >> 3)

        def base0():
            st["n8"] = [jnp.where(blk8, -l, 0.0) for l in lmat]
            st["t0"] = [eye + n for n in st["n8"]]
            st["p1"] = mm(st["n8"], bd(st["n8"]))

        def base1():
            r = mm([jnp.concatenate([p, t], axis=0) for p, t in zip(st["p1"], st["t0"])], bd(st["p1"]))
            st["p2"] = [x[:CHUNK] for x in r]
            st["t"] = [t + x[CHUNK:] for t, x in zip(st["t0"], r)]

        def base2():
            st["t"] = [t + z for t, z in zip(st["t"], mm(st["t"], bd(st["p2"])))]

        def merge_a(ls):
            off = ((row >> (ls + 1)) == (col >> (ls + 1))) & ((row >> ls) == (col >> ls) + 1)
            st["y"] = mm([jnp.where(off, l, 0.0) for l in lmat], bd(st["t"]))

        def merge_b():
            st["t"] = [t - z for t, z in zip(st["t"], mm(st["t"], bd(st["y"])))]

        def apply(name, ref):
            for u, x in zip(units, mm(st["t"], bd([u[name] for u in units]))):
                ref[u["r"], u["l"]] = x.astype(BF16)

        stages = [base0, base1, base2]
        for ls in (3, 4, 5):
            stages += [functools.partial(merge_a, ls), merge_b]
        return stages + [functools.partial(apply, "vb", u_ref), functools.partial(apply, "kbg", w_ref)]

    groups = [list(range(c, min(c + WY_GROUP_CHUNKS, n_chunks))) for c in range(0, n_chunks, WY_GROUP_CHUNKS)]
    units = [u for ch in groups[0] for u in prepare(ch)]
    for gi in range(len(groups)):
        upcoming = groups[gi + 1] if gi + 1 < len(groups) else []
        stages = inverse_stages(units)
        per_stage = -(-len(upcoming) // len(stages))
        units = []
        for si, stage in enumerate(stages):
            stage()
            for ch in upcoming[si * per_stage:(si + 1) * per_stage]:
                units += prepare(ch)


def _gdn_wy_call(gq, gk, gv, small, g_t, *, rows):
    m = gq.shape[0]
    cpb = rows // CHUNK
    blk = pl.BlockSpec((rows, WIDTH), lambda i: (i, 0))
    rowblk = pl.BlockSpec((cpb, 1, WIDTH), lambda i: (i, 0, 0))
    bf = jax.ShapeDtypeStruct((m, WIDTH), BF16)
    return pl.pallas_call(
        _gdn_wy_kernel,
        grid=(m // rows,),
        in_specs=[blk] * 3 + [pl.BlockSpec((rows, N_SMALL), lambda i: (i, 0)),
                              pl.BlockSpec((HEADS, rows), lambda i: (0, i))],
        out_specs=[blk] * 5 + [rowblk],
        out_shape=[bf] * 5 + [jax.ShapeDtypeStruct((m // CHUNK, 1, WIDTH), F32)],
        compiler_params=pltpu.CompilerParams(dimension_semantics=("parallel",),
                                             vmem_limit_bytes=VMEM_LIMIT),
    )(gq, gk, gv, small, g_t)


SCAN_SEQS = 8


def _gdn_scan_kernel(n_cast, u_ref, w_ref, a_ref, qg_ref, kd_ref, dl_ref, gz_ref, nw_ref, ones_ref, *refs):
    cast_in, o_ref, cast_out = refs[:n_cast], refs[n_cast], refs[n_cast + 1:2 * n_cast + 1]
    s_ref, oacc_ref = refs[2 * n_cast + 1:]
    for src, dst in zip(cast_in, cast_out):
        dst[...] = src[...].astype(BF16)
    n_seq, rows, _ = u_ref.shape
    n_groups = WIDTH // MXU_DIM
    chains = [(b, g, slice(g * MXU_DIM, (g + 1) * MXU_DIM)) for b in range(n_seq) for g in range(n_groups)]
    lane_lo = _lane_lo_mask()
    lane_hi = jnp.logical_not(lane_lo)
    zero_half = jnp.zeros((HEAD_DIM, LANES), BF16)

    @pl.when(pl.program_id(1) == 0)
    def _():
        s_ref[...] = jnp.zeros_like(s_ref)

    def pair_lanes(h):
        return slice((h // 2) * LANES, (h // 2 + 1) * LANES)

    def as_block_diag(pieces):
        rows_ = [jnp.concatenate([p.astype(BF16), zero_half] if h < 2 else [zero_half, p.astype(BF16)], axis=1)
                 for h, p in enumerate(pieces)]
        return jnp.concatenate(rows_, axis=0)

    def body(c, carry):
        rs = pl.ds(pl.multiple_of(c * CHUNK, CHUNK), CHUNK)
        state = [[s_ref[b, g, h] for h in range(PACK)] for b, g, _ in chains]
        r = [jnp.dot(jnp.concatenate([w_ref[b, rs, l], qg_ref[b, rs, l]], axis=0),
                     as_block_diag(state[i]), preferred_element_type=F32) for i, (b, _, l) in enumerate(chains)]
        v_new = [u_ref[b, rs, l].astype(F32) - r[i][:CHUNK] for i, (b, _, l) in enumerate(chains)]
        upd = [lax.dot_general(kd_ref[b, rs, l], v_new[i].astype(BF16), (((0,), (0,)), ((), ())),
                               preferred_element_type=F32) for i, (b, _, l) in enumerate(chains)]
        for i, (b, g, l) in enumerate(chains):
            decay = dl_ref[b, c, :, l]
            for h in range(PACK):
                own = upd[i][h * HEAD_DIM:(h + 1) * HEAD_DIM, pair_lanes(h)]
                s_ref[b, g, h] = (state[i][h] * decay[:, pair_lanes(h)]
                                  + jnp.where(lane_lo if h % 2 == 0 else lane_hi, own, 0.0))
        for i, (b, _, l) in enumerate(chains):
            oacc_ref[b, rs, l] = r[i][CHUNK:] + jnp.dot(a_ref[b, rs, l], _block_diag(v_new[i], lane_lo),
                                                        preferred_element_type=F32)
        return carry

    lax.fori_loop(0, rows // CHUNK, body, 0)

    for b in range(n_seq):
        o = oacc_ref[b]
        ms = _head_sums(o * o, ones_ref[...]) * (1.0 / HEAD_DIM)
        o_ref[b] = (o * lax.rsqrt(ms + EPS) * nw_ref[...] * gz_ref[b].astype(F32)).astype(BF16)


def _gdn_scan_call(u, w, a, qg, kd, dl, gz, nw, ones_bd, weights_f32, *, seq, n_seq, rows):
    batch = u.shape[0] // seq
    steps = seq // rows
    as3d = lambda t: t.reshape(batch, seq, WIDTH)
    blk = pl.BlockSpec((n_seq, rows, WIDTH), lambda b, t: (b, t, 0))
    dl4 = dl.reshape(batch, seq // CHUNK, 1, WIDTH)
    slabs = [pl.BlockSpec((wt.shape[0] // steps, wt.shape[1]), lambda b, t: (t, 0)) for wt in weights_f32]
    assert all(wt.shape[0] % (16 * steps) == 0 for wt in weights_f32)
    outs = pl.pallas_call(
        functools.partial(_gdn_scan_kernel, len(weights_f32)),
        grid=(batch // n_seq, steps),
        in_specs=[blk] * 5 + [pl.BlockSpec((n_seq, rows // CHUNK, 1, WIDTH), lambda b, t: (b, t, 0, 0)), blk,
                              _const_spec(nw.shape), _const_spec(ones_bd.shape)] + slabs,
        out_specs=[blk] + slabs,
        out_shape=[jax.ShapeDtypeStruct((batch, seq, WIDTH), BF16)]
                  + [jax.ShapeDtypeStruct(wt.shape, BF16) for wt in weights_f32],
        scratch_shapes=[pltpu.VMEM((n_seq, WIDTH // MXU_DIM, PACK, HEAD_DIM, LANES), F32),
                        pltpu.VMEM((n_seq, rows, WIDTH), F32)],
        compiler_params=pltpu.CompilerParams(dimension_semantics=("arbitrary", "arbitrary"),
                                             vmem_limit_bytes=VMEM_LIMIT),
    )(as3d(u), as3d(w), as3d(a), as3d(qg), as3d(kd), dl4, as3d(gz), nw, ones_bd, *weights_f32)
    return outs[0].reshape(batch * seq, WIDTH), outs[1:]


FOX_TK = 256
VT_ROWS = HEAD_DIM + 16


def _fox_kernel(tk, q_ref, qx_ref, k_ref, kx_ref, v_ref, gate_ref, o_ref,
                vt_ref, kp_ref, s_ref, m_ref, acc_ref, qt_ref):
    slots = q_ref.shape[0] // tk
    tq = tk
    heads = range(HEADS)
    lane = lax.broadcasted_iota(jnp.int32, (tk, LANES), 1)

    def stage(t):
        rows = slice(t * tk, (t + 1) * tk)
        v_t = v_ref[rows, :].T
        for h in heads:
            vt_ref[h * VT_ROWS:h * VT_ROWS + HEAD_DIM, rows] = v_t[h * HEAD_DIM:(h + 1) * HEAD_DIM]
            vt_ref[h * VT_ROWS + HEAD_DIM:(h + 1) * VT_ROWS, rows] = jnp.ones((VT_ROWS - HEAD_DIM, tk), BF16)
        kx = kx_ref[rows, :].astype(F32)
        for p in range(HEADS // 2):
            pair = k_ref[rows, p * LANES:(p + 1) * LANES].astype(F32)
            for h in (2 * p, 2 * p + 1):
                own = pair if h % 2 == 0 else pltpu.roll(pair, HEAD_DIM, 1)
                ext = pltpu.roll(kx, (HEAD_DIM - EXT_STRIDE * h) % LANES, 1)
                blk = jnp.where(lane < HEAD_DIM, own, jnp.where(lane < HEAD_DIM + EXT_STRIDE, ext, 0.0))
                kp_ref[rows, h * LANES:(h + 1) * LANES] = blk.astype(BF16)
        qx_t = qx_ref[rows, :].T
        for h in heads:
            pair_t = q_ref[rows, (h // 2) * LANES:(h // 2 + 1) * LANES].T
            qt_ref[t, h] = jnp.concatenate(
                [pair_t[(h % 2) * HEAD_DIM:(h % 2 + 1) * HEAD_DIM], qx_t[EXT_STRIDE * h:EXT_STRIDE * (h + 1)],
                 jnp.zeros((LANES - HEAD_DIM - EXT_STRIDE, tq), BF16)], axis=0)
            m_ref[t, h] = jnp.full((1, tq), NEG_BIG, F32)
            acc_ref[t, h] = jnp.zeros((VT_ROWS, tq), F32)

    kv_pos = lax.broadcasted_iota(jnp.int32, (tk, tq), 0)
    q_pos = lax.broadcasted_iota(jnp.int32, (tk, tq), 1)

    def score(h, slot, j, masked):
        k0 = pl.multiple_of(j * tk, tk)
        s_h = jnp.dot(kp_ref[pl.ds(k0, tk), h * LANES:(h + 1) * LANES], qt_ref[slot, h],
                      preferred_element_type=F32)
        if masked:
            s_h = jnp.where(kv_pos <= q_pos, s_h, NEG_BIG)
        s_ref[h] = s_h

    def absorb(h, slot, j):
        k0 = pl.multiple_of(j * tk, tk)
        m_prev = m_ref[slot, h]
        s_h = s_ref[h]
        m_new = jnp.maximum(m_prev, jnp.max(s_h, axis=0, keepdims=True))
        p = jnp.exp2(s_h - m_new).astype(BF16)
        pv = jnp.dot(vt_ref[h * VT_ROWS:(h + 1) * VT_ROWS, pl.ds(k0, tk)], p, preferred_element_type=F32)
        m_ref[slot, h] = m_new
        acc_ref[slot, h] = jnp.exp2(m_prev - m_new) * acc_ref[slot, h] + pv

    stage(0)
    for h in heads:
        score(h, 0, 0, True)
    for slot in range(1, slots):
        stage(slot)
        for h in heads:
            absorb(h, slot - 1, slot - 1)
            score(h, slot, slot, True)

    def step(t, pending):
        p_slot, p_kv = pending
        n_slot = jnp.int32(0)
        n_kv = t
        for s in range(2, slots):
            start = (s * (s - 1)) // 2
            n_slot = jnp.where(t >= start, s, n_slot)
            n_kv = jnp.where(t >= start, t - start, n_kv)
        n_slot = jnp.maximum(n_slot, 1)
        for h in heads:
            absorb(h, p_slot, p_kv)
            score(h, n_slot, n_kv, False)
        return n_slot, n_kv

    n_off = (slots * (slots - 1)) // 2
    p_slot, p_kv = lax.fori_loop(0, n_off, step, (jnp.int32(slots - 1), jnp.int32(slots - 1)))
    for h in heads:
        absorb(h, p_slot, p_kv)
    for slot in range(slots):
        rows = slice(slot * tq, (slot + 1) * tq)
        o_t = jnp.concatenate([acc_ref[slot, h, 0:HEAD_DIM] / acc_ref[slot, h, HEAD_DIM:HEAD_DIM + 1]
                               for h in heads], axis=0)
        o_ref[rows, :] = (o_t.T * gate_ref[rows, :].astype(F32)).astype(BF16)


def _fox_call(fq, qx, fk, kx, fv, gate, *, seq, tk):
    m = fq.shape[0]
    n_tiles = seq // tk
    wide = pl.BlockSpec((seq, WIDTH), lambda b: (b, 0))
    narrow = pl.BlockSpec((seq, LANES), lambda b: (b, 0))
    return pl.pallas_call(
        functools.partial(_fox_kernel, tk),
        grid=(m // seq,),
        in_specs=[wide, narrow, wide, narrow, wide, wide],
        out_specs=wide,
        out_shape=jax.ShapeDtypeStruct((m, WIDTH), BF16),
        scratch_shapes=[pltpu.VMEM((HEADS * VT_ROWS, seq), BF16), pltpu.VMEM((seq, HEADS * LANES), BF16),
                        pltpu.VMEM((HEADS, tk, tk), F32),
                        pltpu.VMEM((n_tiles, HEADS, 1, tk), F32), pltpu.VMEM((n_tiles, HEADS, VT_ROWS, tk), F32),
                        pltpu.VMEM((n_tiles, HEADS, LANES, tk), BF16)],
        compiler_params=pltpu.CompilerParams(dimension_semantics=("parallel",),
                                             vmem_limit_bytes=VMEM_LIMIT),
    )(fq, qx, fk, kx, fv, gate)


FF_CHUNK = 4 * MXU_DIM
FF_SPLITS = tuple((s, min(FF_CHUNK, D_FF - s)) for s in range(0, D_FF, FF_CHUNK))


FFN_ROW_GROUPS = 2


def _ffn_kernel(final, mg_ref, mf_ref, x_ref, wo_ref, n2_ref, wg_ref, wu_ref, wd_ref, fn_ref, o_ref):
    rows = x_ref.shape[0] // FFN_ROW_GROUPS
    groups = [slice(r * rows, (r + 1) * rows) for r in range(FFN_ROW_GROUPS)]
    x1 = [x_ref[g, :]
          + jnp.dot(mg_ref[g, :], wo_ref[0:WIDTH, :], preferred_element_type=F32)
          + jnp.dot(mf_ref[g, :], wo_ref[WIDTH:2 * WIDTH, :], preferred_element_type=F32) for g in groups]
    h = [(v * lax.rsqrt(jnp.mean(v * v, axis=-1, keepdims=True) + EPS) * n2_ref[...]).astype(BF16) for v in x1]
    ffn = [None] * len(groups)
    for start, size in FF_SPLITS:
        gate = [jnp.dot(hv, wg_ref[:, start:start + size], preferred_element_type=F32) for hv in h]
        up = [jnp.dot(hv, wu_ref[:, start:start + size], preferred_element_type=F32) for hv in h]
        act = [(gv * _sigmoid(gv) * uv).astype(BF16) for gv, uv in zip(gate, up)]
        down = [jnp.dot(av, wd_ref[start:start + size, :], preferred_element_type=F32) for av in act]
        ffn = [d if f is None else f + d for f, d in zip(ffn, down)]
    for g, v, f in zip(groups, x1, ffn):
        y = v + f
        if final:
            y = y * lax.rsqrt(jnp.mean(y * y, axis=-1, keepdims=True) + EPS) * fn_ref[...]
        o_ref[g, :] = y


def _ffn_call(mix_g, mix_f, x2, wo, n2w, wg, wu, wd, fnw, *, tm, final):
    m = x2.shape[0]
    half = pl.BlockSpec((tm, WIDTH), lambda i: (i, 0))
    full = pl.BlockSpec((tm, D_MODEL), lambda i: (i, 0))

    def resident(shape):
        return pl.BlockSpec(shape, lambda i: (0, 0), pipeline_mode=pl.Buffered(1))

    return pl.pallas_call(
        functools.partial(_ffn_kernel, final),
        grid=(m // tm,),
        in_specs=[half, half, full, resident(wo.shape), resident(n2w.shape), resident(wg.shape),
                  resident(wu.shape), resident(wd.shape), resident(fnw.shape)],
        out_specs=full,
        out_shape=jax.ShapeDtypeStruct((m, D_MODEL), F32),
        compiler_params=pltpu.CompilerParams(dimension_semantics=("parallel",),
                                             vmem_limit_bytes=VMEM_LIMIT),
    )(mix_g, mix_f, x2, wo, n2w, wg, wu, wd, fnw)


def _lane_vec(parts):
    v = jnp.concatenate([p.astype(F32).reshape(-1) for p in parts])
    return jnp.pad(v, (0, N_SMALL - v.shape[0])).reshape(1, N_SMALL)


def kernel(x, norm1_w, w_in, gdn_conv_w, gdn_A_log, gdn_dt_bias, gdn_out_norm_w, fox_f_bias, fox_q_norm_w, fox_k_norm_w, w_out, norm2_w, w_ffn_gate, w_ffn_up, w_ffn_down, final_norm_w):
    batch, seq, _ = x.shape
    m = batch * seq
    depth = norm1_w.shape[0]
    zeros8 = jnp.zeros((HEADS,), F32)
    r2 = jnp.arange(MXU_DIM) // HEAD_DIM
    ones_bd = (r2[:, None] == r2[None, :]).astype(BF16)

    x2 = x.reshape(m, D_MODEL)
    for l in range(depth):
        w = w_in[l].astype(BF16)
        big = 4 * WIDTH
        g_small = w[:, big:big + 2 * HEADS]
        f_start = big + 2 * HEADS
        f_small = w[:, f_start + big:f_start + big + HEADS]
        w_gdn = w[:, :big]
        w_fox = w[:, f_start:f_start + big]
        w_gates = jnp.concatenate(
            [g_small, f_small, jnp.zeros((D_MODEL, N_SMALL - 3 * HEADS), BF16)], axis=1)
        bias_vec = _lane_vec([zeros8, gdn_dt_bias[l], fox_f_bias[l]])
        alog_vec = _lane_vec([zeros8, gdn_A_log[l], zeros8])
        fqw = (jnp.tile(fox_q_norm_w[l].astype(F32), HEADS) * (HEAD_DIM ** -0.5 * LOG2E)).reshape(1, WIDTH)
        fkw = jnp.tile(fox_k_norm_w[l].astype(F32), HEADS).reshape(1, WIDTH)

        gq, gk, gv, gz, fq, fk, fv, fg, small, kx, qx, g_t = _inproj_call(
            x2, norm1_w[l].reshape(1, D_MODEL), w_gdn, w_fox, w_gates, gdn_conv_w[l].astype(F32), bias_vec, alog_vec,
            fqw, fkw, ones_bd, seq=seq, tm=512)

        u, wmat, amat, qg, kd, dl = _gdn_wy_call(gq, gk, gv, small, g_t, rows=2048)
        nw = jnp.tile(gdn_out_norm_w[l].astype(F32), HEADS).reshape(1, WIDTH)
        mix_g, (wo, wg, wu, wd) = _gdn_scan_call(
            u, wmat, amat, qg, kd, dl, gz, nw, ones_bd,
            [w_out[l], w_ffn_gate[l], w_ffn_up[l], w_ffn_down[l]],
            seq=seq, n_seq=math.gcd(batch, SCAN_SEQS), rows=256)
        mix_f = _fox_call(fq, qx, fk, kx, fv, fg, seq=seq, tk=FOX_TK)

        x2 = _ffn_call(mix_g, mix_f, x2, wo, norm2_w[l].reshape(1, D_MODEL), wg, wu, wd,
                       final_norm_w.reshape(1, D_MODEL), tm=512, final=(l == depth - 1))
    return x2.reshape(batch, seq, D_MODEL)
```

```python
import functools
import math

import numpy as np
import jax
import jax.numpy as jnp
from jax import lax
from jax.experimental import pallas as pl
from jax.experimental.pallas import tpu as pltpu

D_MODEL = 1024
HEADS = 8
HEAD_DIM = 64
WIDTH = HEADS * HEAD_DIM
CONV_K = 4
CHUNK = 64
D_FF = 2816
EPS = 1e-6

LANES = 128
MXU_DIM = 256
PACK = MXU_DIM // HEAD_DIM
N_SMALL = LANES
VMEM_LIMIT = 56 * 1024 * 1024

F32 = jnp.float32
BF16 = jnp.bfloat16
NEG_BIG = -1e30
LOG2E = 1.4426950408889634
EXT_STRIDE = 16


def _const_spec(shape):
    nd = len(shape)
    return pl.BlockSpec(shape, lambda *_: (0,) * nd)


def _sigmoid(x):
    return 1.0 / (1.0 + jnp.exp(-x))


def _head_sums(y2, ones_bd):
    yb = y2.astype(BF16)
    parts = [jnp.dot(yb[:, c * MXU_DIM:(c + 1) * MXU_DIM], ones_bd, preferred_element_type=F32)
             for c in range(WIDTH // MXU_DIM)]
    return jnp.concatenate(parts, axis=1)


def _seg_cumsum(v, pos, longest):
    d = 1
    while d < longest:
        v = v + jnp.where(pos >= d, pltpu.roll(v, d, 0), 0.0)
        d *= 2
    return v


INPROJ_ROW_GROUPS = 2


def _inproj_kernel(tiles_per_seq, x_ref, n1_ref, wg_ref, wf_ref, ws_ref, cw_ref, bias_ref, alog_ref, fqw_ref, fkw_ref,
                   ones_ref, sel_ref, one_ref,
                   gq_ref, gk_ref, gv_ref, gz_ref, fq_ref, fk_ref, fv_ref, fg_ref, sm_ref, kx_ref, qx_ref, gt_ref,
                   tail_ref, carry_ref):
    i = pl.program_id(0)
    tm = x_ref.shape[0]

    @pl.when(i % tiles_per_seq == 0)
    def _():
        tail_ref[:, 0:8, :] = jnp.zeros((3, 8, WIDTH), F32)
        carry_ref[...] = jnp.zeros_like(carry_ref)

    ones_bd = ones_ref[...]
    rows = tm // INPROJ_ROW_GROUPS
    groups = [slice(g * rows, (g + 1) * rows) for g in range(INPROJ_ROW_GROUPS)]

    def project(g):
        x = x_ref[g, :]
        h = (x * lax.rsqrt(jnp.mean(x * x, axis=-1, keepdims=True) + EPS) * n1_ref[...]).astype(BF16)
        y_gates = jnp.dot(h, ws_ref[...], preferred_element_type=F32)
        y_gdn = jnp.dot(h, wg_ref[...], preferred_element_type=F32)
        y_fox = jnp.dot(h, wf_ref[...], preferred_element_type=F32)
        return ([y_gdn[:, s * WIDTH:(s + 1) * WIDTH] for s in range(4)]
                + [y_fox[:, s * WIDTH:(s + 1) * WIDTH] for s in range(4)] + [y_gates])

    def epilogue(g, ys, last):
        for s, out_ref in enumerate((gq_ref, gk_ref, gv_ref)):
            y = ys[s]
            cw = cw_ref[:, s * WIDTH:(s + 1) * WIDTH]
            tail_ref[s, 8 + g.start:8 + g.stop] = y
            acc = y * cw[CONV_K - 1:CONV_K]
            for k in range(1, CONV_K):
                acc = acc + tail_ref[s, 8 + g.start - k:8 + g.stop - k] * cw[CONV_K - 1 - k:CONV_K - k]
            if last:
                tail_ref[s, 0:8] = y[rows - 8:]
            act = acc * _sigmoid(acc)
            if s < 2:
                act = act * lax.rsqrt(_head_sums(act * act, ones_bd) + EPS)
            out_ref[g, :] = act.astype(BF16)

        z = ys[3]
        gz_ref[g, :] = (z * _sigmoid(z)).astype(BF16)

        for s, out_ref, w_norm in ((4, fq_ref, fqw_ref), (5, fk_ref, fkw_ref)):
            y = ys[s]
            ms = _head_sums(y * y, ones_bd) * (1.0 / HEAD_DIM)
            out_ref[g, :] = (y * lax.rsqrt(ms + EPS) * w_norm[...]).astype(BF16)

        fv_ref[g, :] = ys[6].astype(BF16)
        fg_ref[g, :] = _sigmoid(ys[7]).astype(BF16)

        t = ys[8] + bias_ref[...]
        lane = lax.broadcasted_iota(jnp.int32, t.shape, 1)
        beta = _sigmoid(t)
        tail = jnp.log1p(jnp.exp(-jnp.abs(t)))
        g_log = -jnp.exp(alog_ref[...]) * (jnp.maximum(t, 0.0) + tail)
        log_f = -(jnp.maximum(-t, 0.0) + tail)
        row = lax.broadcasted_iota(jnp.int32, t.shape, 0)
        seg_pos = jnp.where(lane < 2 * HEADS, row & (CHUNK - 1), row)
        cum = _seg_cumsum(jnp.where(lane < 2 * HEADS, g_log, log_f), seg_pos, rows)
        f_cum = cum + carry_ref[0:1, :]
        carry_ref[0:1, :] = f_cum[rows - 1:rows, :]
        sm_ref[g, :] = jnp.where(lane < HEADS, beta, jnp.where(lane < 2 * HEADS, cum, f_cum))
        gt_ref[:, g] = cum.T[HEADS:2 * HEADS, :]

        f2 = f_cum * LOG2E
        hi = f2.astype(BF16)
        r1 = f2 - hi.astype(F32)
        mid = r1.astype(BF16)
        lo = (r1 - mid.astype(F32)).astype(BF16)
        ext = (jnp.dot(jnp.concatenate([hi, mid, lo], axis=1), sel_ref[...], preferred_element_type=F32)
               + one_ref[...])
        kx_ref[g, :] = ext[:, :LANES].astype(BF16)
        qx_ref[g, :] = ext[:, LANES:].astype(BF16)

    projected = [project(g) for g in groups]
    for gi, g in enumerate(groups):
        epilogue(g, projected[gi], gi == len(groups) - 1)


def _bias_selectors():
    sel = np.zeros((3 * LANES, 2 * LANES), np.float32)
    one = np.zeros((1, 2 * LANES), np.float32)
    for h in range(HEADS):
        src = 2 * HEADS + h
        for part in range(3):
            sel[part * LANES + src, EXT_STRIDE * h + part] = -1.0
            sel[part * LANES + src, LANES + EXT_STRIDE * h + 3 + part] = 1.0
            one[0, EXT_STRIDE * h + 3 + part] = 1.0
            one[0, LANES + EXT_STRIDE * h + part] = 1.0
    return jnp.asarray(sel, BF16), jnp.asarray(one, F32)


def _inproj_call(x2, n1w, w_gdn, w_fox, w_gates, conv_w, bias_vec, alog_vec, fqw, fkw, ones_bd, *, seq, tm):
    m = x2.shape[0]
    assert seq % tm == 0 and tm % CHUNK == 0 and (tm & (tm - 1)) == 0
    sel, one = _bias_selectors()
    wide = pl.BlockSpec((tm, WIDTH), lambda i: (i, 0))
    narrow = pl.BlockSpec((tm, LANES), lambda i: (i, 0))
    out_shape = ([jax.ShapeDtypeStruct((m, WIDTH), BF16)] * 8 + [jax.ShapeDtypeStruct((m, N_SMALL), F32)]
                 + [jax.ShapeDtypeStruct((m, LANES), BF16)] * 2 + [jax.ShapeDtypeStruct((HEADS, m), F32)])
    return pl.pallas_call(
        functools.partial(_inproj_kernel, seq // tm),
        grid=(m // tm,),
        in_specs=[pl.BlockSpec((tm, D_MODEL), lambda i: (i, 0)),
                  _const_spec(n1w.shape), _const_spec(w_gdn.shape), _const_spec(w_fox.shape),
                  _const_spec(w_gates.shape), _const_spec(conv_w.shape),
                  _const_spec(bias_vec.shape), _const_spec(alog_vec.shape),
                  _const_spec(fqw.shape), _const_spec(fkw.shape), _const_spec(ones_bd.shape),
                  _const_spec(sel.shape), _const_spec(one.shape)],
        out_specs=[wide] * 8 + [narrow] * 3 + [pl.BlockSpec((HEADS, tm), lambda i: (0, i))],
        out_shape=out_shape,
        scratch_shapes=[pltpu.VMEM((3, tm + 8, WIDTH), F32), pltpu.VMEM((8, N_SMALL), F32)],
        compiler_params=pltpu.CompilerParams(dimension_semantics=("arbitrary",),
                                             vmem_limit_bytes=VMEM_LIMIT),
    )(x2, n1w, w_gdn, w_fox, w_gates, conv_w, bias_vec, alog_vec, fqw, fkw, ones_bd, sel, one)


def _lane_lo_mask():
    return lax.broadcasted_iota(jnp.int32, (CHUNK, LANES), 1) < HEAD_DIM


def _block_diag(x, lane_lo):
    xb = x.astype(BF16)
    zero = jnp.zeros((CHUNK, LANES), BF16)
    blocks = []
    for h in range(PACK):
        half = xb[:, (h // 2) * LANES:(h // 2 + 1) * LANES]
        keep = jnp.where(lane_lo if h % 2 == 0 else jnp.logical_not(lane_lo), half, zero)
        blocks.append(jnp.concatenate([keep, zero] if h < 2 else [zero, keep], axis=1))
    return jnp.concatenate(blocks, axis=0)


def _mm(lhs, rhs_bd):
    return jnp.dot(lhs.astype(BF16), rhs_bd, preferred_element_type=F32)


def _expand_heads(sm, first_lane):
    lane_lo = lax.broadcasted_iota(jnp.int32, (sm.shape[0], LANES), 1) < HEAD_DIM
    parts = []
    for p in range(HEADS // 2):
        c = first_lane + 2 * p
        even = jnp.broadcast_to(sm[:, c:c + 1], (sm.shape[0], LANES))
        odd = jnp.broadcast_to(sm[:, c + 1:c + 2], (sm.shape[0], LANES))
        parts.append(jnp.where(lane_lo, even, odd))
    return jnp.concatenate(parts, axis=1)


WY_GROUP_CHUNKS = 8


def _gdn_wy_kernel(q_ref, k_ref, v_ref, sm_ref, gt_ref,
                   u_ref, w_ref, a_ref, qg_ref, kd_ref, dl_ref):
    n_chunks = q_ref.shape[0] // CHUNK
    n_groups = WIDTH // MXU_DIM
    shape = (CHUNK, MXU_DIM)
    row = lax.broadcasted_iota(jnp.int32, shape, 0)
    col = lax.broadcasted_iota(jnp.int32, shape, 1) & (HEAD_DIM - 1)
    causal = row >= col
    strict = row > col
    eye = (row == col).astype(F32)
    bd_mask = _lane_lo_mask()
    scale = HEAD_DIM ** -0.5

    bd = lambda xs: [_block_diag(x, bd_mask) for x in xs]
    mm = lambda ls, rs_: [_mm(a, b) for a, b in zip(ls, rs_)]

    def key_row(ch, g):
        return jnp.concatenate([gt_ref[g * PACK + h:g * PACK + h + 1, ch * CHUNK:(ch + 1) * CHUNK]
                                for h in range(PACK)], axis=1)

    def prepare(ch):
        r = slice(ch * CHUNK, (ch + 1) * CHUNK)
        sm = sm_ref[r, :]
        bx_all = _expand_heads(sm, 0)
        gx_all = _expand_heads(sm, HEADS)
        out = []
        for g in range(n_groups):
            l = slice(g * MXU_DIM, (g + 1) * MXU_DIM)
            q = q_ref[r, l].astype(F32)
            k = k_ref[r, l].astype(F32)
            bx, gx = bx_all[:, l], gx_all[:, l]
            kb = k * bx
            decay = jnp.exp(jnp.where(causal, gx - key_row(ch, g), NEG_BIG))
            s1 = lax.dot_general(jnp.concatenate([kb, q], axis=0).astype(BF16), _block_diag(k, bd_mask),
                                 (((1,), (1,)), ((), ())), preferred_element_type=F32)
            a_ref[r, l] = (s1[CHUNK:] * decay * scale).astype(BF16)
            eg = jnp.exp(gx)
            g_last = gx[CHUNK - 1:CHUNK, :]
            qg_ref[r, l] = (q * eg * scale).astype(BF16)
            kd_ref[r, l] = (k * jnp.exp(g_last - gx)).astype(BF16)
            dl_ref[ch, :, l] = jnp.exp(g_last)
            out.append(dict(r=r, l=l, lmat=jnp.where(strict, s1[:CHUNK] * decay, 0.0),
                            vb=v_ref[r, l].astype(F32) * bx, kbg=kb * eg))
        return out

    def inverse_stages(units):
        st = {}
        lmat = [u["lmat"] for u in units]
        blk8 = (row >> 3) == (col >> 3)

        def base0():
            st["n8"] = [jnp.where(blk8, -l, 0.0) for l in lmat]
            st["t0"] = [eye + n for n in st["n8"]]
            st["p1"] = mm(st["n8"], bd(st["n8"]))

        def base1():
            r = mm([jnp.concatenate([p, t], axis=0) for p, t in zip(st["p1"], st["t0"])], bd(st["p1"]))
            st["p2"] = [x[:CHUNK] for x in r]
            st["t"] = [t + x[CHUNK:] for t, x in zip(st["t0"], r)]

        def base2():
            st["t"] = [t + z for t, z in zip(st["t"], mm(st["t"], bd(st["p2"])))]

        def merge_a(ls):
            off = ((row >> (ls + 1)) == (col >> (ls + 1))) & ((row >> ls) == (col >> ls) + 1)
            st["y"] = mm([jnp.where(off, l, 0.0) for l in lmat], bd(st["t"]))

        def merge_b():
            st["t"] = [t - z for t, z in zip(st["t"], mm(st["t"], bd(st["y"])))]

        def apply(name, ref):
            for u, x in zip(units, mm(st["t"], bd([u[name] for u in units]))):
                ref[u["r"], u["l"]] = x.astype(BF16)

        stages = [base0, base1, base2]
        for ls in (3, 4, 5):
            stages += [functools.partial(merge_a, ls), merge_b]
        return stages + [functools.partial(apply, "vb", u_ref), functools.partial(apply, "kbg", w_ref)]

    groups = [list(range(c, min(c + WY_GROUP_CHUNKS, n_chunks))) for c in range(0, n_chunks, WY_GROUP_CHUNKS)]
    units = [u for ch in groups[0] for u in prepare(ch)]
    for gi in range(len(groups)):
        upcoming = groups[gi + 1] if gi + 1 < len(groups) else []
        stages = inverse_stages(units)
        per_stage = -(-len(upcoming) // len(stages))
        units = []
        for si, stage in enumerate(stages):
            stage()
            for ch in upcoming[si * per_stage:(si + 1) * per_stage]:
                units += prepare(ch)


def _gdn_wy_call(gq, gk, gv, small, g_t, *, rows):
    m = gq.shape[0]
    cpb = rows // CHUNK
    blk = pl.BlockSpec((rows, WIDTH), lambda i: (i, 0))
    rowblk = pl.BlockSpec((cpb, 1, WIDTH), lambda i: (i, 0, 0))
    bf = jax.ShapeDtypeStruct((m, WIDTH), BF16)
    return pl.pallas_call(
        _gdn_wy_kernel,
        grid=(m // rows,),
        in_specs=[blk] * 3 + [pl.BlockSpec((rows, N_SMALL), lambda i: (i, 0)),
                              pl.BlockSpec((HEADS, rows), lambda i: (0, i))],
        out_specs=[blk] * 5 + [rowblk],
        out_shape=[bf] * 5 + [jax.ShapeDtypeStruct((m // CHUNK, 1, WIDTH), F32)],
        compiler_params=pltpu.CompilerParams(dimension_semantics=("parallel",),
                                             vmem_limit_bytes=VMEM_LIMIT),
    )(gq, gk, gv, small, g_t)


SCAN_SEQS = 8


def _gdn_scan_kernel(n_cast, u_ref, w_ref, a_ref, qg_ref, kd_ref, dl_ref, gz_ref, nw_ref, ones_ref, *refs):
    cast_in, o_ref, cast_out = refs[:n_cast], refs[n_cast], refs[n_cast + 1:2 * n_cast + 1]
    s_ref, oacc_ref = refs[2 * n_cast + 1:]
    for src, dst in zip(cast_in, cast_out):
        dst[...] = src[...].astype(BF16)
    n_seq, rows, _ = u_ref.shape
    n_groups = WIDTH // MXU_DIM
    chains = [(b, g, slice(g * MXU_DIM, (g + 1) * MXU_DIM)) for b in range(n_seq) for g in range(n_groups)]
    lane_lo = _lane_lo_mask()
    lane_hi = jnp.logical_not(lane_lo)
    zero_half = jnp.zeros((HEAD_DIM, LANES), BF16)

    @pl.when(pl.program_id(1) == 0)
    def _():
        s_ref[...] = jnp.zeros_like(s_ref)

    def pair_lanes(h):
        return slice((h // 2) * LANES, (h // 2 + 1) * LANES)

    def as_block_diag(pieces):
        rows_ = [jnp.concatenate([p.astype(BF16), zero_half] if h < 2 else [zero_half, p.astype(BF16)], axis=1)
                 for h, p in enumerate(pieces)]
        return jnp.concatenate(rows_, axis=0)

    def body(c, carry):
        rs = pl.ds(pl.multiple_of(c * CHUNK, CHUNK), CHUNK)
        state = [[s_ref[b, g, h] for h in range(PACK)] for b, g, _ in chains]
        r = [jnp.dot(jnp.concatenate([w_ref[b, rs, l], qg_ref[b, rs, l]], axis=0),
                     as_block_diag(state[i]), preferred_element_type=F32) for i, (b, _, l) in enumerate(chains)]
        v_new = [u_ref[b, rs, l].astype(F32) - r[i][:CHUNK] for i, (b, _, l) in enumerate(chains)]
        upd = [lax.dot_general(kd_ref[b, rs, l], v_new[i].astype(BF16), (((0,), (0,)), ((), ())),
                               preferred_element_type=F32) for i, (b, _, l) in enumerate(chains)]
        for i, (b, g, l) in enumerate(chains):
            decay = dl_ref[b, c, :, l]
            for h in range(PACK):
                own = upd[i][h * HEAD_DIM:(h + 1) * HEAD_DIM, pair_lanes(h)]
                s_ref[b, g, h] = (state[i][h] * decay[:, pair_lanes(h)]
                                  + jnp.where(lane_lo if h % 2 == 0 else lane_hi, own, 0.0))
        for i, (b, _, l) in enumerate(chains):
            oacc_ref[b, rs, l] = r[i][CHUNK:] + jnp.dot(a_ref[b, rs, l], _block_diag(v_new[i], lane_lo),
                                                        preferred_element_type=F32)
        return carry

    lax.fori_loop(0, rows // CHUNK, body, 0)

    for b in range(n_seq):
        o = oacc_ref[b]
        ms = _head_sums(o * o, ones_ref[...]) * (1.0 / HEAD_DIM)
        o_ref[b] = (o * lax.rsqrt(ms + EPS) * nw_ref[...] * gz_ref[b].astype(F32)).astype(BF16)


def _gdn_scan_call(u, w, a, qg, kd, dl, gz, nw, ones_bd, weights_f32, *, seq, n_seq, rows):
    batch = u.shape[0] // seq
    steps = seq // rows
    as3d = lambda t: t.reshape(batch, seq, WIDTH)
    blk = pl.BlockSpec((n_seq, rows, WIDTH), lambda b, t: (b, t, 0))
    dl4 = dl.reshape(batch, seq // CHUNK, 1, WIDTH)
    slabs = [pl.BlockSpec((wt.shape[0] // steps, wt.shape[1]), lambda b, t: (t, 0)) for wt in weights_f32]
    assert all(wt.shape[0] % (16 * steps) == 0 for wt in weights_f32)
    outs = pl.pallas_call(
        functools.partial(_gdn_scan_kernel, len(weights_f32)),
        grid=(batch // n_seq, steps),
        in_specs=[blk] * 5 + [pl.BlockSpec((n_seq, rows // CHUNK, 1, WIDTH), lambda b, t: (b, t, 0, 0)), blk,
                              _const_spec(nw.shape), _const_spec(ones_bd.shape)] + slabs,
        out_specs=[blk] + slabs,
        out_shape=[jax.ShapeDtypeStruct((batch, seq, WIDTH), BF16)]
                  + [jax.ShapeDtypeStruct(wt.shape, BF16) for wt in weights_f32],
        scratch_shapes=[pltpu.VMEM((n_seq, WIDTH // MXU_DIM, PACK, HEAD_DIM, LANES), F32),
                        pltpu.VMEM((n_seq, rows, WIDTH), F32)],
        compiler_params=pltpu.CompilerParams(dimension_semantics=("arbitrary", "arbitrary"),
                                             vmem_limit_bytes=VMEM_LIMIT),
    )(as3d(u), as3d(w), as3d(a), as3d(qg), as3d(kd), dl4, as3d(gz), nw, ones_bd, *weights_f32)
    return outs[0].reshape(batch * seq, WIDTH), outs[1:]


FOX_TK = 256
VT_ROWS = HEAD_DIM + 16


def _fox_kernel(tk, q_ref, qx_ref, k_ref, kx_ref, v_ref, gate_ref, o_ref,
                vt_ref, kp_ref, s_ref, m_ref, acc_ref, qt_ref):
    slots = q_ref.shape[0] // tk
    tq = tk
    heads = range(HEADS)
    lane = lax.broadcasted_iota(jnp.int32, (tk, LANES), 1)

    def stage(t):
        rows = slice(t * tk, (t + 1) * tk)
        v_t = v_ref[rows, :].T
        for h in heads:
            vt_ref[h * VT_ROWS:h * VT_ROWS + HEAD_DIM, rows] = v_t[h * HEAD_DIM:(h + 1) * HEAD_DIM]
            vt_ref[h * VT_ROWS + HEAD_DIM:(h + 1) * VT_ROWS, rows] = jnp.ones((VT_ROWS - HEAD_DIM, tk), BF16)
        kx = kx_ref[rows, :].astype(F32)
        for p in range(HEADS // 2):
            pair = k_ref[rows, p * LANES:(p + 1) * LANES].astype(F32)
            for h in (2 * p, 2 * p + 1):
                own = pair if h % 2 == 0 else pltpu.roll(pair, HEAD_DIM, 1)
                ext = pltpu.roll(kx, (HEAD_DIM - EXT_STRIDE * h) % LANES, 1)
                blk = jnp.where(lane < HEAD_DIM, own, jnp.where(lane < HEAD_DIM + EXT_STRIDE, ext, 0.0))
                kp_ref[rows, h * LANES:(h + 1) * LANES] = blk.astype(BF16)
        qx_t = qx_ref[rows, :].T
        for h in heads:
            pair_t = q_ref[rows, (h // 2) * LANES:(h // 2 + 1) * LANES].T
            qt_ref[t, h] = jnp.concatenate(
                [pair_t[(h % 2) * HEAD_DIM:(h % 2 + 1) * HEAD_DIM], qx_t[EXT_STRIDE * h:EXT_STRIDE * (h + 1)],
                 jnp.zeros((LANES - HEAD_DIM - EXT_STRIDE, tq), BF16)], axis=0)
            m_ref[t, h] = jnp.full((1, tq), NEG_BIG, F32)
            acc_ref[t, h] = jnp.zeros((VT_ROWS, tq), F32)

    kv_pos = lax.broadcasted_iota(jnp.int32, (tk, tq), 0)
    q_pos = lax.broadcasted_iota(jnp.int32, (tk, tq), 1)

    def score(h, slot, j, masked):
        k0 = pl.multiple_of(j * tk, tk)
        s_h = jnp.dot(kp_ref[pl.ds(k0, tk), h * LANES:(h + 1) * LANES], qt_ref[slot, h],
                      preferred_element_type=F32)
        if masked:
            s_h = jnp.where(kv_pos <= q_pos, s_h, NEG_BIG)
        s_ref[h] = s_h

    def absorb(h, slot, j):
        k0 = pl.multiple_of(j * tk, tk)
        m_prev = m_ref[slot, h]
        s_h = s_ref[h]
        m_new = jnp.maximum(m_prev, jnp.max(s_h, axis=0, keepdims=True))
        p = jnp.exp2(s_h - m_new).astype(BF16)
        pv = jnp.dot(vt_ref[h * VT_ROWS:(h + 1) * VT_ROWS, pl.ds(k0, tk)], p, preferred_element_type=F32)
        m_ref[slot, h] = m_new
        acc_ref[slot, h] = jnp.exp2(m_prev - m_new) * acc_ref[slot, h] + pv

    stage(0)
    for h in heads:
        score(h, 0, 0, True)
    for slot in range(1, slots):
        stage(slot)
        for h in heads:
            absorb(h, slot - 1, slot - 1)
            score(h, slot, slot, True)

    def step(t, pending):
        p_slot, p_kv = pending
        n_slot = jnp.int32(0)
        n_kv = t
        for s in range(2, slots):
            start = (s * (s - 1)) // 2
            n_slot = jnp.where(t >= start, s, n_slot)
            n_kv = jnp.where(t >= start, t - start, n_kv)
        n_slot = jnp.maximum(n_slot, 1)
        for h in heads:
            absorb(h, p_slot, p_kv)
            score(h, n_slot, n_kv, False)
        return n_slot, n_kv

    n_off = (slots * (slots - 1)) // 2
    p_slot, p_kv = lax.fori_loop(0, n_off, step, (jnp.int32(slots - 1), jnp.int32(slots - 1)))
    for h in heads:
        absorb(h, p_slot, p_kv)
    for slot in range(slots):
        rows = slice(slot * tq, (slot + 1) * tq)
        o_t = jnp.concatenate([acc_ref[slot, h, 0:HEAD_DIM] / acc_ref[slot, h, HEAD_DIM:HEAD_DIM + 1]
                               for h in heads], axis=0)
        o_ref[rows, :] = (o_t.T * gate_ref[rows, :].astype(F32)).astype(BF16)


def _fox_call(fq, qx, fk, kx, fv, gate, *, seq, tk):
    m = fq.shape[0]
    n_tiles = seq // tk
    wide = pl.BlockSpec((seq, WIDTH), lambda b: (b, 0))
    narrow = pl.BlockSpec((seq, LANES), lambda b: (b, 0))
    return pl.pallas_call(
        functools.partial(_fox_kernel, tk),
        grid=(m // seq,),
        in_specs=[wide, narrow, wide, narrow, wide, wide],
        out_specs=wide,
        out_shape=jax.ShapeDtypeStruct((m, WIDTH), BF16),
        scratch_shapes=[pltpu.VMEM((HEADS * VT_ROWS, seq), BF16), pltpu.VMEM((seq, HEADS * LANES), BF16),
                        pltpu.VMEM((HEADS, tk, tk), F32),
                        pltpu.VMEM((n_tiles, HEADS, 1, tk), F32), pltpu.VMEM((n_tiles, HEADS, VT_ROWS, tk), F32),
                        pltpu.VMEM((n_tiles, HEADS, LANES, tk), BF16)],
        compiler_params=pltpu.CompilerParams(dimension_semantics=("parallel",),
                                             vmem_limit_bytes=VMEM_LIMIT),
    )(fq, qx, fk, kx, fv, gate)


FF_CHUNK = 4 * MXU_DIM
FF_SPLITS = tuple((s, min(FF_CHUNK, D_FF - s)) for s in range(0, D_FF, FF_CHUNK))


FFN_ROW_GROUPS = 2


def _ffn_kernel(final, mg_ref, mf_ref, x_ref, wo_ref, n2_ref, wg_ref, wu_ref, wd_ref, fn_ref, o_ref):
    rows = x_ref.shape[0] // FFN_ROW_GROUPS
    groups = [slice(r * rows, (r + 1) * rows) for r in range(FFN_ROW_GROUPS)]
    x1 = [x_ref[g, :]
          + jnp.dot(mg_ref[g, :], wo_ref[0:WIDTH, :], preferred_element_type=F32)
          + jnp.dot(mf_ref[g, :], wo_ref[WIDTH:2 * WIDTH, :], preferred_element_type=F32) for g in groups]
    h = [(v * lax.rsqrt(jnp.mean(v * v, axis=-1, keepdims=True) + EPS) * n2_ref[...]).astype(BF16) for v in x1]
    ffn = [None] * len(groups)
    for start, size in FF_SPLITS:
        gate = [jnp.dot(hv, wg_ref[:, start:start + size], preferred_element_type=F32) for hv in h]
        up = [jnp.dot(hv, wu_ref[:, start:start + size], preferred_element_type=F32) for hv in h]
        act = [(gv * _sigmoid(gv) * uv).astype(BF16) for gv, uv in zip(gate, up)]
        down = [jnp.dot(av, wd_ref[start:start + size, :], preferred_element_type=F32) for av in act]
        ffn = [d if f is None else f + d for f, d in zip(ffn, down)]
    for g, v, f in zip(groups, x1, ffn):
        y = v + f
        if final:
            y = y * lax.rsqrt(jnp.mean(y * y, axis=-1, keepdims=True) + EPS) * fn_ref[...]
        o_ref[g, :] = y


def _ffn_call(mix_g, mix_f, x2, wo, n2w, wg, wu, wd, fnw, *, tm, final):
    m = x2.shape[0]
    half = pl.BlockSpec((tm, WIDTH), lambda i: (i, 0))
    full = pl.BlockSpec((tm, D_MODEL), lambda i: (i, 0))

    def resident(shape):
        return pl.BlockSpec(shape, lambda i: (0, 0), pipeline_mode=pl.Buffered(1))

    return pl.pallas_call(
        functools.partial(_ffn_kernel, final),
        grid=(m // tm,),
        in_specs=[half, half, full, resident(wo.shape), resident(n2w.shape), resident(wg.shape),
                  resident(wu.shape), resident(wd.shape), resident(fnw.shape)],
        out_specs=full,
        out_shape=jax.ShapeDtypeStruct((m, D_MODEL), F32),
        compiler_params=pltpu.CompilerParams(dimension_semantics=("parallel",),
                                             vmem_limit_bytes=VMEM_LIMIT),
    )(mix_g, mix_f, x2, wo, n2w, wg, wu, wd, fnw)


def _lane_vec(parts):
    v = jnp.concatenate([p.astype(F32).reshape(-1) for p in parts])
    return jnp.pad(v, (0, N_SMALL - v.shape[0])).reshape(1, N_SMALL)


def kernel(x, norm1_w, w_in, gdn_conv_w, gdn_A_log, gdn_dt_bias, gdn_out_norm_w, fox_f_bias, fox_q_norm_w, fox_k_norm_w, w_out, norm2_w, w_ffn_gate, w_ffn_up, w_ffn_down, final_norm_w):
    batch, seq, _ = x.shape
    m = batch * seq
    depth = norm1_w.shape[0]
    zeros8 = jnp.zeros((HEADS,), F32)
    r2 = jnp.arange(MXU_DIM) // HEAD_DIM
    ones_bd = (r2[:, None] == r2[None, :]).astype(BF16)

    x2 = x.reshape(m, D_MODEL)
    for l in range(depth):
        w = w_in[l].astype(BF16)
        big = 4 * WIDTH
        g_small = w[:, big:big + 2 * HEADS]
        f_start = big + 2 * HEADS
        f_small = w[:, f_start + big:f_start + big + HEADS]
        w_gdn = w[:, :big]
        w_fox = w[:, f_start:f_start + big]
        w_gates = jnp.concatenate(
            [g_small, f_small, jnp.zeros((D_MODEL, N_SMALL - 3 * HEADS), BF16)], axis=1)
        bias_vec = _lane_vec([zeros8, gdn_dt_bias[l], fox_f_bias[l]])
        alog_vec = _lane_vec([zeros8, gdn_A_log[l], zeros8])
        fqw = (jnp.tile(fox_q_norm_w[l].astype(F32), HEADS) * (HEAD_DIM ** -0.5 * LOG2E)).reshape(1, WIDTH)
        fkw = jnp.tile(fox_k_norm_w[l].astype(F32), HEADS).reshape(1, WIDTH)

        gq, gk, gv, gz, fq, fk, fv, fg, small, kx, qx, g_t = _inproj_call(
            x2, norm1_w[l].reshape(1, D_MODEL), w_gdn, w_fox, w_gates, gdn_conv_w[l].astype(F32), bias_vec, alog_vec,
            fqw, fkw, ones_bd, seq=seq, tm=512)

        u, wmat, amat, qg, kd, dl = _gdn_wy_call(gq, gk, gv, small, g_t, rows=2048)
        nw = jnp.tile(gdn_out_norm_w[l].astype(F32), HEADS).reshape(1, WIDTH)
        mix_g, (wo, wg, wu, wd) = _gdn_scan_call(
            u, wmat, amat, qg, kd, dl, gz, nw, ones_bd,
            [w_out[l], w_ffn_gate[l], w_ffn_up[l], w_ffn_down[l]],
            seq=seq, n_seq=math.gcd(batch, SCAN_SEQS), rows=256)
        mix_f = _fox_call(fq, qx, fk, kx, fv, fg, seq=seq, tk=FOX_TK)

        x2 = _ffn_call(mix_g, mix_f, x2, wo, norm2_w[l].reshape(1, D_MODEL), wg, wu, wd,
                       final_norm_w.reshape(1, D_MODEL), tm=512, final=(l == depth - 1))
    return x2.reshape(batch, seq, D_MODEL)
```

```python
import functools
import math

import numpy as np
import jax
import jax.numpy as jnp
from jax import lax
from jax.experimental import pallas as pl
from jax.experimental.pallas import tpu as pltpu

D_MODEL = 1024
HEADS = 8
HEAD_DIM = 64
WIDTH = HEADS * HEAD_DIM
CONV_K = 4
CHUNK = 64
D_FF = 2816
EPS = 1e-6

LANES = 128
MXU_DIM = 256
PACK = MXU_DIM // HEAD_DIM
N_SMALL = LANES
VMEM_LIMIT = 56 * 1024 * 1024

F32 = jnp.float32
BF16 = jnp.bfloat16
NEG_BIG = -1e30
LOG2E = 1.4426950408889634
EXT_STRIDE = 16


def _const_spec(shape):
    nd = len(shape)
    return pl.BlockSpec(shape, lambda *_: (0,) * nd, pipeline_mode=pl.Buffered(1))


def _sigmoid(x):
    return 1.0 / (1.0 + jnp.exp(-x))


def _head_sums(y2, ones_bd):
    yb = y2.astype(BF16)
    parts = [jnp.dot(yb[:, c * MXU_DIM:(c + 1) * MXU_DIM], ones_bd, preferred_element_type=F32)
             for c in range(WIDTH // MXU_DIM)]
    return jnp.concatenate(parts, axis=1)


def _seg_cumsum(v, pos, longest):
    d = 1
    while d < longest:
        v = v + jnp.where(pos >= d, pltpu.roll(v, d, 0), 0.0)
        d *= 2
    return v


INPROJ_ROW_GROUPS = 2


def _inproj_kernel(tiles_per_seq, x_ref, n1_ref, wg_ref, wf_ref, ws_ref, cw_ref, bias_ref, alog_ref, fqw_ref, fkw_ref,
                   ones_ref, sel_ref, one_ref,
                   gq_ref, gk_ref, gv_ref, gz_ref, fq_ref, fk_ref, fv_ref, fg_ref, sm_ref, kx_ref, qx_ref, gt_ref,
                   tail_ref, carry_ref):
    i = pl.program_id(0)
    tm = x_ref.shape[0]

    @pl.when(i % tiles_per_seq == 0)
    def _():
        tail_ref[:, 0:8, :] = jnp.zeros((3, 8, WIDTH), F32)
        carry_ref[...] = jnp.zeros_like(carry_ref)

    ones_bd = ones_ref[...]
    rows = tm // INPROJ_ROW_GROUPS
    groups = [slice(g * rows, (g + 1) * rows) for g in range(INPROJ_ROW_GROUPS)]

    def project(g):
        x = x_ref[g, :]
        h = (x * lax.rsqrt(jnp.mean(x * x, axis=-1, keepdims=True) + EPS) * n1_ref[...]).astype(BF16)
        y_gates = jnp.dot(h, ws_ref[...], preferred_element_type=F32)
        y_gdn = jnp.dot(h, wg_ref[...], preferred_element_type=F32)
        y_fox = jnp.dot(h, wf_ref[...], preferred_element_type=F32)
        return ([y_gdn[:, s * WIDTH:(s + 1) * WIDTH] for s in range(4)]
                + [y_fox[:, s * WIDTH:(s + 1) * WIDTH] for s in range(4)] + [y_gates])

    def epilogue(g, ys, last):
        for s, out_ref in enumerate((gq_ref, gk_ref, gv_ref)):
            y = ys[s]
            cw = cw_ref[:, s * WIDTH:(s + 1) * WIDTH]
            tail_ref[s, 8 + g.start:8 + g.stop] = y
            acc = y * cw[CONV_K - 1:CONV_K]
            for k in range(1, CONV_K):
                acc = acc + tail_ref[s, 8 + g.start - k:8 + g.stop - k] * cw[CONV_K - 1 - k:CONV_K - k]
            if last:
                tail_ref[s, 0:8] = y[rows - 8:]
            act = acc * _sigmoid(acc)
            if s < 2:
                act = act * lax.rsqrt(_head_sums(act * act, ones_bd) + EPS)
            out_ref[g, :] = act.astype(BF16)

        z = ys[3]
        gz_ref[g, :] = (z * _sigmoid(z)).astype(BF16)

        for s, out_ref, w_norm in ((4, fq_ref, fqw_ref), (5, fk_ref, fkw_ref)):
            y = ys[s]
            ms = _head_sums(y * y, ones_bd) * (1.0 / HEAD_DIM)
            out_ref[g, :] = (y * lax.rsqrt(ms + EPS) * w_norm[...]).astype(BF16)

        fv_ref[g, :] = ys[6].astype(BF16)
        fg_ref[g, :] = _sigmoid(ys[7]).astype(BF16)

        t = ys[8] + bias_ref[...]
        lane = lax.broadcasted_iota(jnp.int32, t.shape, 1)
        beta = _sigmoid(t)
        tail = jnp.log1p(jnp.exp(-jnp.abs(t)))
        g_log = -jnp.exp(alog_ref[...]) * (jnp.maximum(t, 0.0) + tail)
        log_f = -(jnp.maximum(-t, 0.0) + tail)
        row = lax.broadcasted_iota(jnp.int32, t.shape, 0)
        seg_pos = jnp.where(lane < 2 * HEADS, row & (CHUNK - 1), row)
        cum = _seg_cumsum(jnp.where(lane < 2 * HEADS, g_log, log_f), seg_pos, rows)
        f_cum = cum + carry_ref[0:1, :]
        carry_ref[0:1, :] = f_cum[rows - 1:rows, :]
        sm_ref[g, :] = jnp.where(lane < HEADS, beta, jnp.where(lane < 2 * HEADS, cum, f_cum))
        gt_ref[:, g] = cum.T[HEADS:2 * HEADS, :]

        f2 = f_cum * LOG2E
        hi = f2.astype(BF16)
        r1 = f2 - hi.astype(F32)
        mid = r1.astype(BF16)
        lo = (r1 - mid.astype(F32)).astype(BF16)
        ext = (jnp.dot(jnp.concatenate([hi, mid, lo], axis=1), sel_ref[...], preferred_element_type=F32)
               + one_ref[...])
        kx_ref[g, :] = ext[:, :LANES].astype(BF16)
        qx_ref[g, :] = ext[:, LANES:].astype(BF16)

    projected = [project(g) for g in groups]
    for gi, g in enumerate(groups):
        epilogue(g, projected[gi], gi == len(groups) - 1)


def _bias_selectors():
    sel = np.zeros((3 * LANES, 2 * LANES), np.float32)
    one = np.zeros((1, 2 * LANES), np.float32)
    for h in range(HEADS):
        src = 2 * HEADS + h
        for part in range(3):
            sel[part * LANES + src, EXT_STRIDE * h + part] = -1.0
            sel[part * LANES + src, LANES + EXT_STRIDE * h + 3 + part] = 1.0
            one[0, EXT_STRIDE * h + 3 + part] = 1.0
            one[0, LANES + EXT_STRIDE * h + part] = 1.0
    return jnp.asarray(sel, BF16), jnp.asarray(one, F32)


def _inproj_call(x2, n1w, w_gdn, w_fox, w_gates, conv_w, bias_vec, alog_vec, fqw, fkw, ones_bd, *, seq, tm):
    m = x2.shape[0]
    assert seq % tm == 0 and tm % CHUNK == 0 and (tm & (tm - 1)) == 0
    sel, one = _bias_selectors()
    wide = pl.BlockSpec((tm, WIDTH), lambda i: (i, 0))
    narrow = pl.BlockSpec((tm, LANES), lambda i: (i, 0))
    out_shape = ([jax.ShapeDtypeStruct((m, WIDTH), BF16)] * 8 + [jax.ShapeDtypeStruct((m, N_SMALL), F32)]
                 + [jax.ShapeDtypeStruct((m, LANES), BF16)] * 2 + [jax.ShapeDtypeStruct((HEADS, m), F32)])
    return pl.pallas_call(
        functools.partial(_inproj_kernel, seq // tm),
        grid=(m // tm,),
        in_specs=[pl.BlockSpec((tm, D_MODEL), lambda i: (i, 0)),
                  _const_spec(n1w.shape), _const_spec(w_gdn.shape), _const_spec(w_fox.shape),
                  _const_spec(w_gates.shape), _const_spec(conv_w.shape),
                  _const_spec(bias_vec.shape), _const_spec(alog_vec.shape),
                  _const_spec(fqw.shape), _const_spec(fkw.shape), _const_spec(ones_bd.shape),
                  _const_spec(sel.shape), _const_spec(one.shape)],
        out_specs=[wide] * 8 + [narrow] * 3 + [pl.BlockSpec((HEADS, tm), lambda i: (0, i))],
        out_shape=out_shape,
        scratch_shapes=[pltpu.VMEM((3, tm + 8, WIDTH), F32), pltpu.VMEM((8, N_SMALL), F32)],
        compiler_params=pltpu.CompilerParams(dimension_semantics=("arbitrary",),
                                             vmem_limit_bytes=VMEM_LIMIT),
    )(x2, n1w, w_gdn, w_fox, w_gates, conv_w, bias_vec, alog_vec, fqw, fkw, ones_bd, sel, one)


def _lane_lo_mask():
    return lax.broadcasted_iota(jnp.int32, (CHUNK, LANES), 1) < HEAD_DIM


def _block_diag(x, lane_lo):
    xb = x.astype(BF16)
    zero = jnp.zeros((CHUNK, LANES), BF16)
    blocks = []
    for h in range(PACK):
        half = xb[:, (h // 2) * LANES:(h // 2 + 1) * LANES]
        keep = jnp.where(lane_lo if h % 2 == 0 else jnp.logical_not(lane_lo), half, zero)
        blocks.append(jnp.concatenate([keep, zero] if h < 2 else [zero, keep], axis=1))
    return jnp.concatenate(blocks, axis=0)


def _mm(lhs, rhs_bd):
    return jnp.dot(lhs.astype(BF16), rhs_bd, preferred_element_type=F32)


def _expand_heads(sm, first_lane):
    lane_lo = lax.broadcasted_iota(jnp.int32, (sm.shape[0], LANES), 1) < HEAD_DIM
    parts = []
    for p in range(HEADS // 2):
        c = first_lane + 2 * p
        even = jnp.broadcast_to(sm[:, c:c + 1], (sm.shape[0], LANES))
        odd = jnp.broadcast_to(sm[:, c + 1:c + 2], (sm.shape[0], LANES))
        parts.append(jnp.where(lane_lo, even, odd))
    return jnp.concatenate(parts, axis=1)


WY_GROUP_CHUNKS = 8


def _gdn_wy_kernel(q_ref, k_ref, v_ref, sm_ref, gt_ref,
                   u_ref, w_ref, a_ref, qg_ref, kd_ref, dl_ref):
    n_chunks = q_ref.shape[0] // CHUNK
    n_groups = WIDTH // MXU_DIM
    shape = (CHUNK, MXU_DIM)
    row = lax.broadcasted_iota(jnp.int32, shape, 0)
    col = lax.broadcasted_iota(jnp.int32, shape, 1) & (HEAD_DIM - 1)
    causal = row >= col
    strict = row > col
    eye = (row == col).astype(F32)
    bd_mask = _lane_lo_mask()
    scale = HEAD_DIM ** -0.5

    bd = lambda xs: [_block_diag(x, bd_mask) for x in xs]
    mm = lambda ls, rs_: [_mm(a, b) for a, b in zip(ls, rs_)]

    def key_row(ch, g):
        return jnp.concatenate([gt_ref[g * PACK + h:g * PACK + h + 1, ch * CHUNK:(ch + 1) * CHUNK]
                                for h in range(PACK)], axis=1)

    def prepare(ch):
        r = slice(ch * CHUNK, (ch + 1) * CHUNK)
        sm = sm_ref[r, :]
        bx_all = _expand_heads(sm, 0)
        gx_all = _expand_heads(sm, HEADS)
        out = []
        for g in range(n_groups):
            l = slice(g * MXU_DIM, (g + 1) * MXU_DIM)
            q = q_ref[r, l].astype(F32)
            k = k_ref[r, l].astype(F32)
            bx, gx = bx_all[:, l], gx_all[:, l]
            kb = k * bx
            decay = jnp.exp(jnp.where(causal, gx - key_row(ch, g), NEG_BIG))
            s1 = lax.dot_general(jnp.concatenate([kb, q], axis=0).astype(BF16), _block_diag(k, bd_mask),
                                 (((1,), (1,)), ((), ())), preferred_element_type=F32)
            a_ref[r, l] = (s1[CHUNK:] * decay * scale).astype(BF16)
            eg = jnp.exp(gx)
            g_last = gx[CHUNK - 1:CHUNK, :]
            qg_ref[r, l] = (q * eg * scale).astype(BF16)
            kd_ref[r, l] = (k * jnp.exp(g_last - gx)).astype(BF16)
            dl_ref[ch, :, l] = jnp.exp(g_last)
            out.append(dict(r=r, l=l, lmat=jnp.where(strict, s1[:CHUNK] * decay, 0.0),
                            vb=v_ref[r, l].astype(F32) * bx, kbg=kb * eg))
        return out

    def inverse_stages(units):
        st = {}
        lmat = [u["lmat"] for u in units]
        blk8 = (row >> 3) == (col >> 3)

        def base0():
            st["n8"] = [jnp.where(blk8, -l, 0.0) for l in lmat]
            st["t0"] = [eye + n for n in st["n8"]]
            st["p1"] = mm(st["n8"], bd(st["n8"]))

        def base1():
            r = mm([jnp.concatenate([p, t], axis=0) for p, t in zip(st["p1"], st["t0"])], bd(st["p1"]))
            st["p2"] = [x[:CHUNK] for x in r]
            st["t"] = [t + x[CHUNK:] for t, x in zip(st["t0"], r)]

        def base2():
            st["t"] = [t + z for t, z in zip(st["t"], mm(st["t"], bd(st["p2"])))]

        def merge_a(ls):
            off = ((row >> (ls + 1)) == (col >> (ls + 1))) & ((row >> ls) == (col >> ls) + 1)
            st["y"] = mm([jnp.where(off, l, 0.0) for l in lmat], bd(st["t"]))

        def merge_b():
            st["t"] = [t - z for t, z in zip(st["t"], mm(st["t"], bd(st["y"])))]

        def apply(name, ref):
            for u, x in zip(units, mm(st["t"], bd([u[name] for u in units]))):
                ref[u["r"], u["l"]] = x.astype(BF16)

        stages = [base0, base1, base2]
        for ls in (3, 4, 5):
            stages += [functools.partial(merge_a, ls), merge_b]
        return stages + [functools.partial(apply, "vb", u_ref), functools.partial(apply, "kbg", w_ref)]

    groups = [list(range(c, min(c + WY_GROUP_CHUNKS, n_chunks))) for c in range(0, n_chunks, WY_GROUP_CHUNKS)]
    units = [u for ch in groups[0] for u in prepare(ch)]
    for gi in range(len(groups)):
        upcoming = groups[gi + 1] if gi + 1 < len(groups) else []
        stages = inverse_stages(units)
        per_stage = -(-len(upcoming) // len(stages))
        units = []
        for si, stage in enumerate(stages):
            stage()
            for ch in upcoming[si * per_stage:(si + 1) * per_stage]:
                units += prepare(ch)


def _gdn_wy_call(gq, gk, gv, small, g_t, *, rows):
    m = gq.shape[0]
    cpb = rows // CHUNK
    blk = pl.BlockSpec((rows, WIDTH), lambda i: (i, 0))
    rowblk = pl.BlockSpec((cpb, 1, WIDTH), lambda i: (i, 0, 0))
    bf = jax.ShapeDtypeStruct((m, WIDTH), BF16)
    return pl.pallas_call(
        _gdn_wy_kernel,
        grid=(m // rows,),
        in_specs=[blk] * 3 + [pl.BlockSpec((rows, N_SMALL), lambda i: (i, 0)),
                              pl.BlockSpec((HEADS, rows), lambda i: (0, i))],
        out_specs=[blk] * 5 + [rowblk],
        out_shape=[bf] * 5 + [jax.ShapeDtypeStruct((m // CHUNK, 1, WIDTH), F32)],
        compiler_params=pltpu.CompilerParams(dimension_semantics=("parallel",),
                                             vmem_limit_bytes=VMEM_LIMIT),
    )(gq, gk, gv, small, g_t)


SCAN_SEQS = 8


def _gdn_scan_kernel(n_cast, u_ref, w_ref, a_ref, qg_ref, kd_ref, dl_ref, gz_ref, nw_ref, ones_ref, *refs):
    cast_in, o_ref, cast_out = refs[:n_cast], refs[n_cast], refs[n_cast + 1:2 * n_cast + 1]
    s_ref, oacc_ref = refs[2 * n_cast + 1:]
    for src, dst in zip(cast_in, cast_out):
        dst[...] = src[...].astype(BF16)
    n_seq, rows, _ = u_ref.shape
    n_groups = WIDTH // MXU_DIM
    chains = [(b, g, slice(g * MXU_DIM, (g + 1) * MXU_DIM)) for b in range(n_seq) for g in range(n_groups)]
    lane_lo = _lane_lo_mask()
    lane_hi = jnp.logical_not(lane_lo)
    zero_half = jnp.zeros((HEAD_DIM, LANES), BF16)

    @pl.when(pl.program_id(1) == 0)
    def _():
        s_ref[...] = jnp.zeros_like(s_ref)

    def pair_lanes(h):
        return slice((h // 2) * LANES, (h // 2 + 1) * LANES)

    def as_block_diag(pieces):
        rows_ = [jnp.concatenate([p.astype(BF16), zero_half] if h < 2 else [zero_half, p.astype(BF16)], axis=1)
                 for h, p in enumerate(pieces)]
        return jnp.concatenate(rows_, axis=0)

    def body(c, carry):
        rs = pl.ds(pl.multiple_of(c * CHUNK, CHUNK), CHUNK)
        state = [[s_ref[b, g, h] for h in range(PACK)] for b, g, _ in chains]
        r = [jnp.dot(jnp.concatenate([w_ref[b, rs, l], qg_ref[b, rs, l]], axis=0),
                     as_block_diag(state[i]), preferred_element_type=F32) for i, (b, _, l) in enumerate(chains)]
        v_new = [u_ref[b, rs, l].astype(F32) - r[i][:CHUNK] for i, (b, _, l) in enumerate(chains)]
        upd = [lax.dot_general(kd_ref[b, rs, l], v_new[i].astype(BF16), (((0,), (0,)), ((), ())),
                               preferred_element_type=F32) for i, (b, _, l) in enumerate(chains)]
        for i, (b, g, l) in enumerate(chains):
            decay = dl_ref[b, c, :, l]
            for h in range(PACK):
                own = upd[i][h * HEAD_DIM:(h + 1) * HEAD_DIM, pair_lanes(h)]
                s_ref[b, g, h] = (state[i][h] * decay[:, pair_lanes(h)]
                                  + jnp.where(lane_lo if h % 2 == 0 else lane_hi, own, 0.0))
        for i, (b, _, l) in enumerate(chains):
            oacc_ref[b, rs, l] = r[i][CHUNK:] + jnp.dot(a_ref[b, rs, l], _block_diag(v_new[i], lane_lo),
                                                        preferred_element_type=F32)
        return carry

    lax.fori_loop(0, rows // CHUNK, body, 0)

    for b in range(n_seq):
        o = oacc_ref[b]
        ms = _head_sums(o * o, ones_ref[...]) * (1.0 / HEAD_DIM)
        o_ref[b] = (o * lax.rsqrt(ms + EPS) * nw_ref[...] * gz_ref[b].astype(F32)).astype(BF16)


def _gdn_scan_call(u, w, a, qg, kd, dl, gz, nw, ones_bd, weights_f32, *, seq, n_seq, rows):
    batch = u.shape[0] // seq
    steps = seq // rows
    as3d = lambda t: t.reshape(batch, seq, WIDTH)
    blk = pl.BlockSpec((n_seq, rows, WIDTH), lambda b, t: (b, t, 0))
    dl4 = dl.reshape(batch, seq // CHUNK, 1, WIDTH)
    slabs = [pl.BlockSpec((wt.shape[0] // steps, wt.shape[1]), lambda b, t: (t, 0)) for wt in weights_f32]
    assert all(wt.shape[0] % (16 * steps) == 0 for wt in weights_f32)
    outs = pl.pallas_call(
        functools.partial(_gdn_scan_kernel, len(weights_f32)),
        grid=(batch // n_seq, steps),
        in_specs=[blk] * 5 + [pl.BlockSpec((n_seq, rows // CHUNK, 1, WIDTH), lambda b, t: (b, t, 0, 0)), blk,
                              _const_spec(nw.shape), _const_spec(ones_bd.shape)] + slabs,
        out_specs=[blk] + slabs,
        out_shape=[jax.ShapeDtypeStruct((batch, seq, WIDTH), BF16)]
                  + [jax.ShapeDtypeStruct(wt.shape, BF16) for wt in weights_f32],
        scratch_shapes=[pltpu.VMEM((n_seq, WIDTH // MXU_DIM, PACK, HEAD_DIM, LANES), F32),
                        pltpu.VMEM((n_seq, rows, WIDTH), F32)],
        compiler_params=pltpu.CompilerParams(dimension_semantics=("arbitrary", "arbitrary"),
                                             vmem_limit_bytes=VMEM_LIMIT),
    )(as3d(u), as3d(w), as3d(a), as3d(qg), as3d(kd), dl4, as3d(gz), nw, ones_bd, *weights_f32)
    return outs[0].reshape(batch * seq, WIDTH), outs[1:]


FOX_TK = 256
VT_ROWS = HEAD_DIM + 16


def _fox_kernel(tk, q_ref, qx_ref, k_ref, kx_ref, v_ref, gate_ref, o_ref,
                vt_ref, kp_ref, s_ref, m_ref, acc_ref, qt_ref):
    slots = q_ref.shape[0] // tk
    tq = tk
    heads = range(HEADS)
    lane = lax.broadcasted_iota(jnp.int32, (tk, LANES), 1)

    def stage(t):
        rows = slice(t * tk, (t + 1) * tk)
        v_t = v_ref[rows, :].T
        for h in heads:
            vt_ref[h * VT_ROWS:h * VT_ROWS + HEAD_DIM, rows] = v_t[h * HEAD_DIM:(h + 1) * HEAD_DIM]
            vt_ref[h * VT_ROWS + HEAD_DIM:(h + 1) * VT_ROWS, rows] = jnp.ones((VT_ROWS - HEAD_DIM, tk), BF16)
        kx = kx_ref[rows, :].astype(F32)
        for p in range(HEADS // 2):
            pair = k_ref[rows, p * LANES:(p + 1) * LANES].astype(F32)
            for h in (2 * p, 2 * p + 1):
                own = pair if h % 2 == 0 else pltpu.roll(pair, HEAD_DIM, 1)
                ext = pltpu.roll(kx, (HEAD_DIM - EXT_STRIDE * h) % LANES, 1)
                blk = jnp.where(lane < HEAD_DIM, own, jnp.where(lane < HEAD_DIM + EXT_STRIDE, ext, 0.0))
                kp_ref[rows, h * LANES:(h + 1) * LANES] = blk.astype(BF16)
        qx_t = qx_ref[rows, :].T
        for h in heads:
            pair_t = q_ref[rows, (h // 2) * LANES:(h // 2 + 1) * LANES].T
            qt_ref[t, h] = jnp.concatenate(
                [pair_t[(h % 2) * HEAD_DIM:(h % 2 + 1) * HEAD_DIM], qx_t[EXT_STRIDE * h:EXT_STRIDE * (h + 1)],
                 jnp.zeros((LANES - HEAD_DIM - EXT_STRIDE, tq), BF16)], axis=0)
            m_ref[t, h] = jnp.full((1, tq), NEG_BIG, F32)
            acc_ref[t, h] = jnp.zeros((VT_ROWS, tq), F32)

    kv_pos = lax.broadcasted_iota(jnp.int32, (tk, tq), 0)
    q_pos = lax.broadcasted_iota(jnp.int32, (tk, tq), 1)

    def score(h, slot, j, masked):
        k0 = pl.multiple_of(j * tk, tk)
        s_h = jnp.dot(kp_ref[pl.ds(k0, tk), h * LANES:(h + 1) * LANES], qt_ref[slot, h],
                      preferred_element_type=F32)
        if masked:
            s_h = jnp.where(kv_pos <= q_pos, s_h, NEG_BIG)
        s_ref[h] = s_h

    def absorb(h, slot, j):
        k0 = pl.multiple_of(j * tk, tk)
        m_prev = m_ref[slot, h]
        s_h = s_ref[h]
        m_new = jnp.maximum(m_prev, jnp.max(s_h, axis=0, keepdims=True))
        p = jnp.exp2(s_h - m_new).astype(BF16)
        pv = jnp.dot(vt_ref[h * VT_ROWS:(h + 1) * VT_ROWS, pl.ds(k0, tk)], p, preferred_element_type=F32)
        m_ref[slot, h] = m_new
        acc_ref[slot, h] = jnp.exp2(m_prev - m_new) * acc_ref[slot, h] + pv

    stage(0)
    for h in heads:
        score(h, 0, 0, True)
    for slot in range(1, slots):
        stage(slot)
        for h in heads:
            absorb(h, slot - 1, slot - 1)
            score(h, slot, slot, True)

    def step(t, pending):
        p_slot, p_kv = pending
        n_slot = jnp.int32(0)
        n_kv = t
        for s in range(2, slots):
            start = (s * (s - 1)) // 2
            n_slot = jnp.where(t >= start, s, n_slot)
            n_kv = jnp.where(t >= start, t - start, n_kv)
        n_slot = jnp.maximum(n_slot, 1)
        for h in heads:
            absorb(h, p_slot, p_kv)
            score(h, n_slot, n_kv, False)
        return n_slot, n_kv

    n_off = (slots * (slots - 1)) // 2
    p_slot, p_kv = lax.fori_loop(0, n_off, step, (jnp.int32(slots - 1), jnp.int32(slots - 1)))
    for h in heads:
        absorb(h, p_slot, p_kv)
    for slot in range(slots):
        rows = slice(slot * tq, (slot + 1) * tq)
        o_t = jnp.concatenate([acc_ref[slot, h, 0:HEAD_DIM] / acc_ref[slot, h, HEAD_DIM:HEAD_DIM + 1]
                               for h in heads], axis=0)
        o_ref[rows, :] = (o_t.T * gate_ref[rows, :].astype(F32)).astype(BF16)


def _fox_call(fq, qx, fk, kx, fv, gate, *, seq, tk):
    m = fq.shape[0]
    n_tiles = seq // tk
    wide = pl.BlockSpec((seq, WIDTH), lambda b: (b, 0))
    narrow = pl.BlockSpec((seq, LANES), lambda b: (b, 0))
    return pl.pallas_call(
        functools.partial(_fox_kernel, tk),
        grid=(m // seq,),
        in_specs=[wide, narrow, wide, narrow, wide, wide],
        out_specs=wide,
        out_shape=jax.ShapeDtypeStruct((m, WIDTH), BF16),
        scratch_shapes=[pltpu.VMEM((HEADS * VT_ROWS, seq), BF16), pltpu.VMEM((seq, HEADS * LANES), BF16),
                        pltpu.VMEM((HEADS, tk, tk), F32),
                        pltpu.VMEM((n_tiles, HEADS, 1, tk), F32), pltpu.VMEM((n_tiles, HEADS, VT_ROWS, tk), F32),
                        pltpu.VMEM((n_tiles, HEADS, LANES, tk), BF16)],
        compiler_params=pltpu.CompilerParams(dimension_semantics=("parallel",),
                                             vmem_limit_bytes=VMEM_LIMIT),
    )(fq, qx, fk, kx, fv, gate)


FF_CHUNK = 4 * MXU_DIM
FF_SPLITS = tuple((s, min(FF_CHUNK, D_FF - s)) for s in range(0, D_FF, FF_CHUNK))


FFN_ROW_GROUPS = 4


def _ffn_kernel(final, mg_ref, mf_ref, x_ref, wo_ref, n2_ref, wg_ref, wu_ref, wd_ref, fn_ref, o_ref):
    rows = x_ref.shape[0] // FFN_ROW_GROUPS
    groups = [slice(r * rows, (r + 1) * rows) for r in range(FFN_ROW_GROUPS)]
    x1 = [x_ref[g, :]
          + jnp.dot(mg_ref[g, :], wo_ref[0:WIDTH, :], preferred_element_type=F32)
          + jnp.dot(mf_ref[g, :], wo_ref[WIDTH:2 * WIDTH, :], preferred_element_type=F32) for g in groups]
    h = [(v * lax.rsqrt(jnp.mean(v * v, axis=-1, keepdims=True) + EPS) * n2_ref[...]).astype(BF16) for v in x1]
    ffn = [None] * len(groups)
    for start, size in FF_SPLITS:
        gate = [jnp.dot(hv, wg_ref[:, start:start + size], preferred_element_type=F32) for hv in h]
        up = [jnp.dot(hv, wu_ref[:, start:start + size], preferred_element_type=F32) for hv in h]
        act = [(gv * _sigmoid(gv) * uv).astype(BF16) for gv, uv in zip(gate, up)]
        down = [jnp.dot(av, wd_ref[start:start + size, :], preferred_element_type=F32) for av in act]
        ffn = [d if f is None else f + d for f, d in zip(ffn, down)]
    for g, v, f in zip(groups, x1, ffn):
        y = v + f
        if final:
            y = y * lax.rsqrt(jnp.mean(y * y, axis=-1, keepdims=True) + EPS) * fn_ref[...]
        o_ref[g, :] = y


def _ffn_call(mix_g, mix_f, x2, wo, n2w, wg, wu, wd, fnw, *, tm, final):
    m = x2.shape[0]
    half = pl.BlockSpec((tm, WIDTH), lambda i: (i, 0))
    full = pl.BlockSpec((tm, D_MODEL), lambda i: (i, 0))

    def resident(shape):
        return pl.BlockSpec(shape, lambda i: (0, 0), pipeline_mode=pl.Buffered(1))

    return pl.pallas_call(
        functools.partial(_ffn_kernel, final),
        grid=(m // tm,),
        in_specs=[half, half, full, resident(wo.shape), resident(n2w.shape), resident(wg.shape),
                  resident(wu.shape), resident(wd.shape), resident(fnw.shape)],
        out_specs=full,
        out_shape=jax.ShapeDtypeStruct((m, D_MODEL), F32),
        compiler_params=pltpu.CompilerParams(dimension_semantics=("parallel",),
                                             vmem_limit_bytes=VMEM_LIMIT),
    )(mix_g, mix_f, x2, wo, n2w, wg, wu, wd, fnw)


def _lane_vec(parts):
    v = jnp.concatenate([p.astype(F32).reshape(-1) for p in parts])
    return jnp.pad(v, (0, N_SMALL - v.shape[0])).reshape(1, N_SMALL)


def kernel(x, norm1_w, w_in, gdn_conv_w, gdn_A_log, gdn_dt_bias, gdn_out_norm_w, fox_f_bias, fox_q_norm_w, fox_k_norm_w, w_out, norm2_w, w_ffn_gate, w_ffn_up, w_ffn_down, final_norm_w):
    batch, seq, _ = x.shape
    m = batch * seq
    depth = norm1_w.shape[0]
    zeros8 = jnp.zeros((HEADS,), F32)
    r2 = jnp.arange(MXU_DIM) // HEAD_DIM
    ones_bd = (r2[:, None] == r2[None, :]).astype(BF16)

    x2 = x.reshape(m, D_MODEL)
    for l in range(depth):
        w = w_in[l].astype(BF16)
        big = 4 * WIDTH
        g_small = w[:, big:big + 2 * HEADS]
        f_start = big + 2 * HEADS
        f_small = w[:, f_start + big:f_start + big + HEADS]
        w_gdn = w[:, :big]
        w_fox = w[:, f_start:f_start + big]
        w_gates = jnp.concatenate(
            [g_small, f_small, jnp.zeros((D_MODEL, N_SMALL - 3 * HEADS), BF16)], axis=1)
        bias_vec = _lane_vec([zeros8, gdn_dt_bias[l], fox_f_bias[l]])
        alog_vec = _lane_vec([zeros8, gdn_A_log[l], zeros8])
        fqw = (jnp.tile(fox_q_norm_w[l].astype(F32), HEADS) * (HEAD_DIM ** -0.5 * LOG2E)).reshape(1, WIDTH)
        fkw = jnp.tile(fox_k_norm_w[l].astype(F32), HEADS).reshape(1, WIDTH)

        gq, gk, gv, gz, fq, fk, fv, fg, small, kx, qx, g_t = _inproj_call(
            x2, norm1_w[l].reshape(1, D_MODEL), w_gdn, w_fox, w_gates, gdn_conv_w[l].astype(F32), bias_vec, alog_vec,
            fqw, fkw, ones_bd, seq=seq, tm=512)

        u, wmat, amat, qg, kd, dl = _gdn_wy_call(gq, gk, gv, small, g_t, rows=2048)
        nw = jnp.tile(gdn_out_norm_w[l].astype(F32), HEADS).reshape(1, WIDTH)
        mix_g, (wo, wg, wu, wd) = _gdn_scan_call(
            u, wmat, amat, qg, kd, dl, gz, nw, ones_bd,
            [w_out[l], w_ffn_gate[l], w_ffn_up[l], w_ffn_down[l]],
            seq=seq, n_seq=math.gcd(batch, SCAN_SEQS), rows=256)
        mix_f = _fox_call(fq, qx, fk, kx, fv, fg, seq=seq, tk=FOX_TK)

        x2 = _ffn_call(mix_g, mix_f, x2, wo, norm2_w[l].reshape(1, D_MODEL), wg, wu, wd,
                       final_norm_w.reshape(1, D_MODEL), tm=1024, final=(l == depth - 1))
    return x2.reshape(batch, seq, D_MODEL)
```

```python
import functools
import math

import numpy as np
import jax
import jax.numpy as jnp
from jax import lax
from jax.experimental import pallas as pl
from jax.experimental.pallas import tpu as pltpu

D_MODEL = 1024
HEADS = 8
HEAD_DIM = 64
WIDTH = HEADS * HEAD_DIM
CONV_K = 4
CHUNK = 64
D_FF = 2816
EPS = 1e-6

LANES = 128
MXU_DIM = 256
PACK = MXU_DIM // HEAD_DIM
N_SMALL = LANES
VMEM_LIMIT = 56 * 1024 * 1024

F32 = jnp.float32
BF16 = jnp.bfloat16
NEG_BIG = -1e30
LOG2E = 1.4426950408889634
EXT_STRIDE = 16


def _const_spec(shape):
    nd = len(shape)
    return pl.BlockSpec(shape, lambda *_: (0,) * nd, pipeline_mode=pl.Buffered(1))


def _sigmoid(x):
    return 1.0 / (1.0 + jnp.exp(-x))


def _head_sums(y2, ones_bd):
    yb = y2.astype(BF16)
    parts = [jnp.dot(yb[:, c * MXU_DIM:(c + 1) * MXU_DIM], ones_bd, preferred_element_type=F32)
             for c in range(WIDTH // MXU_DIM)]
    return jnp.concatenate(parts, axis=1)


def _seg_cumsum(v, pos, longest):
    d = 1
    while d < longest:
        v = v + jnp.where(pos >= d, pltpu.roll(v, d, 0), 0.0)
        d *= 2
    return v


INPROJ_ROW_GROUPS = 2


def _inproj_kernel(tiles_per_seq, x_ref, n1_ref, wg_ref, wf_ref, ws_ref, cw_ref, bias_ref, alog_ref, fqw_ref, fkw_ref,
                   ones_ref, sel_ref, one_ref,
                   gq_ref, gk_ref, gv_ref, gz_ref, fq_ref, fk_ref, fv_ref, fg_ref, sm_ref, kx_ref, qx_ref, gt_ref,
                   tail_ref, carry_ref):
    i = pl.program_id(0)
    tm = x_ref.shape[0]

    @pl.when(i % tiles_per_seq == 0)
    def _():
        tail_ref[:, 0:8, :] = jnp.zeros((3, 8, WIDTH), F32)
        carry_ref[...] = jnp.zeros_like(carry_ref)

    ones_bd = ones_ref[...]
    rows = tm // INPROJ_ROW_GROUPS
    groups = [slice(g * rows, (g + 1) * rows) for g in range(INPROJ_ROW_GROUPS)]

    def project(g):
        x = x_ref[g, :]
        h = (x * lax.rsqrt(jnp.mean(x * x, axis=-1, keepdims=True) + EPS) * n1_ref[...]).astype(BF16)
        y_gates = jnp.dot(h, ws_ref[...], preferred_element_type=F32)
        y_gdn = jnp.dot(h, wg_ref[...], preferred_element_type=F32)
        y_fox = jnp.dot(h, wf_ref[...], preferred_element_type=F32)
        return ([y_gdn[:, s * WIDTH:(s + 1) * WIDTH] for s in range(4)]
                + [y_fox[:, s * WIDTH:(s + 1) * WIDTH] for s in range(4)] + [y_gates])

    def epilogue(g, ys, last):
        for s, out_ref in enumerate((gq_ref, gk_ref, gv_ref)):
            y = ys[s]
            cw = cw_ref[:, s * WIDTH:(s + 1) * WIDTH]
            tail_ref[s, 8 + g.start:8 + g.stop] = y
            acc = y * cw[CONV_K - 1:CONV_K]
            for k in range(1, CONV_K):
                acc = acc + tail_ref[s, 8 + g.start - k:8 + g.stop - k] * cw[CONV_K - 1 - k:CONV_K - k]
            if last:
                tail_ref[s, 0:8] = y[rows - 8:]
            act = acc * _sigmoid(acc)
            if s < 2:
                act = act * lax.rsqrt(_head_sums(act * act, ones_bd) + EPS)
            out_ref[g, :] = act.astype(BF16)

        z = ys[3]
        gz_ref[g, :] = (z * _sigmoid(z)).astype(BF16)

        for s, out_ref, w_norm in ((4, fq_ref, fqw_ref), (5, fk_ref, fkw_ref)):
            y = ys[s]
            ms = _head_sums(y * y, ones_bd) * (1.0 / HEAD_DIM)
            out_ref[g, :] = (y * lax.rsqrt(ms + EPS) * w_norm[...]).astype(BF16)

        fv_ref[g, :] = ys[6].astype(BF16)
        fg_ref[g, :] = _sigmoid(ys[7]).astype(BF16)

        t = ys[8] + bias_ref[...]
        lane = lax.broadcasted_iota(jnp.int32, t.shape, 1)
        beta = _sigmoid(t)
        tail = jnp.log1p(jnp.exp(-jnp.abs(t)))
        g_log = -jnp.exp(alog_ref[...]) * (jnp.maximum(t, 0.0) + tail)
        log_f = -(jnp.maximum(-t, 0.0) + tail)
        row = lax.broadcasted_iota(jnp.int32, t.shape, 0)
        seg_pos = jnp.where(lane < 2 * HEADS, row & (CHUNK - 1), row)
        cum = _seg_cumsum(jnp.where(lane < 2 * HEADS, g_log, log_f), seg_pos, rows)
        f_cum = cum + carry_ref[0:1, :]
        carry_ref[0:1, :] = f_cum[rows - 1:rows, :]
        sm_ref[g, :] = jnp.where(lane < HEADS, beta, jnp.where(lane < 2 * HEADS, cum, f_cum))
        gt_ref[:, g] = cum.T[HEADS:2 * HEADS, :]

        f2 = f_cum * LOG2E
        hi = f2.astype(BF16)
        r1 = f2 - hi.astype(F32)
        mid = r1.astype(BF16)
        lo = (r1 - mid.astype(F32)).astype(BF16)
        ext = (jnp.dot(jnp.concatenate([hi, mid, lo], axis=1), sel_ref[...], preferred_element_type=F32)
               + one_ref[...])
        kx_ref[g, :] = ext[:, :LANES].astype(BF16)
        qx_ref[g, :] = ext[:, LANES:].astype(BF16)

    projected = [project(g) for g in groups]
    for gi, g in enumerate(groups):
        epilogue(g, projected[gi], gi == len(groups) - 1)


def _bias_selectors():
    sel = np.zeros((3 * LANES, 2 * LANES), np.float32)
    one = np.zeros((1, 2 * LANES), np.float32)
    for h in range(HEADS):
        src = 2 * HEADS + h
        for part in range(3):
            sel[part * LANES + src, EXT_STRIDE * h + part] = -1.0
            sel[part * LANES + src, LANES + EXT_STRIDE * h + 3 + part] = 1.0
            one[0, EXT_STRIDE * h + 3 + part] = 1.0
            one[0, LANES + EXT_STRIDE * h + part] = 1.0
    return jnp.asarray(sel, BF16), jnp.asarray(one, F32)


def _inproj_call(x2, n1w, w_gdn, w_fox, w_gates, conv_w, bias_vec, alog_vec, fqw, fkw, ones_bd, *, seq, tm):
    m = x2.shape[0]
    assert seq % tm == 0 and tm % CHUNK == 0 and (tm & (tm - 1)) == 0
    sel, one = _bias_selectors()
    wide = pl.BlockSpec((tm, WIDTH), lambda i: (i, 0))
    narrow = pl.BlockSpec((tm, LANES), lambda i: (i, 0))
    out_shape = ([jax.ShapeDtypeStruct((m, WIDTH), BF16)] * 8 + [jax.ShapeDtypeStruct((m, N_SMALL), F32)]
                 + [jax.ShapeDtypeStruct((m, LANES), BF16)] * 2 + [jax.ShapeDtypeStruct((HEADS, m), F32)])
    return pl.pallas_call(
        functools.partial(_inproj_kernel, seq // tm),
        grid=(m // tm,),
        in_specs=[pl.BlockSpec((tm, D_MODEL), lambda i: (i, 0)),
                  _const_spec(n1w.shape), _const_spec(w_gdn.shape), _const_spec(w_fox.shape),
                  _const_spec(w_gates.shape), _const_spec(conv_w.shape),
                  _const_spec(bias_vec.shape), _const_spec(alog_vec.shape),
                  _const_spec(fqw.shape), _const_spec(fkw.shape), _const_spec(ones_bd.shape),
                  _const_spec(sel.shape), _const_spec(one.shape)],
        out_specs=[wide] * 8 + [narrow] * 3 + [pl.BlockSpec((HEADS, tm), lambda i: (0, i))],
        out_shape=out_shape,
        scratch_shapes=[pltpu.VMEM((3, tm + 8, WIDTH), F32), pltpu.VMEM((8, N_SMALL), F32)],
        compiler_params=pltpu.CompilerParams(dimension_semantics=("arbitrary",),
                                             vmem_limit_bytes=VMEM_LIMIT),
    )(x2, n1w, w_gdn, w_fox, w_gates, conv_w, bias_vec, alog_vec, fqw, fkw, ones_bd, sel, one)


def _lane_lo_mask():
    return lax.broadcasted_iota(jnp.int32, (CHUNK, LANES), 1) < HEAD_DIM


def _block_diag(x, lane_lo):
    xb = x.astype(BF16)
    zero = jnp.zeros((CHUNK, LANES), BF16)
    blocks = []
    for h in range(PACK):
        half = xb[:, (h // 2) * LANES:(h // 2 + 1) * LANES]
        keep = jnp.where(lane_lo if h % 2 == 0 else jnp.logical_not(lane_lo), half, zero)
        blocks.append(jnp.concatenate([keep, zero] if h < 2 else [zero, keep], axis=1))
    return jnp.concatenate(blocks, axis=0)


def _mm(lhs, rhs_bd):
    return jnp.dot(lhs.astype(BF16), rhs_bd, preferred_element_type=F32)


def _expand_heads(sm, first_lane):
    lane_lo = lax.broadcasted_iota(jnp.int32, (sm.shape[0], LANES), 1) < HEAD_DIM
    parts = []
    for p in range(HEADS // 2):
        c = first_lane + 2 * p
        even = jnp.broadcast_to(sm[:, c:c + 1], (sm.shape[0], LANES))
        odd = jnp.broadcast_to(sm[:, c + 1:c + 2], (sm.shape[0], LANES))
        parts.append(jnp.where(lane_lo, even, odd))
    return jnp.concatenate(parts, axis=1)


WY_GROUP_CHUNKS = 4


def _gdn_wy_kernel(q_ref, k_ref, v_ref, sm_ref, gt_ref,
                   u_ref, w_ref, a_ref, qg_ref, kd_ref, dl_ref):
    n_chunks = q_ref.shape[0] // CHUNK
    n_groups = WIDTH // MXU_DIM
    shape = (CHUNK, MXU_DIM)
    row = lax.broadcasted_iota(jnp.int32, shape, 0)
    col = lax.broadcasted_iota(jnp.int32, shape, 1) & (HEAD_DIM - 1)
    causal = row >= col
    strict = row > col
    eye = (row == col).astype(F32)
    bd_mask = _lane_lo_mask()
    scale = HEAD_DIM ** -0.5

    bd = lambda xs: [_block_diag(x, bd_mask) for x in xs]
    mm = lambda ls, rs_: [_mm(a, b) for a, b in zip(ls, rs_)]

    def key_row(ch, g):
        return jnp.concatenate([gt_ref[g * PACK + h:g * PACK + h + 1, ch * CHUNK:(ch + 1) * CHUNK]
                                for h in range(PACK)], axis=1)

    def prepare(ch):
        r = slice(ch * CHUNK, (ch + 1) * CHUNK)
        sm = sm_ref[r, :]
        bx_all = _expand_heads(sm, 0)
        gx_all = _expand_heads(sm, HEADS)
        out = []
        for g in range(n_groups):
            l = slice(g * MXU_DIM, (g + 1) * MXU_DIM)
            q = q_ref[r, l].astype(F32)
            k = k_ref[r, l].astype(F32)
            bx, gx = bx_all[:, l], gx_all[:, l]
            kb = k * bx
            decay = jnp.exp(jnp.where(causal, gx - key_row(ch, g), NEG_BIG))
            s1 = lax.dot_general(jnp.concatenate([kb, q], axis=0).astype(BF16), _block_diag(k, bd_mask),
                                 (((1,), (1,)), ((), ())), preferred_element_type=F32)
            a_ref[r, l] = (s1[CHUNK:] * decay * scale).astype(BF16)
            eg = jnp.exp(gx)
            g_last = gx[CHUNK - 1:CHUNK, :]
            qg_ref[r, l] = (q * eg * scale).astype(BF16)
            kd_ref[r, l] = (k * jnp.exp(g_last - gx)).astype(BF16)
            dl_ref[ch, :, l] = jnp.exp(g_last)
            out.append(dict(r=r, l=l, lmat=jnp.where(strict, s1[:CHUNK] * decay, 0.0),
                            vb=v_ref[r, l].astype(F32) * bx, kbg=kb * eg))
        return out

    def inverse_stages(units):
        st = {}
        lmat = [u["lmat"] for u in units]
        blk8 = (row >> 3) == (col >> 3)

        def base0():
            st["n8"] = [jnp.where(blk8, -l, 0.0) for l in lmat]
            st["t0"] = [eye + n for n in st["n8"]]
            st["p1"] = mm(st["n8"], bd(st["n8"]))

        def base1():
            r = mm([jnp.concatenate([p, t], axis=0) for p, t in zip(st["p1"], st["t0"])], bd(st["p1"]))
            st["p2"] = [x[:CHUNK] for x in r]
            st["t"] = [t + x[CHUNK:] for t, x in zip(st["t0"], r)]

        def base2():
            st["t"] = [t + z for t, z in zip(st["t"], mm(st["t"], bd(st["p2"])))]

        def merge_a(ls):
            off = ((row >> (ls + 1)) == (col >> (ls + 1))) & ((row >> ls) == (col >> ls) + 1)
            st["y"] = mm([jnp.where(off, l, 0.0) for l in lmat], bd(st["t"]))

        def merge_b():
            st["t"] = [t - z for t, z in zip(st["t"], mm(st["t"], bd(st["y"])))]

        def apply(name, ref):
            for u, x in zip(units, mm(st["t"], bd([u[name] for u in units]))):
                ref[u["r"], u["l"]] = x.astype(BF16)

        stages = [base0, base1, base2]
        for ls in (3, 4, 5):
            stages += [functools.partial(merge_a, ls), merge_b]
        return stages + [functools.partial(apply, "vb", u_ref), functools.partial(apply, "kbg", w_ref)]

    groups = [list(range(c, min(c + WY_GROUP_CHUNKS, n_chunks))) for c in range(0, n_chunks, WY_GROUP_CHUNKS)]
    units = [u for ch in groups[0] for u in prepare(ch)]
    for gi in range(len(groups)):
        upcoming = groups[gi + 1] if gi + 1 < len(groups) else []
        stages = inverse_stages(units)
        per_stage = -(-len(upcoming) // len(stages))
        units = []
        for si, stage in enumerate(stages):
            stage()
            for ch in upcoming[si * per_stage:(si + 1) * per_stage]:
                units += prepare(ch)


def _gdn_wy_call(gq, gk, gv, small, g_t, *, rows):
    m = gq.shape[0]
    cpb = rows // CHUNK
    blk = pl.BlockSpec((rows, WIDTH), lambda i: (i, 0))
    rowblk = pl.BlockSpec((cpb, 1, WIDTH), lambda i: (i, 0, 0))
    bf = jax.ShapeDtypeStruct((m, WIDTH), BF16)
    return pl.pallas_call(
        _gdn_wy_kernel,
        grid=(m // rows,),
        in_specs=[blk] * 3 + [pl.BlockSpec((rows, N_SMALL), lambda i: (i, 0)),
                              pl.BlockSpec((HEADS, rows), lambda i: (0, i))],
        out_specs=[blk] * 5 + [rowblk],
        out_shape=[bf] * 5 + [jax.ShapeDtypeStruct((m // CHUNK, 1, WIDTH), F32)],
        compiler_params=pltpu.CompilerParams(dimension_semantics=("parallel",),
                                             vmem_limit_bytes=VMEM_LIMIT),
    )(gq, gk, gv, small, g_t)


SCAN_SEQS = 8


def _gdn_scan_kernel(n_cast, u_ref, w_ref, a_ref, qg_ref, kd_ref, dl_ref, gz_ref, nw_ref, ones_ref, *refs):
    cast_in, o_ref, cast_out = refs[:n_cast], refs[n_cast], refs[n_cast + 1:2 * n_cast + 1]
    s_ref, oacc_ref = refs[2 * n_cast + 1:]
    for src, dst in zip(cast_in, cast_out):
        dst[...] = src[...].astype(BF16)
    n_seq, rows, _ = u_ref.shape
    n_groups = WIDTH // MXU_DIM
    chains = [(b, g, slice(g * MXU_DIM, (g + 1) * MXU_DIM)) for b in range(n_seq) for g in range(n_groups)]
    lane_lo = _lane_lo_mask()
    lane_hi = jnp.logical_not(lane_lo)
    zero_half = jnp.zeros((HEAD_DIM, LANES), BF16)

    @pl.when(pl.program_id(1) == 0)
    def _():
        s_ref[...] = jnp.zeros_like(s_ref)

    def pair_lanes(h):
        return slice((h // 2) * LANES, (h // 2 + 1) * LANES)

    def as_block_diag(pieces):
        rows_ = [jnp.concatenate([p.astype(BF16), zero_half] if h < 2 else [zero_half, p.astype(BF16)], axis=1)
                 for h, p in enumerate(pieces)]
        return jnp.concatenate(rows_, axis=0)

    def body(c, carry):
        rs = pl.ds(pl.multiple_of(c * CHUNK, CHUNK), CHUNK)
        state = [[s_ref[b, g, h] for h in range(PACK)] for b, g, _ in chains]
        r = [jnp.dot(jnp.concatenate([w_ref[b, rs, l], qg_ref[b, rs, l]], axis=0),
                     as_block_diag(state[i]), preferred_element_type=F32) for i, (b, _, l) in enumerate(chains)]
        v_new = [u_ref[b, rs, l].astype(F32) - r[i][:CHUNK] for i, (b, _, l) in enumerate(chains)]
        upd = [lax.dot_general(kd_ref[b, rs, l], v_new[i].astype(BF16), (((0,), (0,)), ((), ())),
                               preferred_element_type=F32) for i, (b, _, l) in enumerate(chains)]
        for i, (b, g, l) in enumerate(chains):
            decay = dl_ref[b, c, :, l]
            for h in range(PACK):
                own = upd[i][h * HEAD_DIM:(h + 1) * HEAD_DIM, pair_lanes(h)]
                s_ref[b, g, h] = (state[i][h] * decay[:, pair_lanes(h)]
                                  + jnp.where(lane_lo if h % 2 == 0 else lane_hi, own, 0.0))
        for i, (b, _, l) in enumerate(chains):
            oacc_ref[b, rs, l] = r[i][CHUNK:] + jnp.dot(a_ref[b, rs, l], _block_diag(v_new[i], lane_lo),
                                                        preferred_element_type=F32)
        return carry

    lax.fori_loop(0, rows // CHUNK, body, 0)

    for b in range(n_seq):
        o = oacc_ref[b]
        ms = _head_sums(o * o, ones_ref[...]) * (1.0 / HEAD_DIM)
        o_ref[b] = (o * lax.rsqrt(ms + EPS) * nw_ref[...] * gz_ref[b].astype(F32)).astype(BF16)


def _gdn_scan_call(u, w, a, qg, kd, dl, gz, nw, ones_bd, weights_f32, *, seq, n_seq, rows):
    batch = u.shape[0] // seq
    steps = seq // rows
    as3d = lambda t: t.reshape(batch, seq, WIDTH)
    blk = pl.BlockSpec((n_seq, rows, WIDTH), lambda b, t: (b, t, 0))
    dl4 = dl.reshape(batch, seq // CHUNK, 1, WIDTH)
    slabs = [pl.BlockSpec((wt.shape[0] // steps, wt.shape[1]), lambda b, t: (t, 0)) for wt in weights_f32]
    assert all(wt.shape[0] % (16 * steps) == 0 for wt in weights_f32)
    outs = pl.pallas_call(
        functools.partial(_gdn_scan_kernel, len(weights_f32)),
        grid=(batch // n_seq, steps),
        in_specs=[blk] * 5 + [pl.BlockSpec((n_seq, rows // CHUNK, 1, WIDTH), lambda b, t: (b, t, 0, 0)), blk,
                              _const_spec(nw.shape), _const_spec(ones_bd.shape)] + slabs,
        out_specs=[blk] + slabs,
        out_shape=[jax.ShapeDtypeStruct((batch, seq, WIDTH), BF16)]
                  + [jax.ShapeDtypeStruct(wt.shape, BF16) for wt in weights_f32],
        scratch_shapes=[pltpu.VMEM((n_seq, WIDTH // MXU_DIM, PACK, HEAD_DIM, LANES), F32),
                        pltpu.VMEM((n_seq, rows, WIDTH), F32)],
        compiler_params=pltpu.CompilerParams(dimension_semantics=("arbitrary", "arbitrary"),
                                             vmem_limit_bytes=VMEM_LIMIT),
    )(as3d(u), as3d(w), as3d(a), as3d(qg), as3d(kd), dl4, as3d(gz), nw, ones_bd, *weights_f32)
    return outs[0].reshape(batch * seq, WIDTH), outs[1:]


FOX_TK = 256
VT_ROWS = HEAD_DIM + 16


def _fox_kernel(tk, q_ref, qx_ref, k_ref, kx_ref, v_ref, gate_ref, o_ref,
                vt_ref, kp_ref, s_ref, m_ref, acc_ref, qt_ref):
    slots = q_ref.shape[0] // tk
    tq = tk
    heads = range(HEADS)
    lane = lax.broadcasted_iota(jnp.int32, (tk, LANES), 1)

    def stage(t):
        rows = slice(t * tk, (t + 1) * tk)
        v_t = v_ref[rows, :].T
        for h in heads:
            vt_ref[h * VT_ROWS:h * VT_ROWS + HEAD_DIM, rows] = v_t[h * HEAD_DIM:(h + 1) * HEAD_DIM]
            vt_ref[h * VT_ROWS + HEAD_DIM:(h + 1) * VT_ROWS, rows] = jnp.ones((VT_ROWS - HEAD_DIM, tk), BF16)
        kx = kx_ref[rows, :].astype(F32)
        for p in range(HEADS // 2):
            pair = k_ref[rows, p * LANES:(p + 1) * LANES].astype(F32)
            for h in (2 * p, 2 * p + 1):
                own = pair if h % 2 == 0 else pltpu.roll(pair, HEAD_DIM, 1)
                ext = pltpu.roll(kx, (HEAD_DIM - EXT_STRIDE * h) % LANES, 1)
                blk = jnp.where(lane < HEAD_DIM, own, jnp.where(lane < HEAD_DIM + EXT_STRIDE, ext, 0.0))
                kp_ref[rows, h * LANES:(h + 1) * LANES] = blk.astype(BF16)
        qx_t = qx_ref[rows, :].T
        for h in heads:
            pair_t = q_ref[rows, (h // 2) * LANES:(h // 2 + 1) * LANES].T
            qt_ref[t, h] = jnp.concatenate(
                [pair_t[(h % 2) * HEAD_DIM:(h % 2 + 1) * HEAD_DIM], qx_t[EXT_STRIDE * h:EXT_STRIDE * (h + 1)],
                 jnp.zeros((LANES - HEAD_DIM - EXT_STRIDE, tq), BF16)], axis=0)
            m_ref[t, h] = jnp.full((1, tq), NEG_BIG, F32)
            acc_ref[t, h] = jnp.zeros((VT_ROWS, tq), F32)

    kv_pos = lax.broadcasted_iota(jnp.int32, (tk, tq), 0)
    q_pos = lax.broadcasted_iota(jnp.int32, (tk, tq), 1)

    def score(h, slot, j, masked):
        k0 = pl.multiple_of(j * tk, tk)
        s_h = jnp.dot(kp_ref[pl.ds(k0, tk), h * LANES:(h + 1) * LANES], qt_ref[slot, h],
                      preferred_element_type=F32)
        if masked:
            s_h = jnp.where(kv_pos <= q_pos, s_h, NEG_BIG)
        s_ref[h] = s_h

    def absorb(h, slot, j):
        k0 = pl.multiple_of(j * tk, tk)
        m_prev = m_ref[slot, h]
        s_h = s_ref[h]
        m_new = jnp.maximum(m_prev, jnp.max(s_h, axis=0, keepdims=True))
        p = jnp.exp2(s_h - m_new).astype(BF16)
        pv = jnp.dot(vt_ref[h * VT_ROWS:(h + 1) * VT_ROWS, pl.ds(k0, tk)], p, preferred_element_type=F32)
        m_ref[slot, h] = m_new
        acc_ref[slot, h] = jnp.exp2(m_prev - m_new) * acc_ref[slot, h] + pv

    stage(0)
    for h in heads:
        score(h, 0, 0, True)
    for slot in range(1, slots):
        stage(slot)
        for h in heads:
            absorb(h, slot - 1, slot - 1)
            score(h, slot, slot, True)

    def step(t, pending):
        p_slot, p_kv = pending
        n_slot = jnp.int32(0)
        n_kv = t
        for s in range(2, slots):
            start = (s * (s - 1)) // 2
            n_slot = jnp.where(t >= start, s, n_slot)
            n_kv = jnp.where(t >= start, t - start, n_kv)
        n_slot = jnp.maximum(n_slot, 1)
        for h in heads:
            absorb(h, p_slot, p_kv)
            score(h, n_slot, n_kv, False)
        return n_slot, n_kv

    n_off = (slots * (slots - 1)) // 2
    p_slot, p_kv = lax.fori_loop(0, n_off, step, (jnp.int32(slots - 1), jnp.int32(slots - 1)))
    for h in heads:
        absorb(h, p_slot, p_kv)
    for slot in range(slots):
        rows = slice(slot * tq, (slot + 1) * tq)
        o_t = jnp.concatenate([acc_ref[slot, h, 0:HEAD_DIM] / acc_ref[slot, h, HEAD_DIM:HEAD_DIM + 1]
                               for h in heads], axis=0)
        o_ref[rows, :] = (o_t.T * gate_ref[rows, :].astype(F32)).astype(BF16)


def _fox_call(fq, qx, fk, kx, fv, gate, *, seq, tk):
    m = fq.shape[0]
    n_tiles = seq // tk
    wide = pl.BlockSpec((seq, WIDTH), lambda b: (b, 0))
    narrow = pl.BlockSpec((seq, LANES), lambda b: (b, 0))
    return pl.pallas_call(
        functools.partial(_fox_kernel, tk),
        grid=(m // seq,),
        in_specs=[wide, narrow, wide, narrow, wide, wide],
        out_specs=wide,
        out_shape=jax.ShapeDtypeStruct((m, WIDTH), BF16),
        scratch_shapes=[pltpu.VMEM((HEADS * VT_ROWS, seq), BF16), pltpu.VMEM((seq, HEADS * LANES), BF16),
                        pltpu.VMEM((HEADS, tk, tk), F32),
                        pltpu.VMEM((n_tiles, HEADS, 1, tk), F32), pltpu.VMEM((n_tiles, HEADS, VT_ROWS, tk), F32),
                        pltpu.VMEM((n_tiles, HEADS, LANES, tk), BF16)],
        compiler_params=pltpu.CompilerParams(dimension_semantics=("parallel",),
                                             vmem_limit_bytes=VMEM_LIMIT),
    )(fq, qx, fk, kx, fv, gate)


FF_CHUNK = 4 * MXU_DIM
FF_SPLITS = tuple((s, min(FF_CHUNK, D_FF - s)) for s in range(0, D_FF, FF_CHUNK))


FFN_ROW_GROUPS = 4


def _ffn_kernel(final, mg_ref, mf_ref, x_ref, wo_ref, n2_ref, wg_ref, wu_ref, wd_ref, fn_ref, o_ref):
    rows = x_ref.shape[0] // FFN_ROW_GROUPS
    groups = [slice(r * rows, (r + 1) * rows) for r in range(FFN_ROW_GROUPS)]
    x1 = [x_ref[g, :]
          + jnp.dot(mg_ref[g, :], wo_ref[0:WIDTH, :], preferred_element_type=F32)
          + jnp.dot(mf_ref[g, :], wo_ref[WIDTH:2 * WIDTH, :], preferred_element_type=F32) for g in groups]
    h = [(v * lax.rsqrt(jnp.mean(v * v, axis=-1, keepdims=True) + EPS) * n2_ref[...]).astype(BF16) for v in x1]
    ffn = [None] * len(groups)
    for start, size in FF_SPLITS:
        gate = [jnp.dot(hv, wg_ref[:, start:start + size], preferred_element_type=F32) for hv in h]
        up = [jnp.dot(hv, wu_ref[:, start:start + size], preferred_element_type=F32) for hv in h]
        act = [(gv * _sigmoid(gv) * uv).astype(BF16) for gv, uv in zip(gate, up)]
        down = [jnp.dot(av, wd_ref[start:start + size, :], preferred_element_type=F32) for av in act]
        ffn = [d if f is None else f + d for f, d in zip(ffn, down)]
    for g, v, f in zip(groups, x1, ffn):
        y = v + f
        if final:
            y = y * lax.rsqrt(jnp.mean(y * y, axis=-1, keepdims=True) + EPS) * fn_ref[...]
        o_ref[g, :] = y


def _ffn_call(mix_g, mix_f, x2, wo, n2w, wg, wu, wd, fnw, *, tm, final):
    m = x2.shape[0]
    half = pl.BlockSpec((tm, WIDTH), lambda i: (i, 0))
    full = pl.BlockSpec((tm, D_MODEL), lambda i: (i, 0))

    def resident(shape):
        return pl.BlockSpec(shape, lambda i: (0, 0), pipeline_mode=pl.Buffered(1))

    return pl.pallas_call(
        functools.partial(_ffn_kernel, final),
        grid=(m // tm,),
        in_specs=[half, half, full, resident(wo.shape), resident(n2w.shape), resident(wg.shape),
                  resident(wu.shape), resident(wd.shape), resident(fnw.shape)],
        out_specs=full,
        out_shape=jax.ShapeDtypeStruct((m, D_MODEL), F32),
        compiler_params=pltpu.CompilerParams(dimension_semantics=("parallel",),
                                             vmem_limit_bytes=VMEM_LIMIT),
    )(mix_g, mix_f, x2, wo, n2w, wg, wu, wd, fnw)


def _lane_vec(parts):
    v = jnp.concatenate([p.astype(F32).reshape(-1) for p in parts])
    return jnp.pad(v, (0, N_SMALL - v.shape[0])).reshape(1, N_SMALL)


def kernel(x, norm1_w, w_in, gdn_conv_w, gdn_A_log, gdn_dt_bias, gdn_out_norm_w, fox_f_bias, fox_q_norm_w, fox_k_norm_w, w_out, norm2_w, w_ffn_gate, w_ffn_up, w_ffn_down, final_norm_w):
    batch, seq, _ = x.shape
    m = batch * seq
    depth = norm1_w.shape[0]
    zeros8 = jnp.zeros((HEADS,), F32)
    r2 = jnp.arange(MXU_DIM) // HEAD_DIM
    ones_bd = (r2[:, None] == r2[None, :]).astype(BF16)

    x2 = x.reshape(m, D_MODEL)
    for l in range(depth):
        w = w_in[l].astype(BF16)
        big = 4 * WIDTH
        g_small = w[:, big:big + 2 * HEADS]
        f_start = big + 2 * HEADS
        f_small = w[:, f_start + big:f_start + big + HEADS]
        w_gdn = w[:, :big]
        w_fox = w[:, f_start:f_start + big]
        w_gates = jnp.concatenate(
            [g_small, f_small, jnp.zeros((D_MODEL, N_SMALL - 3 * HEADS), BF16)], axis=1)
        bias_vec = _lane_vec([zeros8, gdn_dt_bias[l], fox_f_bias[l]])
        alog_vec = _lane_vec([zeros8, gdn_A_log[l], zeros8])
        fqw = (jnp.tile(fox_q_norm_w[l].astype(F32), HEADS) * (HEAD_DIM ** -0.5 * LOG2E)).reshape(1, WIDTH)
        fkw = jnp.tile(fox_k_norm_w[l].astype(F32), HEADS).reshape(1, WIDTH)

        gq, gk, gv, gz, fq, fk, fv, fg, small, kx, qx, g_t = _inproj_call(
            x2, norm1_w[l].reshape(1, D_MODEL), w_gdn, w_fox, w_gates, gdn_conv_w[l].astype(F32), bias_vec, alog_vec,
            fqw, fkw, ones_bd, seq=seq, tm=512)

        u, wmat, amat, qg, kd, dl = _gdn_wy_call(gq, gk, gv, small, g_t, rows=2048)
        nw = jnp.tile(gdn_out_norm_w[l].astype(F32), HEADS).reshape(1, WIDTH)
        mix_g, (wo, wg, wu, wd) = _gdn_scan_call(
            u, wmat, amat, qg, kd, dl, gz, nw, ones_bd,
            [w_out[l], w_ffn_gate[l], w_ffn_up[l], w_ffn_down[l]],
            seq=seq, n_seq=math.gcd(batch, SCAN_SEQS), rows=256)
        mix_f = _fox_call(fq, qx, fk, kx, fv, fg, seq=seq, tk=FOX_TK)

        x2 = _ffn_call(mix_g, mix_f, x2, wo, norm2_w[l].reshape(1, D_MODEL), wg, wu, wd,
                       final_norm_w.reshape(1, D_MODEL), tm=1024, final=(l == depth - 1))
    return x2.reshape(batch, seq, D_MODEL)
```

```python
import functools
import math

import numpy as np
import jax
import jax.numpy as jnp
from jax import lax
from jax.experimental import pallas as pl
from jax.experimental.pallas import tpu as pltpu

D_MODEL = 1024
HEADS = 8
HEAD_DIM = 64
WIDTH = HEADS * HEAD_DIM
CONV_K = 4
CHUNK = 64
D_FF = 2816
EPS = 1e-6

LANES = 128
MXU_DIM = 256
PACK = MXU_DIM // HEAD_DIM
N_SMALL = LANES
VMEM_LIMIT = 56 * 1024 * 1024

F32 = jnp.float32
BF16 = jnp.bfloat16
NEG_BIG = -1e30
LOG2E = 1.4426950408889634
EXT_STRIDE = 16


def _const_spec(shape):
    nd = len(shape)
    return pl.BlockSpec(shape, lambda *_: (0,) * nd, pipeline_mode=pl.Buffered(1))


def _sigmoid(x):
    return 1.0 / (1.0 + jnp.exp(-x))


def _head_sums(y2, ones_bd):
    yb = y2.astype(BF16)
    parts = [jnp.dot(yb[:, c * MXU_DIM:(c + 1) * MXU_DIM], ones_bd, preferred_element_type=F32)
             for c in range(WIDTH // MXU_DIM)]
    return jnp.concatenate(parts, axis=1)


def _seg_cumsum(v, pos, longest):
    d = 1
    while d < longest:
        v = v + jnp.where(pos >= d, pltpu.roll(v, d, 0), 0.0)
        d *= 2
    return v


INPROJ_ROW_GROUPS = 2


def _inproj_kernel(tiles_per_seq, x_ref, n1_ref, wg_ref, wf_ref, ws_ref, cw_ref, bias_ref, alog_ref, fqw_ref, fkw_ref,
                   ones_ref, sel_ref, one_ref,
                   gq_ref, gk_ref, gv_ref, gz_ref, fq_ref, fk_ref, fv_ref, fg_ref, sm_ref, kx_ref, qx_ref, gt_ref,
                   tail_ref, carry_ref):
    i = pl.program_id(0)
    tm = x_ref.shape[0]

    @pl.when(i % tiles_per_seq == 0)
    def _():
        tail_ref[:, 0:8, :] = jnp.zeros((3, 8, WIDTH), F32)
        carry_ref[...] = jnp.zeros_like(carry_ref)

    ones_bd = ones_ref[...]
    rows = tm // INPROJ_ROW_GROUPS
    groups = [slice(g * rows, (g + 1) * rows) for g in range(INPROJ_ROW_GROUPS)]

    def project(g):
        x = x_ref[g, :]
        h = (x * lax.rsqrt(jnp.mean(x * x, axis=-1, keepdims=True) + EPS) * n1_ref[...]).astype(BF16)
        y_gates = jnp.dot(h, ws_ref[...], preferred_element_type=F32)
        y_gdn = jnp.dot(h, wg_ref[...], preferred_element_type=F32)
        y_fox = jnp.dot(h, wf_ref[...], preferred_element_type=F32)
        return ([y_gdn[:, s * WIDTH:(s + 1) * WIDTH] for s in range(4)]
                + [y_fox[:, s * WIDTH:(s + 1) * WIDTH] for s in range(4)] + [y_gates])

    def epilogue(g, ys, last):
        for s, out_ref in enumerate((gq_ref, gk_ref, gv_ref)):
            y = ys[s]
            cw = cw_ref[:, s * WIDTH:(s + 1) * WIDTH]
            tail_ref[s, 8 + g.start:8 + g.stop] = y
            acc = y * cw[CONV_K - 1:CONV_K]
            for k in range(1, CONV_K):
                acc = acc + tail_ref[s, 8 + g.start - k:8 + g.stop - k] * cw[CONV_K - 1 - k:CONV_K - k]
            if last:
                tail_ref[s, 0:8] = y[rows - 8:]
            act = acc * _sigmoid(acc)
            if s < 2:
                act = act * lax.rsqrt(_head_sums(act * act, ones_bd) + EPS)
            out_ref[g, :] = act.astype(BF16)

        z = ys[3]
        gz_ref[g, :] = (z * _sigmoid(z)).astype(BF16)

        for s, out_ref, w_norm in ((4, fq_ref, fqw_ref), (5, fk_ref, fkw_ref)):
            y = ys[s]
            ms = _head_sums(y * y, ones_bd) * (1.0 / HEAD_DIM)
            out_ref[g, :] = (y * lax.rsqrt(ms + EPS) * w_norm[...]).astype(BF16)

        fv_ref[g, :] = ys[6].astype(BF16)
        fg_ref[g, :] = _sigmoid(ys[7]).astype(BF16)

        t = ys[8] + bias_ref[...]
        lane = lax.broadcasted_iota(jnp.int32, t.shape, 1)
        beta = _sigmoid(t)
        tail = jnp.log1p(jnp.exp(-jnp.abs(t)))
        g_log = -jnp.exp(alog_ref[...]) * (jnp.maximum(t, 0.0) + tail)
        log_f = -(jnp.maximum(-t, 0.0) + tail)
        row = lax.broadcasted_iota(jnp.int32, t.shape, 0)
        seg_pos = jnp.where(lane < 2 * HEADS, row & (CHUNK - 1), row)
        cum = _seg_cumsum(jnp.where(lane < 2 * HEADS, g_log, log_f), seg_pos, rows)
        f_cum = cum + carry_ref[0:1, :]
        carry_ref[0:1, :] = f_cum[rows - 1:rows, :]
        sm_ref[g, :] = jnp.where(lane < HEADS, beta, jnp.where(lane < 2 * HEADS, cum, f_cum))
        gt_ref[:, g] = cum.T[HEADS:2 * HEADS, :]

        f2 = f_cum * LOG2E
        hi = f2.astype(BF16)
        r1 = f2 - hi.astype(F32)
        mid = r1.astype(BF16)
        lo = (r1 - mid.astype(F32)).astype(BF16)
        ext = (jnp.dot(jnp.concatenate([hi, mid, lo], axis=1), sel_ref[...], preferred_element_type=F32)
               + one_ref[...])
        kx_ref[g, :] = ext[:, :LANES].astype(BF16)
        qx_ref[g, :] = ext[:, LANES:].astype(BF16)

    projected = [project(g) for g in groups]
    for gi, g in enumerate(groups):
        epilogue(g, projected[gi], gi == len(groups) - 1)


def _bias_selectors():
    sel = np.zeros((3 * LANES, 2 * LANES), np.float32)
    one = np.zeros((1, 2 * LANES), np.float32)
    for h in range(HEADS):
        src = 2 * HEADS + h
        for part in range(3):
            sel[part * LANES + src, EXT_STRIDE * h + part] = -1.0
            sel[part * LANES + src, LANES + EXT_STRIDE * h + 3 + part] = 1.0
            one[0, EXT_STRIDE * h + 3 + part] = 1.0
            one[0, LANES + EXT_STRIDE * h + part] = 1.0
    return jnp.asarray(sel, BF16), jnp.asarray(one, F32)


def _inproj_call(x2, n1w, w_gdn, w_fox, w_gates, conv_w, bias_vec, alog_vec, fqw, fkw, ones_bd, *, seq, tm):
    m = x2.shape[0]
    assert seq % tm == 0 and tm % CHUNK == 0 and (tm & (tm - 1)) == 0
    sel, one = _bias_selectors()
    wide = pl.BlockSpec((tm, WIDTH), lambda i: (i, 0))
    narrow = pl.BlockSpec((tm, LANES), lambda i: (i, 0))
    out_shape = ([jax.ShapeDtypeStruct((m, WIDTH), BF16)] * 8 + [jax.ShapeDtypeStruct((m, N_SMALL), F32)]
                 + [jax.ShapeDtypeStruct((m, LANES), BF16)] * 2 + [jax.ShapeDtypeStruct((HEADS, m), F32)])
    return pl.pallas_call(
        functools.partial(_inproj_kernel, seq // tm),
        grid=(m // tm,),
        in_specs=[pl.BlockSpec((tm, D_MODEL), lambda i: (i, 0)),
                  _const_spec(n1w.shape), _const_spec(w_gdn.shape), _const_spec(w_fox.shape),
                  _const_spec(w_gates.shape), _const_spec(conv_w.shape),
                  _const_spec(bias_vec.shape), _const_spec(alog_vec.shape),
                  _const_spec(fqw.shape), _const_spec(fkw.shape), _const_spec(ones_bd.shape),
                  _const_spec(sel.shape), _const_spec(one.shape)],
        out_specs=[wide] * 8 + [narrow] * 3 + [pl.BlockSpec((HEADS, tm), lambda i: (0, i))],
        out_shape=out_shape,
        scratch_shapes=[pltpu.VMEM((3, tm + 8, WIDTH), F32), pltpu.VMEM((8, N_SMALL), F32)],
        compiler_params=pltpu.CompilerParams(dimension_semantics=("arbitrary",),
                                             vmem_limit_bytes=VMEM_LIMIT),
    )(x2, n1w, w_gdn, w_fox, w_gates, conv_w, bias_vec, alog_vec, fqw, fkw, ones_bd, sel, one)


def _lane_lo_mask():
    return lax.broadcasted_iota(jnp.int32, (CHUNK, LANES), 1) < HEAD_DIM


def _block_diag(x, lane_lo):
    xb = x.astype(BF16)
    zero = jnp.zeros((CHUNK, LANES), BF16)
    blocks = []
    for h in range(PACK):
        half = xb[:, (h // 2) * LANES:(h // 2 + 1) * LANES]
        keep = jnp.where(lane_lo if h % 2 == 0 else jnp.logical_not(lane_lo), half, zero)
        blocks.append(jnp.concatenate([keep, zero] if h < 2 else [zero, keep], axis=1))
    return jnp.concatenate(blocks, axis=0)


def _mm(lhs, rhs_bd):
    return jnp.dot(lhs.astype(BF16), rhs_bd, preferred_element_type=F32)


def _expand_heads(sm, first_lane):
    lane_lo = lax.broadcasted_iota(jnp.int32, (sm.shape[0], LANES), 1) < HEAD_DIM
    parts = []
    for p in range(HEADS // 2):
        c = first_lane + 2 * p
        even = jnp.broadcast_to(sm[:, c:c + 1], (sm.shape[0], LANES))
        odd = jnp.broadcast_to(sm[:, c + 1:c + 2], (sm.shape[0], LANES))
        parts.append(jnp.where(lane_lo, even, odd))
    return jnp.concatenate(parts, axis=1)


WY_GROUP_CHUNKS = 8


def _gdn_wy_kernel(q_ref, k_ref, v_ref, sm_ref, gt_ref,
                   u_ref, w_ref, a_ref, qg_ref, kd_ref, dl_ref):
    n_chunks = q_ref.shape[0] // CHUNK
    n_groups = WIDTH // MXU_DIM
    shape = (CHUNK, MXU_DIM)
    row = lax.broadcasted_iota(jnp.int32, shape, 0)
    col = lax.broadcasted_iota(jnp.int32, shape, 1) & (HEAD_DIM - 1)
    causal = row >= col
    strict = row > col
    eye = (row == col).astype(F32)
    bd_mask = _lane_lo_mask()
    scale = HEAD_DIM ** -0.5

    bd = lambda xs: [_block_diag(x, bd_mask) for x in xs]
    mm = lambda ls, rs_: [_mm(a, b) for a, b in zip(ls, rs_)]

    def key_row(ch, g):
        return jnp.concatenate([gt_ref[g * PACK + h:g * PACK + h + 1, ch * CHUNK:(ch + 1) * CHUNK]
                                for h in range(PACK)], axis=1)

    def prepare(ch):
        r = slice(ch * CHUNK, (ch + 1) * CHUNK)
        sm = sm_ref[r, :]
        bx_all = _expand_heads(sm, 0)
        gx_all = _expand_heads(sm, HEADS)
        out = []
        for g in range(n_groups):
            l = slice(g * MXU_DIM, (g + 1) * MXU_DIM)
            q = q_ref[r, l].astype(F32)
            k = k_ref[r, l].astype(F32)
            bx, gx = bx_all[:, l], gx_all[:, l]
            kb = k * bx
            decay = jnp.exp(jnp.where(causal, gx - key_row(ch, g), NEG_BIG))
            s1 = lax.dot_general(jnp.concatenate([kb, q], axis=0).astype(BF16), _block_diag(k, bd_mask),
                                 (((1,), (1,)), ((), ())), preferred_element_type=F32)
            a_ref[r, l] = (s1[CHUNK:] * decay * scale).astype(BF16)
            eg = jnp.exp(gx)
            g_last = gx[CHUNK - 1:CHUNK, :]
            qg_ref[r, l] = (q * eg * scale).astype(BF16)
            kd_ref[r, l] = (k * jnp.exp(g_last - gx)).astype(BF16)
            dl_ref[ch, :, l] = jnp.exp(g_last)
            out.append(dict(r=r, l=l, lmat=jnp.where(strict, s1[:CHUNK] * decay, 0.0),
                            vb=v_ref[r, l].astype(F32) * bx, kbg=kb * eg))
        return out

    def inverse_stages(units):
        st = {}
        lmat = [u["lmat"] for u in units]
        blk8 = (row >> 3) == (col >> 3)

        def base0():
            st["n8"] = [jnp.where(blk8, -l, 0.0) for l in lmat]
            st["t0"] = [eye + n for n in st["n8"]]
            st["p1"] = mm(st["n8"], bd(st["n8"]))

        def base1():
            r = mm([jnp.concatenate([p, t], axis=0) for p, t in zip(st["p1"], st["t0"])], bd(st["p1"]))
            st["p2"] = [x[:CHUNK] for x in r]
            st["t"] = [t + x[CHUNK:] for t, x in zip(st["t0"], r)]

        def base2():
            st["t"] = [t + z for t, z in zip(st["t"], mm(st["t"], bd(st["p2"])))]

        def merge_a(ls):
            off = ((row >> (ls + 1)) == (col >> (ls + 1))) & ((row >> ls) == (col >> ls) + 1)
            st["y"] = mm([jnp.where(off, l, 0.0) for l in lmat], bd(st["t"]))

        def merge_b():
            st["t"] = [t - z for t, z in zip(st["t"], mm(st["t"], bd(st["y"])))]

        def apply(name, ref):
            for u, x in zip(units, mm(st["t"], bd([u[name] for u in units]))):
                ref[u["r"], u["l"]] = x.astype(BF16)

        stages = [base0, base1, base2]
        for ls in (3, 4, 5):
            stages += [functools.partial(merge_a, ls), merge_b]
        return stages + [functools.partial(apply, "vb", u_ref), functools.partial(apply, "kbg", w_ref)]

    groups = [list(range(c, min(c + WY_GROUP_CHUNKS, n_chunks))) for c in range(0, n_chunks, WY_GROUP_CHUNKS)]
    units = [u for ch in groups[0] for u in prepare(ch)]
    for gi in range(len(groups)):
        upcoming = groups[gi + 1] if gi + 1 < len(groups) else []
        stages = inverse_stages(units)
        per_stage = -(-len(upcoming) // len(stages))
        units = []
        for si, stage in enumerate(stages):
            stage()
            for ch in upcoming[si * per_stage:(si + 1) * per_stage]:
                units += prepare(ch)


def _gdn_wy_call(gq, gk, gv, small, g_t, *, rows):
    m = gq.shape[0]
    cpb = rows // CHUNK
    blk = pl.BlockSpec((rows, WIDTH), lambda i: (i, 0))
    rowblk = pl.BlockSpec((cpb, 1, WIDTH), lambda i: (i, 0, 0))
    bf = jax.ShapeDtypeStruct((m, WIDTH), BF16)
    return pl.pallas_call(
        _gdn_wy_kernel,
        grid=(m // rows,),
        in_specs=[blk] * 3 + [pl.BlockSpec((rows, N_SMALL), lambda i: (i, 0)),
                              pl.BlockSpec((HEADS, rows), lambda i: (0, i))],
        out_specs=[blk] * 5 + [rowblk],
        out_shape=[bf] * 5 + [jax.ShapeDtypeStruct((m // CHUNK, 1, WIDTH), F32)],
        compiler_params=pltpu.CompilerParams(dimension_semantics=("parallel",),
                                             vmem_limit_bytes=VMEM_LIMIT),
    )(gq, gk, gv, small, g_t)


SCAN_SEQS = 8


def _gdn_scan_kernel(n_cast, u_ref, w_ref, a_ref, qg_ref, kd_ref, dl_ref, gz_ref, nw_ref, ones_ref, *refs):
    cast_in, o_ref, cast_out = refs[:n_cast], refs[n_cast], refs[n_cast + 1:2 * n_cast + 1]
    s_ref, oacc_ref = refs[2 * n_cast + 1:]
    n_seq, rows, _ = u_ref.shape
    n_groups = WIDTH // MXU_DIM
    chains = [(b, g, slice(g * MXU_DIM, (g + 1) * MXU_DIM)) for b in range(n_seq) for g in range(n_groups)]
    lane_lo = _lane_lo_mask()
    lane_hi = jnp.logical_not(lane_lo)
    zero_half = jnp.zeros((HEAD_DIM, LANES), BF16)

    @pl.when(pl.program_id(1) == 0)
    def _():
        s_ref[...] = jnp.zeros_like(s_ref)

    def pair_lanes(h):
        return slice((h // 2) * LANES, (h // 2 + 1) * LANES)

    def as_block_diag(pieces):
        rows_ = [jnp.concatenate([p.astype(BF16), zero_half] if h < 2 else [zero_half, p.astype(BF16)], axis=1)
                 for h, p in enumerate(pieces)]
        return jnp.concatenate(rows_, axis=0)

    def body(c, carry):
        rs = pl.ds(pl.multiple_of(c * CHUNK, CHUNK), CHUNK)
        state = [[s_ref[b, g, h] for h in range(PACK)] for b, g, _ in chains]
        r = [jnp.dot(jnp.concatenate([w_ref[b, rs, l], qg_ref[b, rs, l]], axis=0),
                     as_block_diag(state[i]), preferred_element_type=F32) for i, (b, _, l) in enumerate(chains)]
        v_new = [u_ref[b, rs, l].astype(F32) - r[i][:CHUNK] for i, (b, _, l) in enumerate(chains)]
        upd = [lax.dot_general(kd_ref[b, rs, l], v_new[i].astype(BF16), (((0,), (0,)), ((), ())),
                               preferred_element_type=F32) for i, (b, _, l) in enumerate(chains)]
        for i, (b, g, l) in enumerate(chains):
            decay = dl_ref[b, c, :, l]
            for h in range(PACK):
                own = upd[i][h * HEAD_DIM:(h + 1) * HEAD_DIM, pair_lanes(h)]
                s_ref[b, g, h] = (state[i][h] * decay[:, pair_lanes(h)]
                                  + jnp.where(lane_lo if h % 2 == 0 else lane_hi, own, 0.0))
        for i, (b, _, l) in enumerate(chains):
            oacc_ref[b, rs, l] = r[i][CHUNK:] + jnp.dot(a_ref[b, rs, l], _block_diag(v_new[i], lane_lo),
                                                        preferred_element_type=F32)
        return carry

    lax.fori_loop(0, rows // CHUNK, body, 0)

    for src, dst in zip(cast_in, cast_out):
        dst[...] = src[...].astype(BF16)
    for b in range(n_seq):
        o = oacc_ref[b]
        ms = _head_sums(o * o, ones_ref[...]) * (1.0 / HEAD_DIM)
        o_ref[b] = (o * lax.rsqrt(ms + EPS) * nw_ref[...] * gz_ref[b].astype(F32)).astype(BF16)


def _gdn_scan_call(u, w, a, qg, kd, dl, gz, nw, ones_bd, weights_f32, *, seq, n_seq, rows):
    batch = u.shape[0] // seq
    steps = seq // rows
    as3d = lambda t: t.reshape(batch, seq, WIDTH)
    blk = pl.BlockSpec((n_seq, rows, WIDTH), lambda b, t: (b, t, 0))
    dl4 = dl.reshape(batch, seq // CHUNK, 1, WIDTH)
    slabs = [pl.BlockSpec((wt.shape[0] // steps, wt.shape[1]), lambda b, t: (t, 0)) for wt in weights_f32]
    assert all(wt.shape[0] % (16 * steps) == 0 for wt in weights_f32)
    outs = pl.pallas_call(
        functools.partial(_gdn_scan_kernel, len(weights_f32)),
        grid=(batch // n_seq, steps),
        in_specs=[blk] * 5 + [pl.BlockSpec((n_seq, rows // CHUNK, 1, WIDTH), lambda b, t: (b, t, 0, 0)), blk,
                              _const_spec(nw.shape), _const_spec(ones_bd.shape)] + slabs,
        out_specs=[blk] + slabs,
        out_shape=[jax.ShapeDtypeStruct((batch, seq, WIDTH), BF16)]
                  + [jax.ShapeDtypeStruct(wt.shape, BF16) for wt in weights_f32],
        scratch_shapes=[pltpu.VMEM((n_seq, WIDTH // MXU_DIM, PACK, HEAD_DIM, LANES), F32),
                        pltpu.VMEM((n_seq, rows, WIDTH), F32)],
        compiler_params=pltpu.CompilerParams(dimension_semantics=("arbitrary", "arbitrary"),
                                             vmem_limit_bytes=VMEM_LIMIT),
    )(as3d(u), as3d(w), as3d(a), as3d(qg), as3d(kd), dl4, as3d(gz), nw, ones_bd, *weights_f32)
    return outs[0].reshape(batch * seq, WIDTH), outs[1:]


FOX_TK = 256
VT_ROWS = HEAD_DIM + 16


def _fox_kernel(tk, q_ref, qx_ref, k_ref, kx_ref, v_ref, gate_ref, o_ref,
                vt_ref, kp_ref, s_ref, m_ref, acc_ref, qt_ref):
    slots = q_ref.shape[0] // tk
    tq = tk
    heads = range(HEADS)
    lane = lax.broadcasted_iota(jnp.int32, (tk, LANES), 1)

    def stage(t):
        rows = slice(t * tk, (t + 1) * tk)
        v_t = v_ref[rows, :].T
        for h in heads:
            vt_ref[h * VT_ROWS:h * VT_ROWS + HEAD_DIM, rows] = v_t[h * HEAD_DIM:(h + 1) * HEAD_DIM]
            vt_ref[h * VT_ROWS + HEAD_DIM:(h + 1) * VT_ROWS, rows] = jnp.ones((VT_ROWS - HEAD_DIM, tk), BF16)
        kx = kx_ref[rows, :].astype(F32)
        for p in range(HEADS // 2):
            pair = k_ref[rows, p * LANES:(p + 1) * LANES].astype(F32)
            for h in (2 * p, 2 * p + 1):
                own = pair if h % 2 == 0 else pltpu.roll(pair, HEAD_DIM, 1)
                ext = pltpu.roll(kx, (HEAD_DIM - EXT_STRIDE * h) % LANES, 1)
                blk = jnp.where(lane < HEAD_DIM, own, jnp.where(lane < HEAD_DIM + EXT_STRIDE, ext, 0.0))
                kp_ref[rows, h * LANES:(h + 1) * LANES] = blk.astype(BF16)
        qx_t = qx_ref[rows, :].T
        for h in heads:
            pair_t = q_ref[rows, (h // 2) * LANES:(h // 2 + 1) * LANES].T
            qt_ref[t, h] = jnp.concatenate(
                [pair_t[(h % 2) * HEAD_DIM:(h % 2 + 1) * HEAD_DIM], qx_t[EXT_STRIDE * h:EXT_STRIDE * (h + 1)],
                 jnp.zeros((LANES - HEAD_DIM - EXT_STRIDE, tq), BF16)], axis=0)
            m_ref[t, h] = jnp.full((1, tq), NEG_BIG, F32)
            acc_ref[t, h] = jnp.zeros((VT_ROWS, tq), F32)

    kv_pos = lax.broadcasted_iota(jnp.int32, (tk, tq), 0)
    q_pos = lax.broadcasted_iota(jnp.int32, (tk, tq), 1)

    def score(h, slot, j, masked):
        k0 = pl.multiple_of(j * tk, tk)
        s_h = jnp.dot(kp_ref[pl.ds(k0, tk), h * LANES:(h + 1) * LANES], qt_ref[slot, h],
                      preferred_element_type=F32)
        if masked:
            s_h = jnp.where(kv_pos <= q_pos, s_h, NEG_BIG)
        s_ref[h] = s_h

    def absorb(h, slot, j):
        k0 = pl.multiple_of(j * tk, tk)
        m_prev = m_ref[slot, h]
        s_h = s_ref[h]
        m_new = jnp.maximum(m_prev, jnp.max(s_h, axis=0, keepdims=True))
        p = jnp.exp2(s_h - m_new).astype(BF16)
        pv = jnp.dot(vt_ref[h * VT_ROWS:(h + 1) * VT_ROWS, pl.ds(k0, tk)], p, preferred_element_type=F32)
        m_ref[slot, h] = m_new
        acc_ref[slot, h] = jnp.exp2(m_prev - m_new) * acc_ref[slot, h] + pv

    stage(0)
    for h in heads:
        score(h, 0, 0, True)
    for slot in range(1, slots):
        stage(slot)
        for h in heads:
            absorb(h, slot - 1, slot - 1)
            score(h, slot, slot, True)

    def step(t, pending):
        p_slot, p_kv = pending
        n_slot = jnp.int32(0)
        n_kv = t
        for s in range(2, slots):
            start = (s * (s - 1)) // 2
            n_slot = jnp.where(t >= start, s, n_slot)
            n_kv = jnp.where(t >= start, t - start, n_kv)
        n_slot = jnp.maximum(n_slot, 1)
        for h in heads:
            absorb(h, p_slot, p_kv)
            score(h, n_slot, n_kv, False)
        return n_slot, n_kv

    n_off = (slots * (slots - 1)) // 2
    p_slot, p_kv = lax.fori_loop(0, n_off, step, (jnp.int32(slots - 1), jnp.int32(slots - 1)))
    for h in heads:
        absorb(h, p_slot, p_kv)
    for slot in range(slots):
        rows = slice(slot * tq, (slot + 1) * tq)
        o_t = jnp.concatenate([acc_ref[slot, h, 0:HEAD_DIM] / acc_ref[slot, h, HEAD_DIM:HEAD_DIM + 1]
                               for h in heads], axis=0)
        o_ref[rows, :] = (o_t.T * gate_ref[rows, :].astype(F32)).astype(BF16)


def _fox_call(fq, qx, fk, kx, fv, gate, *, seq, tk):
    m = fq.shape[0]
    n_tiles = seq // tk
    wide = pl.BlockSpec((seq, WIDTH), lambda b: (b, 0))
    narrow = pl.BlockSpec((seq, LANES), lambda b: (b, 0))
    return pl.pallas_call(
        functools.partial(_fox_kernel, tk),
        grid=(m // seq,),
        in_specs=[wide, narrow, wide, narrow, wide, wide],
        out_specs=wide,
        out_shape=jax.ShapeDtypeStruct((m, WIDTH), BF16),
        scratch_shapes=[pltpu.VMEM((HEADS * VT_ROWS, seq), BF16), pltpu.VMEM((seq, HEADS * LANES), BF16),
                        pltpu.VMEM((HEADS, tk, tk), F32),
                        pltpu.VMEM((n_tiles, HEADS, 1, tk), F32), pltpu.VMEM((n_tiles, HEADS, VT_ROWS, tk), F32),
                        pltpu.VMEM((n_tiles, HEADS, LANES, tk), BF16)],
        compiler_params=pltpu.CompilerParams(dimension_semantics=("parallel",),
                                             vmem_limit_bytes=VMEM_LIMIT),
    )(fq, qx, fk, kx, fv, gate)


FF_CHUNK = 4 * MXU_DIM
FF_SPLITS = tuple((s, min(FF_CHUNK, D_FF - s)) for s in range(0, D_FF, FF_CHUNK))


FFN_ROW_GROUPS = 4


def _ffn_kernel(final, mg_ref, mf_ref, x_ref, wo_ref, n2_ref, wg_ref, wu_ref, wd_ref, fn_ref, o_ref):
    rows = x_ref.shape[0] // FFN_ROW_GROUPS
    groups = [slice(r * rows, (r + 1) * rows) for r in range(FFN_ROW_GROUPS)]
    x1 = [x_ref[g, :]
          + jnp.dot(mg_ref[g, :], wo_ref[0:WIDTH, :], preferred_element_type=F32)
          + jnp.dot(mf_ref[g, :], wo_ref[WIDTH:2 * WIDTH, :], preferred_element_type=F32) for g in groups]
    h = [(v * lax.rsqrt(jnp.mean(v * v, axis=-1, keepdims=True) + EPS) * n2_ref[...]).astype(BF16) for v in x1]
    ffn = [None] * len(groups)
    for start, size in FF_SPLITS:
        gate = [jnp.dot(hv, wg_ref[:, start:start + size], preferred_element_type=F32) for hv in h]
        up = [jnp.dot(hv, wu_ref[:, start:start + size], preferred_element_type=F32) for hv in h]
        act = [(gv * _sigmoid(gv) * uv).astype(BF16) for gv, uv in zip(gate, up)]
        down = [jnp.dot(av, wd_ref[start:start + size, :], preferred_element_type=F32) for av in act]
        ffn = [d if f is None else f + d for f, d in zip(ffn, down)]
    for g, v, f in zip(groups, x1, ffn):
        y = v + f
        if final:
            y = y * lax.rsqrt(jnp.mean(y * y, axis=-1, keepdims=True) + EPS) * fn_ref[...]
        o_ref[g, :] = y


def _ffn_call(mix_g, mix_f, x2, wo, n2w, wg, wu, wd, fnw, *, tm, final):
    m = x2.shape[0]
    half = pl.BlockSpec((tm, WIDTH), lambda i: (i, 0))
    full = pl.BlockSpec((tm, D_MODEL), lambda i: (i, 0))

    def resident(shape):
        return pl.BlockSpec(shape, lambda i: (0, 0), pipeline_mode=pl.Buffered(1))

    return pl.pallas_call(
        functools.partial(_ffn_kernel, final),
        grid=(m // tm,),
        in_specs=[half, half, full, resident(wo.shape), resident(n2w.shape), resident(wg.shape),
                  resident(wu.shape), resident(wd.shape), resident(fnw.shape)],
        out_specs=full,
        out_shape=jax.ShapeDtypeStruct((m, D_MODEL), F32),
        compiler_params=pltpu.CompilerParams(dimension_semantics=("parallel",),
                                             vmem_limit_bytes=VMEM_LIMIT),
    )(mix_g, mix_f, x2, wo, n2w, wg, wu, wd, fnw)


def _lane_vec(parts):
    v = jnp.concatenate([p.astype(F32).reshape(-1) for p in parts])
    return jnp.pad(v, (0, N_SMALL - v.shape[0])).reshape(1, N_SMALL)


def kernel(x, norm1_w, w_in, gdn_conv_w, gdn_A_log, gdn_dt_bias, gdn_out_norm_w, fox_f_bias, fox_q_norm_w, fox_k_norm_w, w_out, norm2_w, w_ffn_gate, w_ffn_up, w_ffn_down, final_norm_w):
    batch, seq, _ = x.shape
    m = batch * seq
    depth = norm1_w.shape[0]
    zeros8 = jnp.zeros((HEADS,), F32)
    r2 = jnp.arange(MXU_DIM) // HEAD_DIM
    ones_bd = (r2[:, None] == r2[None, :]).astype(BF16)

    x2 = x.reshape(m, D_MODEL)
    for l in range(depth):
        w = w_in[l].astype(BF16)
        big = 4 * WIDTH
        g_small = w[:, big:big + 2 * HEADS]
        f_start = big + 2 * HEADS
        f_small = w[:, f_start + big:f_start + big + HEADS]
        w_gdn = w[:, :big]
        w_fox = w[:, f_start:f_start + big]
        w_gates = jnp.concatenate(
            [g_small, f_small, jnp.zeros((D_MODEL, N_SMALL - 3 * HEADS), BF16)], axis=1)
        bias_vec = _lane_vec([zeros8, gdn_dt_bias[l], fox_f_bias[l]])
        alog_vec = _lane_vec([zeros8, gdn_A_log[l], zeros8])
        fqw = (jnp.tile(fox_q_norm_w[l].astype(F32), HEADS) * (HEAD_DIM ** -0.5 * LOG2E)).reshape(1, WIDTH)
        fkw = jnp.tile(fox_k_norm_w[l].astype(F32), HEADS).reshape(1, WIDTH)

        gq, gk, gv, gz, fq, fk, fv, fg, small, kx, qx, g_t = _inproj_call(
            x2, norm1_w[l].reshape(1, D_MODEL), w_gdn, w_fox, w_gates, gdn_conv_w[l].astype(F32), bias_vec, alog_vec,
            fqw, fkw, ones_bd, seq=seq, tm=512)

        u, wmat, amat, qg, kd, dl = _gdn_wy_call(gq, gk, gv, small, g_t, rows=2048)
        nw = jnp.tile(gdn_out_norm_w[l].astype(F32), HEADS).reshape(1, WIDTH)
        mix_g, (wo, wg, wu, wd) = _gdn_scan_call(
            u, wmat, amat, qg, kd, dl, gz, nw, ones_bd,
            [w_out[l], w_ffn_gate[l], w_ffn_up[l], w_ffn_down[l]],
            seq=seq, n_seq=math.gcd(batch, SCAN_SEQS), rows=256)
        mix_f = _fox_call(fq, qx, fk, kx, fv, fg, seq=seq, tk=FOX_TK)

        x2 = _ffn_call(mix_g, mix_f, x2, wo, norm2_w[l].reshape(1, D_MODEL), wg, wu, wd,
                       final_norm_w.reshape(1, D_MODEL), tm=1024, final=(l == depth - 1))
    return x2.reshape(batch, seq, D_MODEL)
```
